```python
import math
import jax, jax.numpy as jnp
from jax import lax
import numpy as np

D_MODEL = 1024
BATCH = 8
SEQ = 2048
DEPTH = 2
DEC_BATCH = 128
DEC_SEQ = 8
PAST_LEN = 8192
PAGE_SIZE = 128

F32 = jnp.float32
RMS_EPS = 1e-6
CONV_WIDTH = 4
SSD_HEADS = 8
SSD_HEAD_DIM = 64
SSD_INNER = SSD_HEADS * SSD_HEAD_DIM
SSD_GROUPS = 2
SSD_STATE = 64
SSD_CONV_DIM = SSD_INNER + 2 * SSD_GROUPS * SSD_STATE
SSD_CHUNK = 64
HG_HEADS = 4
HG_KEY = 128
HG_VAL = 128
HG_INNER = HG_HEADS * HG_VAL
HG_CHUNK = 16
MLA_HEADS = 8
MLA_Q_LORA = 256
MLA_KV_LORA = 256
MLA_NOPE = 64
MLA_ROPE = 32
MLA_V = 64
MLA_INNER = MLA_HEADS * MLA_V
ROPE_THETA = 10000.0
Q_BLOCK = 128
LRU_WIDTH = 512
LRU_BLOCKS = 8
LRU_BLOCK_DIM = LRU_WIDTH // LRU_BLOCKS
LRU_C = 8.0
N_BRANCH = 4
BRANCH_WIDTH = 512
IN_SIZES = (SSD_INNER, SSD_CONV_DIM, SSD_HEADS,
            HG_HEADS * HG_KEY, HG_HEADS * HG_KEY, HG_INNER, HG_INNER,
            MLA_Q_LORA, MLA_KV_LORA, MLA_ROPE,
            LRU_WIDTH, LRU_WIDTH,
            N_BRANCH * D_MODEL)
D_IN = sum(IN_SIZES)
MOE_GROUPS = 4
MOE_EXPERTS_PER_GROUP = 8
MOE_EXPERTS = MOE_GROUPS * MOE_EXPERTS_PER_GROUP
MOE_TOPK = 2
MOE_HIDDEN = 256

kernel_name = 'hybrid_ssd_hgrn2_mla_rglru_hmoe_step'


def rmsnorm(x, g):
    xf = x.astype(F32)
    y = xf * lax.rsqrt(jnp.mean(xf * xf, axis=-1, keepdims=True) + RMS_EPS)
    return (y * g.astype(F32)).astype(x.dtype)


def causal_dwconv(x, buf, w, b):
    xx = jnp.concatenate([buf.astype(x.dtype), x], axis=1)
    y = lax.conv_general_dilated(xx, w[:, None, :].astype(x.dtype), window_strides=(1,), padding='VALID',
                                 dimension_numbers=('NWC', 'WIO', 'NWC'), feature_group_count=x.shape[-1])
    return y + b.astype(x.dtype), xx[:, -(CONV_WIDTH - 1):]


def rope_angles(pos):
    half = MLA_ROPE // 2
    inv = ROPE_THETA ** (-jnp.arange(half, dtype=F32) * 2.0 / MLA_ROPE)
    ang = pos.astype(F32)[:, None] * inv[None, :]
    return jnp.cos(ang), jnp.sin(ang)


def apply_rope(x, cos, sin):
    x1, x2 = jnp.split(x.astype(F32), 2, axis=-1)
    return jnp.concatenate([x1 * cos - x2 * sin, x2 * cos + x1 * sin], axis=-1).astype(x.dtype)


def ssd_chunked(x, dt, a, bm, cm, h0):
    b, L = x.shape[:2]
    cl = SSD_CHUNK if L % SSD_CHUNK == 0 else L
    nc = L // cl
    G, R, P, N = SSD_GROUPS, SSD_HEADS // SSD_GROUPS, SSD_HEAD_DIM, SSD_STATE
    xs = x.astype(F32).reshape(b, nc, cl, G, R, P)
    dts = dt.astype(F32).reshape(b, nc, cl, G, R)
    bs = bm.astype(F32).reshape(b, nc, cl, G, N)
    cs = cm.astype(F32).reshape(b, nc, cl, G, N)
    acs = jnp.cumsum(dts * a.reshape(G, R), axis=2)
    tri = jnp.tril(jnp.ones((cl, cl), bool))[:, :, None, None]
    seg = acs[:, :, :, None] - acs[:, :, None]
    decay = jnp.exp(jnp.where(tri, seg, -jnp.inf))
    cb = jnp.einsum('bctgn,bcsgn->bctsg', cs, bs)
    y_intra = jnp.einsum('bctsg,bctsgr,bcsgr,bcsgrp->bctgrp', cb, decay, dts, xs)
    decay_end = jnp.exp(acs[:, :, -1:] - acs)
    chunk_states = jnp.einsum('bclgn,bclgr,bclgrp->bcgrpn', bs, decay_end * dts, xs)
    chunk_decay = jnp.exp(acs[:, :, -1])

    def step(h, inp):
        st, dc = inp
        return dc[..., None, None] * h + st, h

    h_last, h_prev = lax.scan(step, h0.astype(F32).reshape(b, G, R, P, N),
                              (jnp.moveaxis(chunk_states, 1, 0), jnp.moveaxis(chunk_decay, 1, 0)))
    h_prev = jnp.moveaxis(h_prev, 0, 1)
    y_inter = jnp.einsum('bctgn,bcgrpn,bctgr->bctgrp', cs, h_prev, jnp.exp(acs))
    y = (y_intra + y_inter).reshape(b, L, SSD_HEADS, P)
    return y, h_last.reshape(b, SSD_HEADS, P, N)


def hgrn2_chunked(q, k, v, logf, s0):
    b, L = q.shape[:2]
    cl = HG_CHUNK if L % HG_CHUNK == 0 else L
    nc = L // cl
    tri = jnp.tril(jnp.ones((cl, cl), bool))[:, :, None, None]

    def to_chunks(t):
        return jnp.moveaxis(t.astype(F32).reshape(b, nc, cl, *t.shape[2:]), 1, 0)

    def step(s, inp):
        qc, kc, vc, lc = inp
        bc = jnp.cumsum(lc, axis=1)
        decay = jnp.exp(jnp.where(tri, bc[:, :, None] - bc[:, None], -jnp.inf))
        att = jnp.einsum('bthk,bshk,btshk->bhts', qc, kc, decay)
        o = jnp.einsum('bhts,bshv->bthv', att, vc) + jnp.einsum('bthk,bhkv->bthv', qc * jnp.exp(bc), s)
        s = jnp.exp(bc[:, -1])[..., None] * s + jnp.einsum('bshk,bshv->bhkv', kc * jnp.exp(bc[:, -1:] - bc), vc)
        return s, o

    s_last, o = lax.scan(step, s0.astype(F32), tuple(to_chunks(t) for t in (q, k, v, logf)))
    o = jnp.moveaxis(o, 0, 1).reshape(b, L, HG_HEADS, HG_VAL)
    return o, s_last


def rglru(x, r, i, a_param, pos, h0):
    log_a = -LRU_C * r * jax.nn.softplus(-a_param.astype(F32))
    a = jnp.exp(log_a)
    mult = jnp.sqrt(-jnp.expm1(2.0 * log_a))
    mult = jnp.where((pos == 0)[None, :, None], 1.0, mult)
    u = mult * (i * x.astype(F32))
    u = u.at[:, 0].add(a[:, 0] * h0.astype(F32))
    _, h = lax.associative_scan(lambda c1, c2: (c1[0] * c2[0], c2[0] * c1[1] + c2[1]), (a, u), axis=1)
    return h, h[:, -1]


def latent_attention(q_abs, q_rope, c_kv, k_rope, q_pos, k_pos):
    b, T, H, R = q_abs.shape
    qb = Q_BLOCK if T % Q_BLOCK == 0 else T
    nb = T // qb
    scale = (MLA_NOPE + MLA_ROPE) ** -0.5
    c_kv = c_kv.astype(q_abs.dtype)
    k_rope = k_rope.astype(q_abs.dtype)

    def blocks(t):
        return jnp.moveaxis(t.reshape(b, nb, qb, *t.shape[2:]), 1, 0)

    def one(inp):
        qa, qr, qp = inp
        s = (jnp.einsum('bqhr,bkr->bhqk', qa, c_kv, preferred_element_type=F32)
             + jnp.einsum('bqhd,bkd->bhqk', qr, k_rope, preferred_element_type=F32)) * scale
        s = jnp.where(k_pos[None, :] <= qp[:, None], s, -jnp.inf)
        p = jax.nn.softmax(s, axis=-1)
        return jnp.einsum('bhqk,bkr->bqhr', p.astype(c_kv.dtype), c_kv)

    o = lax.map(one, (blocks(q_abs), blocks(q_rope), q_pos.reshape(nb, qb)))
    return jnp.moveaxis(o, 0, 1).reshape(b, T, H, R)


def token_mixers(h, pos, lp, lb, st):
    past_c, past_kr, ssm0, ssm_conv0, hg0, lru0, lru_conv0 = st
    bsz, T, _ = h.shape
    dtp = h.dtype
    splits = np.cumsum(IN_SIZES)[:-1].tolist()
    (z, xbc, dt_raw, hq, hf, hi, hg, qd, kvd, krr, lx, ly, gl) = jnp.split(h @ lp['w_in'], splits, axis=-1)

    xbc, ssm_conv1 = causal_dwconv(xbc, ssm_conv0, lp['ssd_conv_w'], lp['ssd_conv_b'])
    xbc = jax.nn.silu(xbc)
    xs, bm, cm = jnp.split(xbc, [SSD_INNER, SSD_INNER + SSD_GROUPS * SSD_STATE], axis=-1)
    xs = xs.reshape(bsz, T, SSD_HEADS, SSD_HEAD_DIM)
    dt = jax.nn.softplus(dt_raw.astype(F32) + lp['ssd_dt_bias'].astype(F32))
    a = -jnp.exp(lp['ssd_a_log'].astype(F32))
    y, ssm1 = ssd_chunked(xs, dt, a, bm.reshape(bsz, T, SSD_GROUPS, SSD_STATE),
                          cm.reshape(bsz, T, SSD_GROUPS, SSD_STATE), ssm0)
    y = y + lp['ssd_d'].astype(F32)[:, None] * xs.astype(F32)
    y_a = rmsnorm(y.reshape(bsz, T, SSD_INNER) * jax.nn.silu(z.astype(F32)), lp['ssd_norm_g']).astype(dtp)

    hf32 = hf.astype(F32)
    logf = jnp.logaddexp(jnp.log(lb), jnp.log1p(-lb) + jax.nn.log_sigmoid(hf32))
    k_in = (1.0 - lb) * jax.nn.sigmoid(-hf32)
    shp = (bsz, T, HG_HEADS, HG_KEY)
    o, hg1 = hgrn2_chunked(hq.reshape(shp), k_in.reshape(shp), hi.reshape(bsz, T, HG_HEADS, HG_VAL),
                           logf.reshape(shp), hg0)
    y_b = (rmsnorm(o, lp['hg_norm_g']) * jax.nn.silu(hg.astype(F32).reshape(bsz, T, HG_HEADS, HG_VAL)))
    y_b = y_b.reshape(bsz, T, HG_INNER).astype(dtp)

    cos, sin = rope_angles(pos)
    q = (rmsnorm(qd, lp['mla_q_norm_g']) @ lp['mla_w_uq']).reshape(bsz, T, MLA_HEADS, MLA_NOPE + MLA_ROPE)
    q_nope = q[..., :MLA_NOPE]
    q_rope = apply_rope(q[..., MLA_NOPE:], cos[:, None], sin[:, None])
    c_kv = rmsnorm(kvd, lp['mla_kv_norm_g'])
    k_rope = apply_rope(krr, cos, sin)
    w_ukv = lp['mla_w_ukv'].reshape(MLA_KV_LORA, MLA_HEADS, MLA_NOPE + MLA_V)
    q_abs = jnp.einsum('bthn,rhn->bthr', q_nope, w_ukv[..., :MLA_NOPE])
    if past_c is None:
        keys_c, keys_r = c_kv, k_rope
    else:
        keys_c = jnp.concatenate([past_c.astype(dtp), c_kv], axis=1)
        keys_r = jnp.concatenate([past_kr.astype(dtp), k_rope], axis=1)
    k_pos = jnp.arange(keys_c.shape[1])
    o_lat = latent_attention(q_abs, q_rope, keys_c, keys_r, pos, k_pos)
    y_c = jnp.einsum('bthr,rhv->bthv', o_lat, w_ukv[..., MLA_NOPE:]).reshape(bsz, T, MLA_INNER).astype(dtp)

    xl, lru_conv1 = causal_dwconv(lx, lru_conv0, lp['lru_conv_w'], lp['lru_conv_b'])
    xb = xl.reshape(bsz, T, LRU_BLOCKS, LRU_BLOCK_DIM)
    r = jax.nn.sigmoid(jnp.einsum('btnd,nde->btne', xb, lp['lru_w_r']).reshape(bsz, T, LRU_WIDTH).astype(F32)
                       + lp['lru_b_r'].astype(F32))
    ig = jax.nn.sigmoid(jnp.einsum('btnd,nde->btne', xb, lp['lru_w_i']).reshape(bsz, T, LRU_WIDTH).astype(F32)
                        + lp['lru_b_i'].astype(F32))
    hseq, lru1 = rglru(xl, r, ig, lp['lru_a'], pos, lru0)
    y_d = (hseq * jax.nn.gelu(ly.astype(F32))).astype(dtp)

    branches = jnp.stack([y_a, y_b, y_c, y_d])
    br = jnp.einsum('nbtw,nwd->nbtd', branches, lp['w_branch'])
    gates = jax.nn.sigmoid(gl.reshape(bsz, T, N_BRANCH, D_MODEL))
    merged = jnp.einsum('btnd,nbtd->btd', gates, br)
    out = merged @ lp['w_out']
    new = (c_kv, k_rope, ssm1.astype(dtp), ssm_conv1, hg1.astype(dtp), lru1.astype(dtp), lru_conv1)
    return out, new


def hier_moe(h, lp):
    bsz, T, D = h.shape
    xf = h.reshape(bsz * T, D)
    g_prob = jax.nn.softmax((xf @ lp['moe_w_grp']).astype(F32) + lp['moe_b_grp'].astype(F32), axis=-1)
    g_top, g_idx = lax.top_k(g_prob, 1)
    e_logits = ((xf @ lp['moe_w_rt']).astype(F32) + lp['moe_b_rt'].astype(F32)).reshape(
        -1, MOE_GROUPS, MOE_EXPERTS_PER_GROUP)
    e_prob = jax.nn.softmax(jnp.take_along_axis(e_logits, g_idx[:, :, None], axis=1)[:, 0], axis=-1)
    e_top, e_idx = lax.top_k(e_prob, MOE_TOPK)
    weights = g_top * e_top / jnp.sum(e_top, axis=-1, keepdims=True)
    expert_id = g_idx * MOE_EXPERTS_PER_GROUP + e_idx
    combine = jnp.einsum('nk,nke->ne', weights, jax.nn.one_hot(expert_id, MOE_EXPERTS, dtype=F32)).astype(h.dtype)
    hid = jax.nn.silu(jnp.einsum('nd,edf->nef', xf, lp['moe_w_gate'])) * jnp.einsum('nd,edf->nef', xf, lp['moe_w_up'])
    out = jnp.einsum('nef,ne,efd->nd', hid, combine, lp['moe_w_down'])
    return out.reshape(bsz, T, D)


def decoder_layer(x, c, pos, lp, lb, st):
    bsz = x.shape[0]
    mod = (jax.nn.silu(c) @ lp['w_mod'] + lp['b_mod']).reshape(bsz, 6, D_MODEL)
    sh1, sc1, g1, sh2, sc2, g2 = [mod[:, j, None] for j in range(6)]
    h = rmsnorm(x, lp['norm1_g']) * (1.0 + sc1) + sh1
    y, new = token_mixers(h, pos, lp, lb, st)
    x = x + g1 * y
    h = rmsnorm(x, lp['norm2_g']) * (1.0 + sc2) + sh2
    x = x + g2 * hier_moe(h, lp)
    return x, new


def setup_inputs(seed: int = 0) -> dict:
    key = jax.random.key(seed)
    ks = iter(jax.random.split(key, 64))
    L, D = DEPTH, D_MODEL

    def nrm(shape, scale=1.0):
        return jax.random.normal(next(ks), shape, F32) * scale

    def gain(shape):
        return 1.0 + 0.05 * jax.random.normal(next(ks), shape, F32)

    n_pages = PAST_LEN // PAGE_SIZE
    n_used = DEC_BATCH * n_pages
    n_pool = n_used + max(1, n_used // 4)
    dt0 = jnp.exp(jax.random.uniform(next(ks), (L, SSD_HEADS), F32, math.log(1e-3), math.log(1e-1)))
    ssd_dt_bias = dt0 + jnp.log(-jnp.expm1(-dt0))
    ssd_a_log = jnp.log(jax.random.uniform(next(ks), (L, SSD_HEADS), F32, 1.0, 16.0))
    u = jax.random.uniform(next(ks), (L, LRU_WIDTH), F32, 0.9, 0.999) ** (1.0 / LRU_C)
    lru_a = jnp.log(u) - jnp.log1p(-u)
    page_table = jax.random.permutation(next(ks), n_pool)[:n_used].reshape(DEC_BATCH, n_pages).astype(jnp.int32)
    return {
        'x_prompt': nrm((BATCH, SEQ, D)),
        'x_sample': nrm((DEC_BATCH, DEC_SEQ, D)),
        'c_prompt': nrm((BATCH, D)),
        'c_sample': nrm((DEC_BATCH, D)),
        'cache_kv_latent': nrm((L, n_pool, PAGE_SIZE, MLA_KV_LORA)),
        'cache_k_rope': nrm((L, n_pool, PAGE_SIZE, MLA_ROPE)),
        'state_ssm': nrm((L, DEC_BATCH, SSD_HEADS, SSD_HEAD_DIM, SSD_STATE), 0.1),
        'state_ssm_conv': nrm((L, DEC_BATCH, CONV_WIDTH - 1, SSD_CONV_DIM)),
        'state_hgrn': nrm((L, DEC_BATCH, HG_HEADS, HG_KEY, HG_VAL), 0.1),
        'state_lru': nrm((L, DEC_BATCH, LRU_WIDTH), 0.5),
        'state_lru_conv': nrm((L, DEC_BATCH, CONV_WIDTH - 1, LRU_WIDTH)),
        'page_table': page_table,
        'norm1_g': gain((L, D)),
        'norm2_g': gain((L, D)),
        'w_mod': nrm((L, D, 6 * D), 0.5 * D ** -0.5),
        'b_mod': nrm((L, 6 * D), 0.02),
        'w_in': nrm((L, D, D_IN), D ** -0.5),
        'ssd_conv_w': nrm((L, CONV_WIDTH, SSD_CONV_DIM), CONV_WIDTH ** -0.5),
        'ssd_conv_b': nrm((L, SSD_CONV_DIM), 0.02),
        'ssd_dt_bias': ssd_dt_bias,
        'ssd_a_log': ssd_a_log,
        'ssd_d': gain((L, SSD_HEADS)),
        'ssd_norm_g': gain((L, SSD_INNER)),
        'hg_lb_raw': nrm((L, HG_HEADS * HG_KEY)),
        'hg_norm_g': gain((L, HG_VAL)),
        'mla_q_norm_g': gain((L, MLA_Q_LORA)),
        'mla_w_uq': nrm((L, MLA_Q_LORA, MLA_HEADS * (MLA_NOPE + MLA_ROPE)), MLA_Q_LORA ** -0.5),
        'mla_kv_norm_g': gain((L, MLA_KV_LORA)),
        'mla_w_ukv': nrm((L, MLA_KV_LORA, MLA_HEADS * (MLA_NOPE + MLA_V)), MLA_KV_LORA ** -0.5),
        'lru_conv_w': nrm((L, CONV_WIDTH, LRU_WIDTH), CONV_WIDTH ** -0.5),
        'lru_conv_b': nrm((L, LRU_WIDTH), 0.02),
        'lru_w_r': nrm((L, LRU_BLOCKS, LRU_BLOCK_DIM, LRU_BLOCK_DIM), LRU_BLOCK_DIM ** -0.5),
        'lru_b_r': nrm((L, LRU_WIDTH), 0.02),
        'lru_w_i': nrm((L, LRU_BLOCKS, LRU_BLOCK_DIM, LRU_BLOCK_DIM), LRU_BLOCK_DIM ** -0.5),
        'lru_b_i': nrm((L, LRU_WIDTH), 0.02),
        'lru_a': lru_a,
        'w_branch': nrm((L, N_BRANCH, BRANCH_WIDTH, D), BRANCH_WIDTH ** -0.5),
        'w_out': nrm((L, D, D), D ** -0.5),
        'moe_w_grp': nrm((L, D, MOE_GROUPS), D ** -0.5),
        'moe_b_grp': nrm((L, MOE_GROUPS), 0.01),
        'moe_w_rt': nrm((L, D, MOE_EXPERTS), D ** -0.5),
        'moe_b_rt': nrm((L, MOE_EXPERTS), 0.01),
        'moe_w_gate': nrm((L, MOE_EXPERTS, D, MOE_HIDDEN), D ** -0.5),
        'moe_w_up': nrm((L, MOE_EXPERTS, D, MOE_HIDDEN), D ** -0.5),
        'moe_w_down': nrm((L, MOE_EXPERTS, MOE_HIDDEN, D), MOE_HIDDEN ** -0.5),
        'final_norm_g': gain((D,)),
    }


def reference(x_prompt, x_sample, c_prompt, c_sample,
              cache_kv_latent, cache_k_rope, state_ssm, state_ssm_conv, state_hgrn, state_lru, state_lru_conv,
              page_table,
              norm1_g, norm2_g, w_mod, b_mod, w_in,
              ssd_conv_w, ssd_conv_b, ssd_dt_bias, ssd_a_log, ssd_d, ssd_norm_g,
              hg_lb_raw, hg_norm_g,
              mla_q_norm_g, mla_w_uq, mla_kv_norm_g, mla_w_ukv,
              lru_conv_w, lru_conv_b, lru_w_r, lru_b_r, lru_w_i, lru_b_i, lru_a,
              w_branch, w_out,
              moe_w_grp, moe_b_grp, moe_w_rt, moe_b_rt, moe_w_gate, moe_w_up, moe_w_down,
              final_norm_g):
    bp, tp = x_prompt.shape[:2]
    bs, ts = x_sample.shape[:2]
    n_past = page_table.shape[1] * cache_kv_latent.shape[2]
    pos_p = jnp.arange(tp)
    pos_s = n_past + jnp.arange(ts)
    lb_all = jnp.cumsum(jax.nn.softmax(hg_lb_raw.astype(F32), axis=0), axis=0)
    lb_all = lb_all - lb_all[:1]
    dtp = x_prompt.dtype
    yp, ys = x_prompt, x_sample
    p_new, s_new = [], []
    for l in range(DEPTH):
        lp = {
            'norm1_g': norm1_g[l], 'norm2_g': norm2_g[l], 'w_mod': w_mod[l], 'b_mod': b_mod[l], 'w_in': w_in[l],
            'ssd_conv_w': ssd_conv_w[l], 'ssd_conv_b': ssd_conv_b[l], 'ssd_dt_bias': ssd_dt_bias[l],
            'ssd_a_log': ssd_a_log[l], 'ssd_d': ssd_d[l], 'ssd_norm_g': ssd_norm_g[l],
            'hg_norm_g': hg_norm_g[l],
            'mla_q_norm_g': mla_q_norm_g[l], 'mla_w_uq': mla_w_uq[l],
            'mla_kv_norm_g': mla_kv_norm_g[l], 'mla_w_ukv': mla_w_ukv[l],
            'lru_conv_w': lru_conv_w[l], 'lru_conv_b': lru_conv_b[l], 'lru_w_r': lru_w_r[l], 'lru_b_r': lru_b_r[l],
            'lru_w_i': lru_w_i[l], 'lru_b_i': lru_b_i[l], 'lru_a': lru_a[l],
            'w_branch': w_branch[l], 'w_out': w_out[l],
            'moe_w_grp': moe_w_grp[l], 'moe_b_grp': moe_b_grp[l], 'moe_w_rt': moe_w_rt[l], 'moe_b_rt': moe_b_rt[l],
            'moe_w_gate': moe_w_gate[l], 'moe_w_up': moe_w_up[l], 'moe_w_down': moe_w_down[l],
        }
        p_st = (None, None,
                jnp.zeros((bp, SSD_HEADS, SSD_HEAD_DIM, SSD_STATE), dtp),
                jnp.zeros((bp, CONV_WIDTH - 1, SSD_CONV_DIM), dtp),
                jnp.zeros((bp, HG_HEADS, HG_KEY, HG_VAL), dtp),
                jnp.zeros((bp, LRU_WIDTH), dtp),
                jnp.zeros((bp, CONV_WIDTH - 1, LRU_WIDTH), dtp))
        s_st = (cache_kv_latent[l, page_table].reshape(bs, n_past, MLA_KV_LORA),
                cache_k_rope[l, page_table].reshape(bs, n_past, MLA_ROPE),
                state_ssm[l], state_ssm_conv[l], state_hgrn[l], state_lru[l], state_lru_conv[l])
        yp, pn = decoder_layer(yp, c_prompt, pos_p, lp, lb_all[l], p_st)
        ys, sn = decoder_layer(ys, c_sample, pos_s, lp, lb_all[l], s_st)
        p_new.append(pn)
        s_new.append(sn)
    yp = rmsnorm(yp, final_norm_g)
    ys = rmsnorm(ys, final_norm_g)

    def stk(news, j):
        return jnp.stack([n[j] for n in news])

    return (yp, ys,
            stk(p_new, 0), stk(p_new, 1), stk(p_new, 2), stk(p_new, 3), stk(p_new, 4), stk(p_new, 5), stk(p_new, 6),
            stk(s_new, 0), stk(s_new, 1), stk(s_new, 2), stk(s_new, 3), stk(s_new, 4), stk(s_new, 5), stk(s_new, 6))
```

```python
import functools
import math

import jax
import jax.numpy as jnp
from jax import lax
from jax.experimental import pallas as pl
from jax.experimental.pallas import tpu as pltpu

F32 = jnp.float32
BF16 = jnp.bfloat16
HI = lax.Precision.HIGHEST
NEG = -1e30

RMS_EPS = 1e-6
D_MODEL = 1024
CONV_W = 4
SSD_H, SSD_P, SSD_G, SSD_N = 8, 64, 2, 64
SSD_INNER = SSD_H * SSD_P
SSD_CONV = SSD_INNER + 2 * SSD_G * SSD_N
SSD_CHUNK = 64
HG_H, HG_K, HG_V = 4, 128, 128
HG_INNER = HG_H * HG_V
HG_CHUNK = 16
MLA_H, MLA_QL, MLA_KVL, MLA_NOPE, MLA_ROPE, MLA_V = 8, 256, 256, 64, 32, 64
ROPE_THETA = 10000.0
LRU_W, LRU_NB = 512, 8
LRU_C = 8.0
N_BRANCH = 4
MOE_G, MOE_EPG, MOE_E, MOE_HID = 4, 8, 32, 256

V7X_VMEM_LIMIT = 56 * 1024 * 1024
LANE = 128

P_GL, P_Z, P_HQ, P_HF, P_HI, P_HG, P_LX, P_LY = 0, 4096, 4608, 5120, 5632, 6144, 6656, 7168
P_XBC, P_QD, P_KVD, P_DT, P_KR = 7680, 8448, 8704, 8960, 9088
P_TOTAL = 9216


def _cparams(sem):
    return pltpu.CompilerParams(dimension_semantics=sem, vmem_limit_bytes=V7X_VMEM_LIMIT)


def _silu(x):
    return x * (1.0 / (1.0 + jnp.exp(-x)))


def _sigmoid(x):
    return 1.0 / (1.0 + jnp.exp(-x))


def _softplus(x):
    return jnp.maximum(x, 0.0) + jnp.log(1.0 + jnp.exp(-jnp.abs(x)))


def _iota(shape, dim):
    return lax.broadcasted_iota(jnp.int32, shape, dim)


def _dot(a, b):
    return jnp.dot(a.astype(BF16), b.astype(BF16), preferred_element_type=F32)


def _dot_nt(a, b):
    return lax.dot_general(a.astype(BF16), b.astype(BF16), (((1,), (1,)), ((), ())), preferred_element_type=F32)


def _dot_tn(a, b):
    return lax.dot_general(a.astype(BF16), b.astype(BF16), (((0,), (0,)), ((), ())), preferred_element_type=F32)


def _dot_hi(a, b):
    return jnp.dot(a, b, precision=HI, preferred_element_type=F32)


def _dot_nt_hi(a, b):
    return lax.dot_general(a, b, (((1,), (1,)), ((), ())), precision=HI, preferred_element_type=F32)


def _tril(n):
    return (_iota((n, n), 0) >= _iota((n, n), 1)).astype(F32)


def _eye(n):
    return (_iota((n, n), 0) == _iota((n, n), 1)).astype(F32)


def _row_blocks(b, t, target):
    if t >= target:
        return 1, target
    return min(b, target // t), t


def _mod_kernel(c_ref, w_ref, b_ref, o_ref):
    o_ref[...] = _dot(_silu(c_ref[...]), w_ref[...]) + b_ref[...]


def _mod_call(c, w, b):
    m, d = c.shape
    n = w.shape[1]
    tn = 1536
    return pl.pallas_call(
        _mod_kernel,
        grid=(n // tn,),
        in_specs=[pl.BlockSpec((m, d), lambda j: (0, 0)),
                  pl.BlockSpec((d, tn), lambda j: (0, j)),
                  pl.BlockSpec((1, tn), lambda j: (0, j))],
        out_specs=pl.BlockSpec((m, tn), lambda j: (0, j)),
        out_shape=jax.ShapeDtypeStruct((m, n), F32),
        compiler_params=_cparams(("parallel",)),
        name="adaln_mod",
    )(c, w, b.reshape(1, n))


def _prenorm_kernel(x_ref, g_ref, mod_ref, o_ref, *, sh_row, sc_row):
    x = x_ref[...]
    bb, tt, d = x.shape
    y = x * lax.rsqrt(jnp.mean(x * x, axis=-1, keepdims=True) + RMS_EPS) * g_ref[...]
    mod = mod_ref[...]
    y = y * (1.0 + mod[:, sc_row:sc_row + 1, :]) + mod[:, sh_row:sh_row + 1, :]
    o_ref[...] = y.reshape(bb * tt, d).astype(o_ref.dtype)


def _prenorm_call(x, g, mod, sh_row, sc_row, out_dtype=BF16):
    b, t, d = x.shape
    bb, tt = _row_blocks(b, t, 512)
    nt = t // tt
    return pl.pallas_call(
        functools.partial(_prenorm_kernel, sh_row=sh_row, sc_row=sc_row),
        grid=(b // bb, nt),
        in_specs=[pl.BlockSpec((bb, tt, d), lambda i, j: (i, j, 0)),
                  pl.BlockSpec((1, 1, d), lambda i, j: (0, 0, 0)),
                  pl.BlockSpec((bb, mod.shape[1], d), lambda i, j: (i, 0, 0))],
        out_specs=pl.BlockSpec((bb * tt, d), lambda i, j: (i * nt + j, 0)),
        out_shape=jax.ShapeDtypeStruct((b * t, d), out_dtype),
        compiler_params=_cparams(("parallel", "parallel")),
        name="prenorm",
    )(x, g.reshape(1, 1, d), mod)


def _matmul_kernel(x_ref, w_ref, o_ref):
    o_ref[...] = jnp.dot(x_ref[...], w_ref[...], preferred_element_type=F32)


def _matmul_call(x, w, tm=512, tn=1024):
    m, k = x.shape
    n = w.shape[1]
    tm = min(tm, m)
    return pl.pallas_call(
        _matmul_kernel,
        grid=(n // tn, m // tm),
        in_specs=[pl.BlockSpec((tm, k), lambda j, i: (i, 0)),
                  pl.BlockSpec((k, tn), lambda j, i: (0, j))],
        out_specs=pl.BlockSpec((tm, tn), lambda j, i: (i, j)),
        out_shape=jax.ShapeDtypeStruct((m, n), F32),
        compiler_params=_cparams(("parallel", "parallel")),
        name="in_proj",
    )(x, w)


def _pack_w_in(w_in):
    d = w_in.shape[0]
    sizes = (SSD_INNER, SSD_CONV, SSD_H, HG_H * HG_K, HG_H * HG_K, HG_INNER, HG_INNER,
             MLA_QL, MLA_KVL, MLA_ROPE, LRU_W, LRU_W, N_BRANCH * D_MODEL)
    offs = [0]
    for s in sizes:
        offs.append(offs[-1] + s)
    z, xbc, dt, hq, hf, hi, hg, qd, kvd, krr, lx, ly, gl = [w_in[:, offs[i]:offs[i + 1]] for i in range(13)]
    pad = lambda a: jnp.pad(a, ((0, 0), (0, LANE - a.shape[1])))
    return jnp.concatenate([gl, z, hq, hf, hi, hg, lx, ly, xbc, qd, kvd, pad(dt), pad(krr)], axis=1).astype(BF16)


def _causal_conv(xx_ref, x, w_ref, b_ref, tt):
    xx_ref[8:8 + tt, :] = x
    w = w_ref[...]
    y = b_ref[...] + w[3:4, :] * x
    for k in range(1, CONV_W):
        y = y + w[3 - k:4 - k, :] * xx_ref[8 - k:8 - k + tt, :]
    return y


def _ssd_kernel(z_ref, xbc_ref, dt_ref, cw_ref, cb_ref, dtb_ref, alog_ref, dfull_ref, ng_ref, s0_ref, c0_ref,
                y_ref, s1_ref, c1_ref, st_ref, xx_ref, yc_ref, *, tt, cl):
    t = pl.program_id(1)
    nt = pl.num_programs(1)
    hp = SSD_H // SSD_G * SSD_P
    eye = _eye(SSD_N)

    @pl.when(t == 0)
    def _():
        xx_ref[0:8, :] = jnp.zeros((8, SSD_CONV), F32)
        xx_ref[5:8, :] = c0_ref[0]
        for h in range(SSD_H):
            g, r = divmod(h, SSD_H // SSD_G)
            st_ref[g, :, r * SSD_P:(r + 1) * SSD_P] = _dot_nt_hi(eye, s0_ref[0, h])

    conv = _causal_conv(xx_ref, xbc_ref[...], cw_ref, cb_ref, tt)
    new_hist = xx_ref[tt:tt + 8, :]
    xx_ref[0:8, :] = new_hist
    act = _silu(conv)
    xs = act[:, :SSD_INNER]
    bm = act[:, SSD_INNER:SSD_INNER + SSD_G * SSD_N]
    cm = act[:, SSD_INNER + SSD_G * SSD_N:]
    dt = _softplus(dt_ref[...] + dtb_ref[...])
    da = dt * (-jnp.exp(alog_ref[...]))
    expand = (_iota((LANE, SSD_INNER), 1) // SSD_P == _iota((LANE, SSD_INNER), 0)).astype(F32)
    dtf = _dot_hi(dt, expand)
    tril = _tril(cl)
    tri = _iota((cl, cl), 0) >= _iota((cl, cl), 1)
    lane = _iota((cl, LANE), 1)

    for c in range(tt // cl):
        rows = slice(c * cl, (c + 1) * cl)
        acs = _dot_hi(tril, da[rows])
        acs_last = acs[cl - 1:cl, :]
        eacs = jnp.exp(_dot_hi(acs, expand))
        dend = jnp.exp(_dot_hi(acs_last - acs, expand))
        x_c = xs[rows]
        dt_c = dtf[rows]
        xdt = x_c * dt_c
        xsc = xdt * dend
        for g in range(SSD_G):
            bg = bm[rows, g * SSD_N:(g + 1) * SSD_N]
            cg = cm[rows, g * SSD_N:(g + 1) * SSD_N]
            cb = _dot_nt(cg, bg)
            sg = st_ref[g]
            y_inter = _dot(cg, sg) * eacs[:, g * hp:(g + 1) * hp]
            for r in range(SSD_H // SSD_G):
                h = g * (SSD_H // SSD_G) + r
                sel = (lane == h).astype(F32)
                u = acs * sel
                seg = _dot_nt_hi(u, sel) - _dot_nt_hi(sel, u)
                decay = jnp.exp(jnp.where(tri, seg, NEG))
                y_h = _dot(cb * decay, xdt[:, h * SSD_P:(h + 1) * SSD_P])
                yc_ref[rows, h * SSD_P:(h + 1) * SSD_P] = y_h + y_inter[:, r * SSD_P:(r + 1) * SSD_P]
            cdec = eacs[cl - 1:cl, g * hp:(g + 1) * hp]
            st_ref[g] = cdec * sg + _dot_tn(bg, xsc[:, g * hp:(g + 1) * hp])

    y = yc_ref[...] + dfull_ref[...] * xs
    yz = y * _silu(z_ref[...])
    out = yz * lax.rsqrt(jnp.mean(yz * yz, axis=-1, keepdims=True) + RMS_EPS) * ng_ref[...]
    y_ref[...] = out.astype(y_ref.dtype)

    @pl.when(t == nt - 1)
    def _():
        c1_ref[0] = new_hist[5:8, :]
        for h in range(SSD_H):
            g, r = divmod(h, SSD_H // SSD_G)
            s1_ref[0, h] = _dot_nt_hi(eye, st_ref[g, :, r * SSD_P:(r + 1) * SSD_P])


def _ssd_call(p, b, t, conv_w, conv_b, dt_bias, a_log, d_skip, norm_g, s0, c0, out_dtype):
    tt = min(t, 256)
    cl = SSD_CHUNK if tt % SSD_CHUNK == 0 else tt
    nt = t // tt
    padl = lambda v: jnp.pad(v, (0, LANE - v.shape[0])).reshape(1, LANE)
    row = lambda i, j: i * nt + j
    const2 = lambda i, j: (0, 0)
    return pl.pallas_call(
        functools.partial(_ssd_kernel, tt=tt, cl=cl),
        grid=(b, nt),
        in_specs=[pl.BlockSpec((tt, SSD_INNER), lambda i, j: (row(i, j), P_Z // SSD_INNER)),
                  pl.BlockSpec((tt, SSD_CONV), lambda i, j: (row(i, j), P_XBC // SSD_CONV)),
                  pl.BlockSpec((tt, LANE), lambda i, j: (row(i, j), P_DT // LANE)),
                  pl.BlockSpec((CONV_W, SSD_CONV), const2),
                  pl.BlockSpec((1, SSD_CONV), const2),
                  pl.BlockSpec((1, LANE), const2),
                  pl.BlockSpec((1, LANE), const2),
                  pl.BlockSpec((1, SSD_INNER), const2),
                  pl.BlockSpec((1, SSD_INNER), const2),
                  pl.BlockSpec((1, SSD_H, SSD_P, SSD_N), lambda i, j: (i, 0, 0, 0)),
                  pl.BlockSpec((1, CONV_W - 1, SSD_CONV), lambda i, j: (i, 0, 0))],
        out_specs=[pl.BlockSpec((tt, SSD_INNER), lambda i, j: (row(i, j), 0)),
                   pl.BlockSpec((1, SSD_H, SSD_P, SSD_N), lambda i, j: (i, 0, 0, 0)),
                   pl.BlockSpec((1, CONV_W - 1, SSD_CONV), lambda i, j: (i, 0, 0))],
        out_shape=[jax.ShapeDtypeStruct((b * t, SSD_INNER), out_dtype),
                   jax.ShapeDtypeStruct((b, SSD_H, SSD_P, SSD_N), F32),
                   jax.ShapeDtypeStruct((b, CONV_W - 1, SSD_CONV), F32)],
        scratch_shapes=[pltpu.VMEM((SSD_G, SSD_N, SSD_H // SSD_G * SSD_P), F32),
                        pltpu.VMEM((tt + 8, SSD_CONV), F32),
                        pltpu.VMEM((tt, SSD_INNER), F32)],
        compiler_params=_cparams(("parallel", "arbitrary")),
        name="ssd",
    )(p, p, p, conv_w, conv_b.reshape(1, -1), padl(dt_bias), padl(a_log),
      jnp.repeat(d_skip, SSD_P).reshape(1, -1), norm_g.reshape(1, -1), s0, c0)


def _hgrn_kernel(q_ref, f_ref, i_ref, g_ref, loglb_ref, log1m_ref, onem_ref, ng_ref, s0_ref,
                 y_ref, s1_ref, st_ref, *, tt, cl):
    t = pl.program_id(1)
    nt = pl.num_programs(1)
    eye = _eye(HG_K)
    hs = [slice(h * HG_K, (h + 1) * HG_K) for h in range(HG_H)]

    @pl.when(t == 0)
    def _():
        for h in range(HG_H):
            st_ref[h] = _dot_nt_hi(eye, s0_ref[0, h])

    tril = _tril(cl)
    rowi = _iota((cl, HG_INNER), 0)
    loglb, log1m, onem, ng = loglb_ref[...], log1m_ref[...], onem_ref[...], ng_ref[...]

    def chunk(c, carry):
        rows = pl.ds(pl.multiple_of(c * cl, cl), cl)
        q, hf, v = q_ref[rows, :], f_ref[rows, :], i_ref[rows, :]
        b_ = log1m - _softplus(-hf)
        logf = jnp.maximum(loglb, b_) + jnp.log(1.0 + jnp.exp(-jnp.abs(loglb - b_)))
        kin = onem * _sigmoid(-hf)
        bc = _dot_hi(tril, logf)
        bl = bc[cl - 1:cl, :]
        p = q * kin
        o = [jnp.sum(p[:, s], axis=-1, keepdims=True) * v[:, s] for s in hs]
        for d in range(1, cl):
            kd, bd, vd = pltpu.roll(kin, d, 0), pltpu.roll(bc, d, 0), pltpu.roll(v, d, 0)
            p = q * kd * jnp.exp(jnp.where(rowi >= d, bc - bd, NEG))
            o = [o[h] + jnp.sum(p[:, s], axis=-1, keepdims=True) * vd[:, s] for h, s in enumerate(hs)]
        qe = q * jnp.exp(bc)
        ke = kin * jnp.exp(bl - bc)
        outs = []
        for h, s in enumerate(hs):
            st = st_ref[h]
            oh = o[h] + _dot_nt(qe[:, s], st)
            st_ref[h] = jnp.exp(bl[:, s]) * st + _dot_tn(v[:, s], ke[:, s])
            outs.append(oh * lax.rsqrt(jnp.mean(oh * oh, axis=-1, keepdims=True) + RMS_EPS) * ng)
        y = jnp.concatenate(outs, axis=-1) * _silu(g_ref[rows, :])
        y_ref[rows, :] = y.astype(y_ref.dtype)
        return carry

    lax.fori_loop(0, tt // cl, chunk, 0)

    @pl.when(t == nt - 1)
    def _():
        for h in range(HG_H):
            s1_ref[0, h] = _dot_nt_hi(eye, st_ref[h])


def _hgrn_call(p, b, t, lb, norm_g, s0, out_dtype):
    tt = min(t, 256)
    cl = HG_CHUNK if tt % HG_CHUNK == 0 else tt
    nt = t // tt
    row = lambda i, j: i * nt + j
    const2 = lambda i, j: (0, 0)
    col = lambda off: (lambda i, j: (row(i, j), off // HG_INNER))
    vec = pl.BlockSpec((1, HG_INNER), const2)
    return pl.pallas_call(
        functools.partial(_hgrn_kernel, tt=tt, cl=cl),
        grid=(b, nt),
        in_specs=[pl.BlockSpec((tt, HG_INNER), col(P_HQ)), pl.BlockSpec((tt, HG_INNER), col(P_HF)),
                  pl.BlockSpec((tt, HG_INNER), col(P_HI)), pl.BlockSpec((tt, HG_INNER), col(P_HG)),
                  vec, vec, vec, pl.BlockSpec((1, HG_V), const2),
                  pl.BlockSpec((1, HG_H, HG_K, HG_V), lambda i, j: (i, 0, 0, 0))],
        out_specs=[pl.BlockSpec((tt, HG_INNER), lambda i, j: (row(i, j), 0)),
                   pl.BlockSpec((1, HG_H, HG_K, HG_V), lambda i, j: (i, 0, 0, 0))],
        out_shape=[jax.ShapeDtypeStruct((b * t, HG_INNER), out_dtype),
                   jax.ShapeDtypeStruct((b, HG_H, HG_K, HG_V), F32)],
        scratch_shapes=[pltpu.VMEM((HG_H, HG_V, HG_K), F32)],
        compiler_params=_cparams(("parallel", "arbitrary")),
        name="hgrn2",
    )(p, p, p, p, jnp.log(lb).reshape(1, -1), jnp.log1p(-lb).reshape(1, -1), (1.0 - lb).reshape(1, -1),
      norm_g.reshape(1, -1), s0)


def _gelu_tanh(x):
    return 0.5 * x * (1.0 + jnp.tanh(math.sqrt(2.0 / math.pi) * (x + 0.044715 * (x * x * x))))


def _lru_kernel(lx_ref, ly_ref, cw_ref, cb_ref, wr_ref, br_ref, wi_ref, bi_ref, ap_ref, h0_ref, c0_ref,
                y_ref, h1_ref, c1_ref, carry_ref, xx_ref, hs_ref, *, tt, pos0):
    t = pl.program_id(1)
    nt = pl.num_programs(1)

    @pl.when(t == 0)
    def _():
        xx_ref[0:8, :] = jnp.zeros((8, LRU_W), F32)
        xx_ref[5:8, :] = c0_ref[0]
        carry_ref[...] = h0_ref[0]

    xl = _causal_conv(xx_ref, lx_ref[...], cw_ref, cb_ref, tt)
    new_hist = xx_ref[tt:tt + 8, :]
    xx_ref[0:8, :] = new_hist
    r = _sigmoid(_dot(xl, wr_ref[...]) + br_ref[...])
    ig = _sigmoid(_dot(xl, wi_ref[...]) + bi_ref[...])
    log_a = -LRU_C * r * _softplus(-ap_ref[...])
    a = jnp.exp(log_a)
    mult = jnp.sqrt(jnp.tanh(-log_a) * (a * a + 1.0))
    pos = pos0 + t * tt + _iota((tt, LRU_W), 0)
    u = jnp.where(pos == 0, 1.0, mult) * (ig * xl)
    row8 = _iota((8, LRU_W), 0)
    carry = carry_ref[...]
    for gi in range(tt // 8):
        a8, u8 = a[gi * 8:(gi + 1) * 8], u[gi * 8:(gi + 1) * 8]
        for k in (1, 2, 4):
            m = row8 >= k
            u8, a8 = (jnp.where(m, a8 * pltpu.roll(u8, k, 0) + u8, u8),
                      jnp.where(m, a8 * pltpu.roll(a8, k, 0), a8))
        h8 = u8 + a8 * carry
        carry = h8[7:8, :]
        hs_ref[gi * 8:(gi + 1) * 8, :] = h8
    carry_ref[...] = carry
    y_ref[...] = (hs_ref[...] * _gelu_tanh(ly_ref[...])).astype(y_ref.dtype)

    @pl.when(t == nt - 1)
    def _():
        h1_ref[0] = carry
        c1_ref[0] = new_hist[5:8, :]


def _block_diag(w):
    nb, di, do = w.shape
    eye = jnp.eye(nb, dtype=w.dtype)
    return (eye[:, None, :, None] * w[:, :, None, :]).reshape(nb * di, nb * do)


def _lru_call(p, b, t, pos0, conv_w, conv_b, w_r, b_r, w_i, b_i, a_param, h0, c0, out_dtype):
    tt = min(t, 256)
    nt = t // tt
    row = lambda i, j: i * nt + j
    const2 = lambda i, j: (0, 0)
    vec = pl.BlockSpec((1, LRU_W), const2)
    mat = pl.BlockSpec((LRU_W, LRU_W), const2)
    y, h1, c1 = pl.pallas_call(
        functools.partial(_lru_kernel, tt=tt, pos0=pos0),
        grid=(b, nt),
        in_specs=[pl.BlockSpec((tt, LRU_W), lambda i, j: (row(i, j), P_LX // LRU_W)),
                  pl.BlockSpec((tt, LRU_W), lambda i, j: (row(i, j), P_LY // LRU_W)),
                  pl.BlockSpec((CONV_W, LRU_W), const2), vec, mat, vec, mat, vec, vec,
                  pl.BlockSpec((1, 1, LRU_W), lambda i, j: (i, 0, 0)),
                  pl.BlockSpec((1, CONV_W - 1, LRU_W), lambda i, j: (i, 0, 0))],
        out_specs=[pl.BlockSpec((tt, LRU_W), lambda i, j: (row(i, j), 0)),
                   pl.BlockSpec((1, 1, LRU_W), lambda i, j: (i, 0, 0)),
                   pl.BlockSpec((1, CONV_W - 1, LRU_W), lambda i, j: (i, 0, 0))],
        out_shape=[jax.ShapeDtypeStruct((b * t, LRU_W), out_dtype),
                   jax.ShapeDtypeStruct((b, 1, LRU_W), F32),
                   jax.ShapeDtypeStruct((b, CONV_W - 1, LRU_W), F32)],
        scratch_shapes=[pltpu.VMEM((1, LRU_W), F32),
                        pltpu.VMEM((tt + 8, LRU_W), F32),
                        pltpu.VMEM((tt, LRU_W), F32)],
        compiler_params=_cparams(("parallel", "arbitrary")),
        name="rglru",
    )(p, p, conv_w, conv_b.reshape(1, -1), _block_diag(w_r).astype(BF16), b_r.reshape(1, -1),
      _block_diag(w_i).astype(BF16), b_i.reshape(1, -1), a_param.reshape(1, -1),
      h0.reshape(b, 1, LRU_W), c0)
    return y, h1.reshape(b, LRU_W), c1


MLA_SCALE = (MLA_NOPE + MLA_ROPE) ** -0.5
ROPE_HALF = MLA_ROPE // 2


def _rope_rotate(x, cos, sin):
    lane = _iota(x.shape, 1)
    rot = jnp.where(lane % MLA_ROPE < ROPE_HALF, -pltpu.roll(x, LANE - ROPE_HALF, 1), pltpu.roll(x, ROPE_HALF, 1))
    return x * cos + rot * sin


def _mla_prep_kernel(qd_ref, kvd_ref, krr_ref, cos_ref, sin_ref, qg_ref, wuq_ref, kvg_ref, wukt_ref,
                     qabs_ref, qrope_ref, ckv_ref, krope_ref, kcb_ref, krb_ref):
    qd = qd_ref[...]
    qn = qd * lax.rsqrt(jnp.mean(qd * qd, axis=-1, keepdims=True) + RMS_EPS) * qg_ref[...]
    q = _dot(qn, wuq_ref[...])
    cos, sin = cos_ref[...], sin_ref[...]
    nn = MLA_H * MLA_NOPE
    lane = _iota((q.shape[0], LANE), 1)
    halves = [_rope_rotate(q[:, nn + i * LANE:nn + (i + 1) * LANE], cos, sin) for i in range(2)]
    per_half = LANE // MLA_ROPE
    for h in range(MLA_H):
        qabs_ref[h] = _dot(q[:, h * MLA_NOPE:(h + 1) * MLA_NOPE], wukt_ref[h]).astype(qabs_ref.dtype)
        half, sh = halves[h // per_half], (h % per_half) * MLA_ROPE
        piece = pltpu.roll(half, LANE - sh, 1) if sh else half
        qrope_ref[h] = jnp.where(lane < MLA_ROPE, piece, 0.0).astype(qrope_ref.dtype)
    kvd = kvd_ref[...]
    ckv = kvd * lax.rsqrt(jnp.mean(kvd * kvd, axis=-1, keepdims=True) + RMS_EPS) * kvg_ref[...]
    ckv_ref[...] = ckv
    kcb_ref[...] = ckv.astype(BF16)
    kr = jnp.where(lane < MLA_ROPE, _rope_rotate(krr_ref[...], cos, sin), 0.0)
    krope_ref[...] = kr[:, :MLA_ROPE]
    krb_ref[...] = kr.astype(BF16)


def _mla_attn_prompt_kernel(qabs_ref, qrope_ref, kc_ref, kr_ref, wuv_ref, y_ref, m_ref, l_ref, acc_ref, *, tq, tk):
    i, j = pl.program_id(1), pl.program_id(2)
    nk = pl.num_programs(2)

    @pl.when(j == 0)
    def _():
        m_ref[...] = jnp.full(m_ref.shape, NEG, F32)
        l_ref[...] = jnp.zeros(l_ref.shape, F32)
        acc_ref[...] = jnp.zeros(acc_ref.shape, F32)

    @pl.when(j <= i)
    def _():
        kc, kr = kc_ref[...], kr_ref[...]
        visible = (j * tk + _iota((tq, tk), 1)) <= (i * tq + _iota((tq, tk), 0))
        for h in range(MLA_H):
            s = (_dot_nt(qabs_ref[h], kc) + _dot_nt(qrope_ref[h], kr)) * MLA_SCALE
            s = jnp.where(visible, s, NEG)
            m_old = m_ref[h]
            m_new = jnp.maximum(m_old, jnp.max(s, axis=-1, keepdims=True))
            pr = jnp.exp(s - m_new)
            alpha = jnp.exp(m_old - m_new)
            l_ref[h] = alpha * l_ref[h] + jnp.sum(pr, axis=-1, keepdims=True)
            acc_ref[h] = alpha * acc_ref[h] + _dot(pr, kc)
            m_ref[h] = m_new

    @pl.when(j == nk - 1)
    def _():
        for h in range(MLA_H):
            o = acc_ref[h] / l_ref[h]
            y_ref[:, h * MLA_V:(h + 1) * MLA_V] = _dot(o, wuv_ref[h]).astype(y_ref.dtype)


def _mla_attn_sample_kernel(pt_ref, *refs, pp, tq):
    del pt_ref
    lat_refs, rope_refs = refs[:pp], refs[pp:2 * pp]
    qabs_ref, qrope_ref, ckv_ref, krope_ref, wuv_ref, y_ref, m_ref, l_ref, acc_ref = refs[2 * pp:]
    g = pl.program_id(1)
    ng = pl.num_programs(1)
    rows = MLA_H * tq
    q = qabs_ref[...].reshape(rows, MLA_KVL)
    qr = qrope_ref[...].reshape(rows, LANE)[:, :MLA_ROPE]

    @pl.when(g == 0)
    def _():
        m_ref[...] = jnp.full(m_ref.shape, NEG, F32)
        l_ref[...] = jnp.zeros(l_ref.shape, F32)
        acc_ref[...] = jnp.zeros(acc_ref.shape, F32)

    def update(s, vals):
        m_old = m_ref[...]
        m_new = jnp.maximum(m_old, jnp.max(s, axis=-1, keepdims=True))
        pr = jnp.exp(s - m_new)
        alpha = jnp.exp(m_old - m_new)
        l_ref[...] = alpha * l_ref[...] + jnp.sum(pr, axis=-1, keepdims=True)
        acc_ref[...] = alpha * acc_ref[...] + _dot(pr, vals)
        m_ref[...] = m_new

    kc = jnp.concatenate([r[...] for r in lat_refs], axis=0).astype(BF16)
    kr = jnp.concatenate([r[...] for r in rope_refs], axis=0).astype(BF16)
    update((_dot_nt(q, kc) + _dot_nt(qr, kr)) * MLA_SCALE, kc)

    @pl.when(g == ng - 1)
    def _():
        kc_new, kr_new = ckv_ref[...], krope_ref[...]
        s = (_dot_nt(q, kc_new) + _dot_nt(qr, kr_new)) * MLA_SCALE
        visible = _iota((rows, tq), 1) <= _iota((rows, tq), 0) % tq
        update(jnp.where(visible, s, NEG), kc_new)
        o = acc_ref[...] / l_ref[...]
        for h in range(MLA_H):
            y_ref[:, h * MLA_V:(h + 1) * MLA_V] = _dot(o[h * tq:(h + 1) * tq], wuv_ref[h]).astype(y_ref.dtype)


def _rope_tables(pos):
    inv = ROPE_THETA ** (-jnp.arange(ROPE_HALF, dtype=F32) * 2.0 / MLA_ROPE)
    ang = pos.astype(F32)[:, None] * inv[None, :]
    reps = LANE // ROPE_HALF
    return jnp.tile(jnp.cos(ang), (1, reps)), jnp.tile(jnp.sin(ang), (1, reps))


def _mla_call(p, b, t, pos0, q_norm_g, w_uq, kv_norm_g, w_ukv, cache_c, cache_r, page_table, layer, out_dtype):
    m = b * t
    tm = min(m, 256)
    wq = w_uq.reshape(MLA_QL, MLA_H, MLA_NOPE + MLA_ROPE)
    wq = jnp.concatenate([wq[..., :MLA_NOPE].reshape(MLA_QL, -1), wq[..., MLA_NOPE:].reshape(MLA_QL, -1)], axis=1)
    wkv = w_ukv.reshape(MLA_KVL, MLA_H, MLA_NOPE + MLA_V)
    wukt = jnp.transpose(wkv[..., :MLA_NOPE], (1, 2, 0)).astype(BF16)
    wuv = jnp.transpose(wkv[..., MLA_NOPE:], (1, 0, 2)).astype(BF16)
    cos, sin = _rope_tables(pos0 + jnp.arange(t))
    if t < tm:
        cos, sin = jnp.tile(cos, (tm // t, 1)), jnp.tile(sin, (tm // t, 1))
    ntab = cos.shape[0] // tm
    qdt = BF16 if t >= tm else F32
    const2 = lambda i: (0, 0)
    qabs, qrope, ckv, krope, kcb, krb = pl.pallas_call(
        _mla_prep_kernel,
        grid=(m // tm,),
        in_specs=[pl.BlockSpec((tm, MLA_QL), lambda i: (i, P_QD // MLA_QL)),
                  pl.BlockSpec((tm, MLA_KVL), lambda i: (i, P_KVD // MLA_KVL)),
                  pl.BlockSpec((tm, LANE), lambda i: (i, P_KR // LANE)),
                  pl.BlockSpec((tm, LANE), lambda i: (i % ntab, 0)),
                  pl.BlockSpec((tm, LANE), lambda i: (i % ntab, 0)),
                  pl.BlockSpec((1, MLA_QL), const2),
                  pl.BlockSpec((MLA_QL, MLA_H * (MLA_NOPE + MLA_ROPE)), const2),
                  pl.BlockSpec((1, MLA_KVL), const2),
                  pl.BlockSpec((MLA_H, MLA_NOPE, MLA_KVL), lambda i: (0, 0, 0))],
        out_specs=[pl.BlockSpec((MLA_H, tm, MLA_KVL), lambda i: (0, i, 0)),
                   pl.BlockSpec((MLA_H, tm, LANE), lambda i: (0, i, 0)),
                   pl.BlockSpec((tm, MLA_KVL), lambda i: (i, 0)),
                   pl.BlockSpec((tm, MLA_ROPE), lambda i: (i, 0)),
                   pl.BlockSpec((tm, MLA_KVL), lambda i: (i, 0)),
                   pl.BlockSpec((tm, LANE), lambda i: (i, 0))],
        out_shape=[jax.ShapeDtypeStruct((MLA_H, m, MLA_KVL), qdt),
                   jax.ShapeDtypeStruct((MLA_H, m, LANE), qdt),
                   jax.ShapeDtypeStruct((m, MLA_KVL), F32),
                   jax.ShapeDtypeStruct((m, MLA_ROPE), F32),
                   jax.ShapeDtypeStruct((m, MLA_KVL), BF16),
                   jax.ShapeDtypeStruct((m, LANE), BF16)],
        compiler_params=_cparams(("parallel",)),
        name="mla_prep",
    )(p, p, p, cos, sin, q_norm_g.reshape(1, -1), wq.astype(BF16), kv_norm_g.reshape(1, -1), wukt)

    if cache_c is None:
        tq = tk = min(t, 256)
        nq = t // tq
        y = pl.pallas_call(
            functools.partial(_mla_attn_prompt_kernel, tq=tq, tk=tk),
            grid=(b, nq, nq),
            in_specs=[pl.BlockSpec((MLA_H, tq, MLA_KVL), lambda bi, i, j: (0, bi * nq + i, 0)),
                      pl.BlockSpec((MLA_H, tq, LANE), lambda bi, i, j: (0, bi * nq + i, 0)),
                      pl.BlockSpec((tk, MLA_KVL), lambda bi, i, j: (bi * nq + jnp.minimum(i, j), 0)),
                      pl.BlockSpec((tk, LANE), lambda bi, i, j: (bi * nq + jnp.minimum(i, j), 0)),
                      pl.BlockSpec((MLA_H, MLA_KVL, MLA_V), lambda bi, i, j: (0, 0, 0))],
            out_specs=pl.BlockSpec((tq, MLA_H * MLA_V), lambda bi, i, j: (bi * nq + i, 0)),
            out_shape=jax.ShapeDtypeStruct((m, MLA_H * MLA_V), out_dtype),
            scratch_shapes=[pltpu.VMEM((MLA_H, tq, 1), F32), pltpu.VMEM((MLA_H, tq, 1), F32),
                            pltpu.VMEM((MLA_H, tq, MLA_KVL), F32)],
            compiler_params=_cparams(("parallel", "parallel", "arbitrary")),
            name="mla_attn_prompt",
        )(qabs, qrope, kcb, krb, wuv)
        return y, ckv, krope

    n_pages = page_table.shape[1]
    page = cache_c.shape[2]
    pp = max(d for d in range(1, 9) if n_pages % d == 0)
    lat_specs = [pl.BlockSpec((None, None, page, MLA_KVL),
                              functools.partial(lambda bi, g, pt, k: (layer, pt[bi, g * pp + k], 0, 0), k=k))
                 for k in range(pp)]
    rope_specs = [pl.BlockSpec((None, None, page, MLA_ROPE),
                               functools.partial(lambda bi, g, pt, k: (layer, pt[bi, g * pp + k], 0, 0), k=k))
                  for k in range(pp)]
    rows = MLA_H * t
    y = pl.pallas_call(
        functools.partial(_mla_attn_sample_kernel, pp=pp, tq=t),
        grid_spec=pltpu.PrefetchScalarGridSpec(
            num_scalar_prefetch=1,
            grid=(b, n_pages // pp),
            in_specs=lat_specs + rope_specs + [
                pl.BlockSpec((MLA_H, t, MLA_KVL), lambda bi, g, pt: (0, bi, 0)),
                pl.BlockSpec((MLA_H, t, LANE), lambda bi, g, pt: (0, bi, 0)),
                pl.BlockSpec((t, MLA_KVL), lambda bi, g, pt: (bi, 0)),
                pl.BlockSpec((t, MLA_ROPE), lambda bi, g, pt: (bi, 0)),
                pl.BlockSpec((MLA_H, MLA_KVL, MLA_V), lambda bi, g, pt: (0, 0, 0))],
            out_specs=pl.BlockSpec((t, MLA_H * MLA_V), lambda bi, g, pt: (bi, 0)),
            scratch_shapes=[pltpu.VMEM((rows, 1), F32), pltpu.VMEM((rows, 1), F32),
                            pltpu.VMEM((rows, MLA_KVL), F32)]),
        out_shape=jax.ShapeDtypeStruct((m, MLA_H * MLA_V), out_dtype),
        compiler_params=_cparams(("parallel", "arbitrary")),
        name="mla_attn_sample",
    )(page_table, *([cache_c] * pp), *([cache_r] * pp), qabs, qrope, ckv, krope, wuv)
    return y, ckv, krope


ROUTER_GRP_LANE = MOE_E


def _merge_kernel(gl_ref, ya_ref, yb_ref, yc_ref, yd_ref, wb_ref, wo_ref, x_ref, mod_ref, ng_ref, wr_ref, br_ref,
                  xn_ref, h2_ref, lg_ref):
    bb, tt, d = x_ref.shape
    merged = None
    for n, y_ref in enumerate((ya_ref, yb_ref, yc_ref, yd_ref)):
        term = _sigmoid(gl_ref[:, n * d:(n + 1) * d]) * _dot(y_ref[...], wb_ref[n])
        merged = term if merged is None else merged + term
    out = _dot(merged, wo_ref[...])
    mod = mod_ref[...]
    x = x_ref[...] + mod[:, 2:3, :] * out.reshape(bb, tt, d)
    xn_ref[...] = x
    h = x * lax.rsqrt(jnp.mean(x * x, axis=-1, keepdims=True) + RMS_EPS) * ng_ref[...]
    h = (h * (1.0 + mod[:, 4:5, :]) + mod[:, 3:4, :]).reshape(bb * tt, d)
    h2_ref[...] = h.astype(h2_ref.dtype)
    lg_ref[...] = _dot_hi(h, wr_ref[...]) + br_ref[...]


def _merge_call(p, ys, w_branch, w_out, x, mod, norm2_g, w_router, b_router):
    b, t, d = x.shape
    bb, tt = _row_blocks(b, t, 256)
    nt = t // tt
    tm = bb * tt
    row = lambda i, j: (i * nt + j, 0)
    yspec = pl.BlockSpec((tm, ys[0].shape[1]), row)
    return pl.pallas_call(
        _merge_kernel,
        grid=(b // bb, nt),
        in_specs=[pl.BlockSpec((tm, N_BRANCH * d), row), yspec, yspec, yspec, yspec,
                  pl.BlockSpec(w_branch.shape, lambda i, j: (0, 0, 0)),
                  pl.BlockSpec(w_out.shape, lambda i, j: (0, 0)),
                  pl.BlockSpec((bb, tt, d), lambda i, j: (i, j, 0)),
                  pl.BlockSpec((bb, mod.shape[1], d), lambda i, j: (i, 0, 0)),
                  pl.BlockSpec((1, 1, d), lambda i, j: (0, 0, 0)),
                  pl.BlockSpec((d, LANE), lambda i, j: (0, 0)),
                  pl.BlockSpec((1, LANE), lambda i, j: (0, 0))],
        out_specs=[pl.BlockSpec((bb, tt, d), lambda i, j: (i, j, 0)),
                   pl.BlockSpec((tm, d), row),
                   pl.BlockSpec((tm, LANE), row)],
        out_shape=[jax.ShapeDtypeStruct((b, t, d), F32),
                   jax.ShapeDtypeStruct((b * t, d), BF16),
                   jax.ShapeDtypeStruct((b * t, LANE), F32)],
        compiler_params=_cparams(("parallel", "parallel")),
        name="merge",
    )(p, *ys, w_branch, w_out, x, mod, norm2_g.reshape(1, 1, d), w_router, b_router)


def _route(logits):
    lane = _iota(logits.shape, 1)
    big = jnp.int32(1 << 20)
    grp = jnp.where(jnp.right_shift(lane, 2) == ROUTER_GRP_LANE // MOE_G, logits, NEG)
    gmax = jnp.max(grp, axis=-1, keepdims=True)
    g_top = 1.0 / jnp.sum(jnp.exp(grp - gmax), axis=-1, keepdims=True)
    gidx = jnp.min(jnp.where(grp == gmax, lane, big), axis=-1, keepdims=True) - ROUTER_GRP_LANE
    el = jnp.where(jnp.right_shift(lane, 3) == gidx, logits, NEG)
    m1 = jnp.max(el, axis=-1, keepdims=True)
    i1 = jnp.min(jnp.where(el == m1, lane, big), axis=-1, keepdims=True)
    el2 = jnp.where(lane == i1, NEG, el)
    m2 = jnp.max(el2, axis=-1, keepdims=True)
    i2 = jnp.min(jnp.where(el2 == m2, lane, big), axis=-1, keepdims=True)
    e2 = jnp.exp(m2 - m1)
    w1 = g_top / (1.0 + e2)
    return jnp.where(lane == i1, w1, 0.0) + jnp.where(lane == i2, w1 * e2, 0.0)


def _moe_dense_kernel(h_ref, lg_ref, wg_ref, wu_ref, wd_ref, x_ref, mod_ref, o_ref, comb_ref, acc_ref):
    e = pl.program_id(1)
    ne = pl.num_programs(1)
    bb, tt, d = x_ref.shape

    @pl.when(e == 0)
    def _():
        comb_ref[...] = _route(lg_ref[...])
        acc_ref[...] = jnp.zeros(acc_ref.shape, F32)

    comb = comb_ref[...]
    ce = jnp.sum(jnp.where(_iota(comb.shape, 1) == e, comb, 0.0), axis=-1, keepdims=True)
    h = h_ref[...]
    hid = _silu(_dot(h, wg_ref[...])) * _dot(h, wu_ref[...]) * ce
    acc_ref[...] += _dot(hid, wd_ref[...])

    @pl.when(e == ne - 1)
    def _():
        o_ref[...] = x_ref[...] + mod_ref[...][:, 5:6, :] * acc_ref[...].reshape(bb, tt, d)


def _moe_call(h2, logits, w_gate, w_up, w_down, x, mod):
    b, t, d = x.shape
    bb, tt = _row_blocks(b, t, 512)
    nt = t // tt
    tm = bb * tt
    ne, _, hid = w_gate.shape
    row = lambda i, e: (i, 0)
    xmap = lambda i, e: (i // nt, i % nt, 0)
    return pl.pallas_call(
        _moe_dense_kernel,
        grid=(b * t // tm, ne),
        in_specs=[pl.BlockSpec((tm, d), row), pl.BlockSpec((tm, LANE), row),
                  pl.BlockSpec((None, d, hid), lambda i, e: (e, 0, 0)),
                  pl.BlockSpec((None, d, hid), lambda i, e: (e, 0, 0)),
                  pl.BlockSpec((None, hid, d), lambda i, e: (e, 0, 0)),
                  pl.BlockSpec((bb, tt, d), xmap),
                  pl.BlockSpec((bb, mod.shape[1], d), lambda i, e: (i // nt, 0, 0))],
        out_specs=pl.BlockSpec((bb, tt, d), xmap),
        out_shape=jax.ShapeDtypeStruct((b, t, d), F32),
        scratch_shapes=[pltpu.VMEM((tm, LANE), F32), pltpu.VMEM((tm, d), F32)],
        compiler_params=_cparams(("parallel", "arbitrary")),
        name="moe",
    )(h2, logits, w_gate, w_up, w_down, x, mod)


def _layer(x, mod, pos0, lw, state, cache, out_dtype):
    b, t, d = x.shape
    ssm0, ssm_conv0, hg0, lru0, lru_conv0 = state
    h = _prenorm_call(x, lw['norm1_g'], mod, sh_row=0, sc_row=1)
    p = _matmul_call(h, lw['w_in'])
    y_a, ssm1, ssm_conv1 = _ssd_call(p, b, t, lw['ssd_conv_w'], lw['ssd_conv_b'], lw['ssd_dt_bias'], lw['ssd_a_log'],
                                     lw['ssd_d'], lw['ssd_norm_g'], ssm0, ssm_conv0, out_dtype)
    y_b, hg1 = _hgrn_call(p, b, t, lw['hg_lb'], lw['hg_norm_g'], hg0, out_dtype)
    y_c, ckv, krope = _mla_call(p, b, t, pos0, lw['mla_q_norm_g'], lw['mla_w_uq'], lw['mla_kv_norm_g'],
                                lw['mla_w_ukv'], *cache, out_dtype)
    y_d, lru1, lru_conv1 = _lru_call(p, b, t, pos0, lw['lru_conv_w'], lw['lru_conv_b'], lw['lru_w_r'], lw['lru_b_r'],
                                     lw['lru_w_i'], lw['lru_b_i'], lw['lru_a'], lru0, lru_conv0, out_dtype)
    x, h2, logits = _merge_call(p, (y_a, y_b, y_c, y_d), lw['w_branch'], lw['w_out'], x, mod, lw['norm2_g'],
                                lw['w_router'], lw['b_router'])
    x = _moe_call(h2, logits, lw['moe_w_gate'], lw['moe_w_up'], lw['moe_w_down'], x, mod)
    new = (ckv.reshape(b, t, -1), krope.reshape(b, t, -1), ssm1, ssm_conv1, hg1, lru1, lru_conv1)
    return x, new


def kernel(x_prompt, x_sample, c_prompt, c_sample, cache_kv_latent, cache_k_rope, state_ssm, state_ssm_conv,
           state_hgrn, state_lru, state_lru_conv, page_table, norm1_g, norm2_g, w_mod, b_mod, w_in, ssd_conv_w,
           ssd_conv_b, ssd_dt_bias, ssd_a_log, ssd_d, ssd_norm_g, hg_lb_raw, hg_norm_g, mla_q_norm_g, mla_w_uq,
           mla_kv_norm_g, mla_w_ukv, lru_conv_w, lru_conv_b, lru_w_r, lru_b_r, lru_w_i, lru_b_i, lru_a, w_branch,
           w_out, moe_w_grp, moe_b_grp, moe_w_rt, moe_b_rt, moe_w_gate, moe_w_up, moe_w_down, final_norm_g):
    bp, tp, d = x_prompt.shape
    bs, ts, _ = x_sample.shape
    depth = w_in.shape[0]
    n_past = page_table.shape[1] * cache_kv_latent.shape[2]
    lb_all = jnp.cumsum(jax.nn.softmax(hg_lb_raw.astype(F32), axis=0), axis=0)
    lb_all = lb_all - lb_all[:1]
    c_all = jnp.concatenate([c_prompt, c_sample], axis=0)
    zeros = lambda *s: jnp.zeros(s, F32)
    yp, ys = x_prompt, x_sample
    p_new, s_new = [], []
    for l in range(depth):
        mod = _mod_call(c_all, w_mod[l].astype(BF16), b_mod[l]).reshape(bp + bs, 6, d)
        pad_r = LANE - MOE_E - MOE_G
        lw = {
            'norm1_g': norm1_g[l], 'norm2_g': norm2_g[l], 'w_in': _pack_w_in(w_in[l]),
            'ssd_conv_w': ssd_conv_w[l], 'ssd_conv_b': ssd_conv_b[l], 'ssd_dt_bias': ssd_dt_bias[l],
            'ssd_a_log': ssd_a_log[l], 'ssd_d': ssd_d[l], 'ssd_norm_g': ssd_norm_g[l],
            'hg_lb': lb_all[l], 'hg_norm_g': hg_norm_g[l],
            'mla_q_norm_g': mla_q_norm_g[l], 'mla_w_uq': mla_w_uq[l],
            'mla_kv_norm_g': mla_kv_norm_g[l], 'mla_w_ukv': mla_w_ukv[l],
            'lru_conv_w': lru_conv_w[l], 'lru_conv_b': lru_conv_b[l], 'lru_w_r': lru_w_r[l], 'lru_b_r': lru_b_r[l],
            'lru_w_i': lru_w_i[l], 'lru_b_i': lru_b_i[l], 'lru_a': lru_a[l],
            'w_branch': w_branch[l].astype(BF16), 'w_out': w_out[l].astype(BF16),
            'w_router': jnp.pad(jnp.concatenate([moe_w_rt[l], moe_w_grp[l]], axis=1), ((0, 0), (0, pad_r))),
            'b_router': jnp.pad(jnp.concatenate([moe_b_rt[l], moe_b_grp[l]]), (0, pad_r)).reshape(1, LANE),
            'moe_w_gate': moe_w_gate[l].astype(BF16), 'moe_w_up': moe_w_up[l].astype(BF16),
            'moe_w_down': moe_w_down[l].astype(BF16),
        }
        p_state = (zeros(bp, SSD_H, SSD_P, SSD_N), zeros(bp, CONV_W - 1, SSD_CONV), zeros(bp, HG_H, HG_K, HG_V),
                   zeros(bp, LRU_W), zeros(bp, CONV_W - 1, LRU_W))
        s_state = (state_ssm[l], state_ssm_conv[l], state_hgrn[l], state_lru[l], state_lru_conv[l])
        yp, pn = _layer(yp, mod[:bp], 0, lw, p_state, (None, None, None, l), BF16)
        ys, sn = _layer(ys, mod[bp:], n_past, lw, s_state, (cache_kv_latent, cache_k_rope, page_table, l), F32)
        p_new.append(pn)
        s_new.append(sn)
    no_mod = zeros(1, 2, d)
    yp = _prenorm_call(yp, final_norm_g, jnp.broadcast_to(no_mod, (bp, 2, d)), 0, 1, F32).reshape(bp, tp, d)
    ys = _prenorm_call(ys, final_norm_g, jnp.broadcast_to(no_mod, (bs, 2, d)), 0, 1, F32).reshape(bs, ts, d)
    stk = lambda news, j: jnp.stack([n[j] for n in news])
    return (yp, ys) + tuple(stk(p_new, j) for j in range(7)) + tuple(stk(s_new, j) for j in range(7))
```

```python
import functools
import math

import jax
import jax.numpy as jnp
from jax import lax
from jax.experimental import pallas as pl
from jax.experimental.pallas import tpu as pltpu

F32 = jnp.float32
BF16 = jnp.bfloat16
HI = lax.Precision.HIGHEST
NEG = -1e30

RMS_EPS = 1e-6
D_MODEL = 1024
CONV_W = 4
SSD_H, SSD_P, SSD_G, SSD_N = 8, 64, 2, 64
SSD_INNER = SSD_H * SSD_P
SSD_CONV = SSD_INNER + 2 * SSD_G * SSD_N
SSD_CHUNK = 64
HG_H, HG_K, HG_V = 4, 128, 128
HG_INNER = HG_H * HG_V
HG_CHUNK = 16
MLA_H, MLA_QL, MLA_KVL, MLA_NOPE, MLA_ROPE, MLA_V = 8, 256, 256, 64, 32, 64
ROPE_THETA = 10000.0
LRU_W, LRU_NB = 512, 8
LRU_C = 8.0
N_BRANCH = 4
MOE_G, MOE_EPG, MOE_E, MOE_HID = 4, 8, 32, 256

V7X_VMEM_LIMIT = 56 * 1024 * 1024
LANE = 128

P_GL, P_Z, P_HQ, P_HF, P_HI, P_HG, P_LX, P_LY = 0, 4096, 4608, 5120, 5632, 6144, 6656, 7168
P_XBC, P_QD, P_KVD, P_DT, P_KR = 7680, 8448, 8704, 8960, 9088
P_TOTAL = 9216


def _cparams(sem):
    return pltpu.CompilerParams(dimension_semantics=sem, vmem_limit_bytes=V7X_VMEM_LIMIT)


def _silu(x):
    return x * (1.0 / (1.0 + jnp.exp(-x)))


def _sigmoid(x):
    return 1.0 / (1.0 + jnp.exp(-x))


def _softplus(x):
    return jnp.maximum(x, 0.0) + jnp.log(1.0 + jnp.exp(-jnp.abs(x)))


def _iota(shape, dim):
    return lax.broadcasted_iota(jnp.int32, shape, dim)


def _dot(a, b):
    return jnp.dot(a.astype(BF16), b.astype(BF16), preferred_element_type=F32)


def _dot_nt(a, b):
    return lax.dot_general(a.astype(BF16), b.astype(BF16), (((1,), (1,)), ((), ())), preferred_element_type=F32)


def _dot_tn(a, b):
    return lax.dot_general(a.astype(BF16), b.astype(BF16), (((0,), (0,)), ((), ())), preferred_element_type=F32)


def _dot_hi(a, b):
    return jnp.dot(a, b, precision=HI, preferred_element_type=F32)


def _dot_nt_hi(a, b):
    return lax.dot_general(a, b, (((1,), (1,)), ((), ())), precision=HI, preferred_element_type=F32)


def _tril(n):
    return (_iota((n, n), 0) >= _iota((n, n), 1)).astype(F32)


def _eye(n):
    return (_iota((n, n), 0) == _iota((n, n), 1)).astype(F32)


def _row_blocks(b, t, target):
    if t >= target:
        return 1, target
    return min(b, target // t), t


def _mod_kernel(c_ref, w_ref, b_ref, o_ref):
    o_ref[...] = _dot(_silu(c_ref[...]), w_ref[...]) + b_ref[...]


def _mod_call(c, w, b):
    m, d = c.shape
    n = w.shape[1]
    tn = 1536
    return pl.pallas_call(
        _mod_kernel,
        grid=(n // tn,),
        in_specs=[pl.BlockSpec((m, d), lambda j: (0, 0)),
                  pl.BlockSpec((d, tn), lambda j: (0, j)),
                  pl.BlockSpec((1, tn), lambda j: (0, j))],
        out_specs=pl.BlockSpec((m, tn), lambda j: (0, j)),
        out_shape=jax.ShapeDtypeStruct((m, n), F32),
        compiler_params=_cparams(("parallel",)),
        name="adaln_mod",
    )(c, w, b.reshape(1, n))


def _prenorm_kernel(x_ref, g_ref, mod_ref, o_ref, *, sh_row, sc_row):
    x = x_ref[...]
    bb, tt, d = x.shape
    y = x * lax.rsqrt(jnp.mean(x * x, axis=-1, keepdims=True) + RMS_EPS) * g_ref[...]
    mod = mod_ref[...]
    y = y * (1.0 + mod[:, sc_row:sc_row + 1, :]) + mod[:, sh_row:sh_row + 1, :]
    o_ref[...] = y.reshape(bb * tt, d).astype(o_ref.dtype)


def _prenorm_call(x, g, mod, sh_row, sc_row, out_dtype=BF16):
    b, t, d = x.shape
    bb, tt = _row_blocks(b, t, 512)
    nt = t // tt
    return pl.pallas_call(
        functools.partial(_prenorm_kernel, sh_row=sh_row, sc_row=sc_row),
        grid=(b // bb, nt),
        in_specs=[pl.BlockSpec((bb, tt, d), lambda i, j: (i, j, 0)),
                  pl.BlockSpec((1, 1, d), lambda i, j: (0, 0, 0)),
                  pl.BlockSpec((bb, mod.shape[1], d), lambda i, j: (i, 0, 0))],
        out_specs=pl.BlockSpec((bb * tt, d), lambda i, j: (i * nt + j, 0)),
        out_shape=jax.ShapeDtypeStruct((b * t, d), out_dtype),
        compiler_params=_cparams(("parallel", "parallel")),
        name="prenorm",
    )(x, g.reshape(1, 1, d), mod)


def _matmul_kernel(x_ref, w_ref, o_ref):
    o_ref[...] = jnp.dot(x_ref[...], w_ref[...], preferred_element_type=F32)


def _matmul_call(x, w, tm=512, tn=1024):
    m, k = x.shape
    n = w.shape[1]
    tm = min(tm, m)
    return pl.pallas_call(
        _matmul_kernel,
        grid=(n // tn, m // tm),
        in_specs=[pl.BlockSpec((tm, k), lambda j, i: (i, 0)),
                  pl.BlockSpec((k, tn), lambda j, i: (0, j))],
        out_specs=pl.BlockSpec((tm, tn), lambda j, i: (i, j)),
        out_shape=jax.ShapeDtypeStruct((m, n), F32),
        compiler_params=_cparams(("parallel", "parallel")),
        name="in_proj",
    )(x, w)


def _pack_w_in(w_in):
    d = w_in.shape[0]
    sizes = (SSD_INNER, SSD_CONV, SSD_H, HG_H * HG_K, HG_H * HG_K, HG_INNER, HG_INNER,
             MLA_QL, MLA_KVL, MLA_ROPE, LRU_W, LRU_W, N_BRANCH * D_MODEL)
    offs = [0]
    for s in sizes:
        offs.append(offs[-1] + s)
    z, xbc, dt, hq, hf, hi, hg, qd, kvd, krr, lx, ly, gl = [w_in[:, offs[i]:offs[i + 1]] for i in range(13)]
    pad = lambda a: jnp.pad(a, ((0, 0), (0, LANE - a.shape[1])))
    return jnp.concatenate([gl, z, hq, hf, hi, hg, lx, ly, xbc, qd, kvd, pad(dt), pad(krr)], axis=1).astype(BF16)


def _causal_conv(xx_ref, x, w_ref, b_ref, tt):
    xx_ref[8:8 + tt, :] = x
    w = w_ref[...]
    y = b_ref[...] + w[3:4, :] * x
    for k in range(1, CONV_W):
        y = y + w[3 - k:4 - k, :] * xx_ref[8 - k:8 - k + tt, :]
    return y


def _ssd_kernel(z_ref, xbc_ref, dt_ref, cw_ref, cb_ref, dtb_ref, alog_ref, dfull_ref, ng_ref, s0_ref, c0_ref,
                y_ref, s1_ref, c1_ref, st_ref, xx_ref, yc_ref, *, tt, cl):
    t = pl.program_id(1)
    nt = pl.num_programs(1)
    hp = SSD_H // SSD_G * SSD_P
    eye = _eye(SSD_N)

    @pl.when(t == 0)
    def _():
        xx_ref[0:8, :] = jnp.zeros((8, SSD_CONV), F32)
        xx_ref[5:8, :] = c0_ref[0]
        for h in range(SSD_H):
            g, r = divmod(h, SSD_H // SSD_G)
            st_ref[g, :, r * SSD_P:(r + 1) * SSD_P] = _dot_nt_hi(eye, s0_ref[0, h])

    conv = _causal_conv(xx_ref, xbc_ref[...], cw_ref, cb_ref, tt)
    new_hist = xx_ref[tt:tt + 8, :]
    xx_ref[0:8, :] = new_hist
    act = _silu(conv)
    xs = act[:, :SSD_INNER]
    bm = act[:, SSD_INNER:SSD_INNER + SSD_G * SSD_N]
    cm = act[:, SSD_INNER + SSD_G * SSD_N:]
    dt = _softplus(dt_ref[...] + dtb_ref[...])
    da = dt * (-jnp.exp(alog_ref[...]))
    expand = (_iota((LANE, SSD_INNER), 1) // SSD_P == _iota((LANE, SSD_INNER), 0)).astype(F32)
    dtf = _dot_hi(dt, expand)
    ti, si = _iota((tt, tt), 0), _iota((tt, tt), 1)
    same_chunk = ti // cl == si // cl
    acs_all = _dot_hi((same_chunk & (ti >= si)).astype(F32), da)
    acs_t = lax.dot_general(da, (same_chunk & (ti <= si)).astype(F32), (((0,), (0,)), ((), ())),
                            precision=HI, preferred_element_type=F32)
    last_all = _dot_hi((si == ti // cl * cl + (cl - 1)).astype(F32), acs_all)
    eacs_all = jnp.exp(_dot_hi(acs_all, expand))
    dend_all = jnp.exp(_dot_hi(last_all - acs_all, expand))
    xdt_all = xs * dtf
    xsc_all = xdt_all * dend_all
    tri = _iota((cl, cl), 0) >= _iota((cl, cl), 1)

    for c in range(tt // cl):
        rows = slice(c * cl, (c + 1) * cl)
        eacs = eacs_all[rows]
        xdt = xdt_all[rows]
        xsc = xsc_all[rows]
        for g in range(SSD_G):
            bg = bm[rows, g * SSD_N:(g + 1) * SSD_N]
            cg = cm[rows, g * SSD_N:(g + 1) * SSD_N]
            cb = _dot_nt(cg, bg)
            sg = st_ref[g]
            y_inter = _dot(cg, sg) * eacs[:, g * hp:(g + 1) * hp]
            for r in range(SSD_H // SSD_G):
                h = g * (SSD_H // SSD_G) + r
                seg = acs_all[rows, h:h + 1] - acs_t[h:h + 1, rows]
                decay = jnp.exp(jnp.where(tri, seg, NEG))
                y_h = _dot(cb * decay, xdt[:, h * SSD_P:(h + 1) * SSD_P])
                yc_ref[rows, h * SSD_P:(h + 1) * SSD_P] = y_h + y_inter[:, r * SSD_P:(r + 1) * SSD_P]
            cdec = eacs[cl - 1:cl, g * hp:(g + 1) * hp]
            st_ref[g] = cdec * sg + _dot_tn(bg, xsc[:, g * hp:(g + 1) * hp])

    y = yc_ref[...] + dfull_ref[...] * xs
    yz = y * _silu(z_ref[...])
    out = yz * lax.rsqrt(jnp.mean(yz * yz, axis=-1, keepdims=True) + RMS_EPS) * ng_ref[...]
    y_ref[...] = out.astype(y_ref.dtype)

    @pl.when(t == nt - 1)
    def _():
        c1_ref[0] = new_hist[5:8, :]
        for h in range(SSD_H):
            g, r = divmod(h, SSD_H // SSD_G)
            s1_ref[0, h] = _dot_nt_hi(eye, st_ref[g, :, r * SSD_P:(r + 1) * SSD_P])


def _ssd_call(p, b, t, conv_w, conv_b, dt_bias, a_log, d_skip, norm_g, s0, c0, out_dtype):
    tt = min(t, 256)
    cl = SSD_CHUNK if tt % SSD_CHUNK == 0 else tt
    nt = t // tt
    padl = lambda v: jnp.pad(v, (0, LANE - v.shape[0])).reshape(1, LANE)
    row = lambda i, j: i * nt + j
    const2 = lambda i, j: (0, 0)
    return pl.pallas_call(
        functools.partial(_ssd_kernel, tt=tt, cl=cl),
        grid=(b, nt),
        in_specs=[pl.BlockSpec((tt, SSD_INNER), lambda i, j: (row(i, j), P_Z // SSD_INNER)),
                  pl.BlockSpec((tt, SSD_CONV), lambda i, j: (row(i, j), P_XBC // SSD_CONV)),
                  pl.BlockSpec((tt, LANE), lambda i, j: (row(i, j), P_DT // LANE)),
                  pl.BlockSpec((CONV_W, SSD_CONV), const2),
                  pl.BlockSpec((1, SSD_CONV), const2),
                  pl.BlockSpec((1, LANE), const2),
                  pl.BlockSpec((1, LANE), const2),
                  pl.BlockSpec((1, SSD_INNER), const2),
                  pl.BlockSpec((1, SSD_INNER), const2),
                  pl.BlockSpec((1, SSD_H, SSD_P, SSD_N), lambda i, j: (i, 0, 0, 0)),
                  pl.BlockSpec((1, CONV_W - 1, SSD_CONV), lambda i, j: (i, 0, 0))],
        out_specs=[pl.BlockSpec((tt, SSD_INNER), lambda i, j: (row(i, j), 0)),
                   pl.BlockSpec((1, SSD_H, SSD_P, SSD_N), lambda i, j: (i, 0, 0, 0)),
                   pl.BlockSpec((1, CONV_W - 1, SSD_CONV), lambda i, j: (i, 0, 0))],
        out_shape=[jax.ShapeDtypeStruct((b * t, SSD_INNER), out_dtype),
                   jax.ShapeDtypeStruct((b, SSD_H, SSD_P, SSD_N), F32),
                   jax.ShapeDtypeStruct((b, CONV_W - 1, SSD_CONV), F32)],
        scratch_shapes=[pltpu.VMEM((SSD_G, SSD_N, SSD_H // SSD_G * SSD_P), F32),
                        pltpu.VMEM((tt + 8, SSD_CONV), F32),
                        pltpu.VMEM((tt, SSD_INNER), F32)],
        compiler_params=_cparams(("parallel", "arbitrary")),
        name="ssd",
    )(p, p, p, conv_w, conv_b.reshape(1, -1), padl(dt_bias), padl(a_log),
      jnp.repeat(d_skip, SSD_P).reshape(1, -1), norm_g.reshape(1, -1), s0, c0)


def _hgrn_kernel(q_ref, f_ref, i_ref, g_ref, loglb_ref, log1m_ref, onem_ref, ng_ref, s0_ref,
                 y_ref, s1_ref, st_ref, *, tt, cl):
    t = pl.program_id(1)
    nt = pl.num_programs(1)
    eye = _eye(HG_K)
    hs = [slice(h * HG_K, (h + 1) * HG_K) for h in range(HG_H)]

    @pl.when(t == 0)
    def _():
        for h in range(HG_H):
            st_ref[h] = _dot_nt_hi(eye, s0_ref[0, h])

    tril = _tril(cl)
    rowi = _iota((cl, HG_INNER), 0)
    loglb, log1m, onem, ng = loglb_ref[...], log1m_ref[...], onem_ref[...], ng_ref[...]

    def chunk(c, carry):
        rows = pl.ds(pl.multiple_of(c * cl, cl), cl)
        q, hf, v = q_ref[rows, :], f_ref[rows, :], i_ref[rows, :]
        b_ = log1m - _softplus(-hf)
        logf = jnp.maximum(loglb, b_) + jnp.log(1.0 + jnp.exp(-jnp.abs(loglb - b_)))
        kin = onem * _sigmoid(-hf)
        bc = _dot_hi(tril, logf)
        bl = bc[cl - 1:cl, :]
        p = q * kin
        o = [jnp.sum(p[:, s], axis=-1, keepdims=True) * v[:, s] for s in hs]
        for d in range(1, cl):
            kd, bd, vd = pltpu.roll(kin, d, 0), pltpu.roll(bc, d, 0), pltpu.roll(v, d, 0)
            p = q * kd * jnp.exp(jnp.where(rowi >= d, bc - bd, NEG))
            o = [o[h] + jnp.sum(p[:, s], axis=-1, keepdims=True) * vd[:, s] for h, s in enumerate(hs)]
        qe = q * jnp.exp(bc)
        ke = kin * jnp.exp(bl - bc)
        outs = []
        for h, s in enumerate(hs):
            st = st_ref[h]
            oh = o[h] + _dot_nt(qe[:, s], st)
            st_ref[h] = jnp.exp(bl[:, s]) * st + _dot_tn(v[:, s], ke[:, s])
            outs.append(oh * lax.rsqrt(jnp.mean(oh * oh, axis=-1, keepdims=True) + RMS_EPS) * ng)
        y = jnp.concatenate(outs, axis=-1) * _silu(g_ref[rows, :])
        y_ref[rows, :] = y.astype(y_ref.dtype)
        return carry

    lax.fori_loop(0, tt // cl, chunk, 0)

    @pl.when(t == nt - 1)
    def _():
        for h in range(HG_H):
            s1_ref[0, h] = _dot_nt_hi(eye, st_ref[h])


def _hgrn_call(p, b, t, lb, norm_g, s0, out_dtype):
    tt = min(t, 256)
    cl = HG_CHUNK if tt % HG_CHUNK == 0 else tt
    nt = t // tt
    row = lambda i, j: i * nt + j
    const2 = lambda i, j: (0, 0)
    col = lambda off: (lambda i, j: (row(i, j), off // HG_INNER))
    vec = pl.BlockSpec((1, HG_INNER), const2)
    return pl.pallas_call(
        functools.partial(_hgrn_kernel, tt=tt, cl=cl),
        grid=(b, nt),
        in_specs=[pl.BlockSpec((tt, HG_INNER), col(P_HQ)), pl.BlockSpec((tt, HG_INNER), col(P_HF)),
                  pl.BlockSpec((tt, HG_INNER), col(P_HI)), pl.BlockSpec((tt, HG_INNER), col(P_HG)),
                  vec, vec, vec, pl.BlockSpec((1, HG_V), const2),
                  pl.BlockSpec((1, HG_H, HG_K, HG_V), lambda i, j: (i, 0, 0, 0))],
        out_specs=[pl.BlockSpec((tt, HG_INNER), lambda i, j: (row(i, j), 0)),
                   pl.BlockSpec((1, HG_H, HG_K, HG_V), lambda i, j: (i, 0, 0, 0))],
        out_shape=[jax.ShapeDtypeStruct((b * t, HG_INNER), out_dtype),
                   jax.ShapeDtypeStruct((b, HG_H, HG_K, HG_V), F32)],
        scratch_shapes=[pltpu.VMEM((HG_H, HG_V, HG_K), F32)],
        compiler_params=_cparams(("parallel", "arbitrary")),
        name="hgrn2",
    )(p, p, p, p, jnp.log(lb).reshape(1, -1), jnp.log1p(-lb).reshape(1, -1), (1.0 - lb).reshape(1, -1),
      norm_g.reshape(1, -1), s0)


def _gelu_tanh(x):
    return 0.5 * x * (1.0 + jnp.tanh(math.sqrt(2.0 / math.pi) * (x + 0.044715 * (x * x * x))))


def _lru_kernel(lx_ref, ly_ref, cw_ref, cb_ref, wr_ref, br_ref, wi_ref, bi_ref, ap_ref, h0_ref, c0_ref,
                y_ref, h1_ref, c1_ref, carry_ref, xx_ref, hs_ref, *, tt, pos0):
    t = pl.program_id(1)
    nt = pl.num_programs(1)

    @pl.when(t == 0)
    def _():
        xx_ref[0:8, :] = jnp.zeros((8, LRU_W), F32)
        xx_ref[5:8, :] = c0_ref[0]
        carry_ref[...] = h0_ref[0]

    xl = _causal_conv(xx_ref, lx_ref[...], cw_ref, cb_ref, tt)
    new_hist = xx_ref[tt:tt + 8, :]
    xx_ref[0:8, :] = new_hist
    r = _sigmoid(_dot(xl, wr_ref[...]) + br_ref[...])
    ig = _sigmoid(_dot(xl, wi_ref[...]) + bi_ref[...])
    log_a = -LRU_C * r * _softplus(-ap_ref[...])
    a = jnp.exp(log_a)
    mult = jnp.sqrt(jnp.tanh(-log_a) * (a * a + 1.0))
    pos = pos0 + t * tt + _iota((tt, LRU_W), 0)
    u = jnp.where(pos == 0, 1.0, mult) * (ig * xl)
    row8 = _iota((8, LRU_W), 0)
    carry = carry_ref[...]
    for gi in range(tt // 8):
        a8, u8 = a[gi * 8:(gi + 1) * 8], u[gi * 8:(gi + 1) * 8]
        for k in (1, 2, 4):
            m = row8 >= k
            u8, a8 = (jnp.where(m, a8 * pltpu.roll(u8, k, 0) + u8, u8),
                      jnp.where(m, a8 * pltpu.roll(a8, k, 0), a8))
        h8 = u8 + a8 * carry
        carry = h8[7:8, :]
        hs_ref[gi * 8:(gi + 1) * 8, :] = h8
    carry_ref[...] = carry
    y_ref[...] = (hs_ref[...] * _gelu_tanh(ly_ref[...])).astype(y_ref.dtype)

    @pl.when(t == nt - 1)
    def _():
        h1_ref[0] = carry
        c1_ref[0] = new_hist[5:8, :]


def _block_diag(w):
    nb, di, do = w.shape
    eye = jnp.eye(nb, dtype=w.dtype)
    return (eye[:, None, :, None] * w[:, :, None, :]).reshape(nb * di, nb * do)


def _lru_call(p, b, t, pos0, conv_w, conv_b, w_r, b_r, w_i, b_i, a_param, h0, c0, out_dtype):
    tt = min(t, 256)
    nt = t // tt
    row = lambda i, j: i * nt + j
    const2 = lambda i, j: (0, 0)
    vec = pl.BlockSpec((1, LRU_W), const2)
    mat = pl.BlockSpec((LRU_W, LRU_W), const2)
    y, h1, c1 = pl.pallas_call(
        functools.partial(_lru_kernel, tt=tt, pos0=pos0),
        grid=(b, nt),
        in_specs=[pl.BlockSpec((tt, LRU_W), lambda i, j: (row(i, j), P_LX // LRU_W)),
                  pl.BlockSpec((tt, LRU_W), lambda i, j: (row(i, j), P_LY // LRU_W)),
                  pl.BlockSpec((CONV_W, LRU_W), const2), vec, mat, vec, mat, vec, vec,
                  pl.BlockSpec((1, 1, LRU_W), lambda i, j: (i, 0, 0)),
                  pl.BlockSpec((1, CONV_W - 1, LRU_W), lambda i, j: (i, 0, 0))],
        out_specs=[pl.BlockSpec((tt, LRU_W), lambda i, j: (row(i, j), 0)),
                   pl.BlockSpec((1, 1, LRU_W), lambda i, j: (i, 0, 0)),
                   pl.BlockSpec((1, CONV_W - 1, LRU_W), lambda i, j: (i, 0, 0))],
        out_shape=[jax.ShapeDtypeStruct((b * t, LRU_W), out_dtype),
                   jax.ShapeDtypeStruct((b, 1, LRU_W), F32),
                   jax.ShapeDtypeStruct((b, CONV_W - 1, LRU_W), F32)],
        scratch_shapes=[pltpu.VMEM((1, LRU_W), F32),
                        pltpu.VMEM((tt + 8, LRU_W), F32),
                        pltpu.VMEM((tt, LRU_W), F32)],
        compiler_params=_cparams(("parallel", "arbitrary")),
        name="rglru",
    )(p, p, conv_w, conv_b.reshape(1, -1), _block_diag(w_r).astype(BF16), b_r.reshape(1, -1),
      _block_diag(w_i).astype(BF16), b_i.reshape(1, -1), a_param.reshape(1, -1),
      h0.reshape(b, 1, LRU_W), c0)
    return y, h1.reshape(b, LRU_W), c1


MLA_SCALE = (MLA_NOPE + MLA_ROPE) ** -0.5
ROPE_HALF = MLA_ROPE // 2


def _rope_rotate(x, cos, sin):
    lane = _iota(x.shape, 1)
    rot = jnp.where(lane % MLA_ROPE < ROPE_HALF, -pltpu.roll(x, LANE - ROPE_HALF, 1), pltpu.roll(x, ROPE_HALF, 1))
    return x * cos + rot * sin


MLA_QK = MLA_KVL + LANE
MLA_PAGES_PER_STEP = 32


def _mla_prep_kernel(qd_ref, kvd_ref, krr_ref, cos_ref, sin_ref, qg_ref, wuq_ref, kvg_ref, wukt_ref,
                     qcat_ref, ckv_ref, krope_ref, kcat_ref):
    qd = qd_ref[...]
    qn = qd * lax.rsqrt(jnp.mean(qd * qd, axis=-1, keepdims=True) + RMS_EPS) * qg_ref[...]
    q = _dot(qn, wuq_ref[...])
    cos, sin = cos_ref[...], sin_ref[...]
    nn = MLA_H * MLA_NOPE
    lane = _iota((q.shape[0], LANE), 1)
    halves = [_rope_rotate(q[:, nn + i * LANE:nn + (i + 1) * LANE], cos, sin) for i in range(2)]
    per_half = LANE // MLA_ROPE
    for h in range(MLA_H):
        qcat_ref[h, :, :MLA_KVL] = _dot(q[:, h * MLA_NOPE:(h + 1) * MLA_NOPE], wukt_ref[h]).astype(qcat_ref.dtype)
        half, sh = halves[h // per_half], (h % per_half) * MLA_ROPE
        piece = pltpu.roll(half, LANE - sh, 1) if sh else half
        qcat_ref[h, :, MLA_KVL:] = jnp.where(lane < MLA_ROPE, piece, 0.0).astype(qcat_ref.dtype)
    kvd = kvd_ref[...]
    ckv = kvd * lax.rsqrt(jnp.mean(kvd * kvd, axis=-1, keepdims=True) + RMS_EPS) * kvg_ref[...]
    ckv_ref[...] = ckv
    kr = jnp.where(lane < MLA_ROPE, _rope_rotate(krr_ref[...], cos, sin), 0.0)
    krope_ref[...] = kr[:, :MLA_ROPE]
    kcat_ref[:, :MLA_KVL] = ckv.astype(BF16)
    kcat_ref[:, MLA_KVL:] = kr.astype(BF16)


def _lane_wide(v, n):
    if n == LANE:
        return v
    return pltpu.repeat(v, n // LANE, axis=1) if n % LANE == 0 else v[:, :n]


def _softmax_step(s, vals, m_ref, l_ref, acc_ref):
    m_old = m_ref[...]
    m_new = jnp.maximum(m_old, jnp.max(s, axis=-1, keepdims=True))
    pr = jnp.exp(s - _lane_wide(m_new, s.shape[1]))
    alpha = jnp.exp(m_old - m_new)
    l_ref[...] = alpha * l_ref[...] + jnp.sum(pr, axis=-1, keepdims=True)
    acc_ref[...] = _lane_wide(alpha, acc_ref.shape[1]) * acc_ref[...] + _dot(pr, vals)
    m_ref[...] = m_new


def _softmax_init(m_ref, l_ref, acc_ref):
    m_ref[...] = jnp.full(m_ref.shape, NEG, F32)
    l_ref[...] = jnp.zeros(l_ref.shape, F32)
    acc_ref[...] = jnp.zeros(acc_ref.shape, F32)


def _mla_attn_prompt_kernel(q_ref, k_ref, wuv_ref, y_ref, m_ref, l_ref, acc_ref, *, tq):
    i, j = pl.program_id(1), pl.program_id(2)
    rows = MLA_H * tq

    @pl.when(j == 0)
    def _():
        _softmax_init(m_ref, l_ref, acc_ref)

    def step(diagonal):
        k = k_ref[...]
        s = _dot_nt(q_ref[...].reshape(rows, MLA_QK), k) * MLA_SCALE
        if diagonal:
            s = jnp.where(_iota((rows, tq), 1) <= _iota((rows, tq), 0) % tq, s, NEG)
        _softmax_step(s, k[:, :MLA_KVL], m_ref, l_ref, acc_ref)

    @pl.when(j < i)
    def _():
        step(False)

    @pl.when(j == i)
    def _():
        step(True)
        o = acc_ref[...] / _lane_wide(l_ref[...], MLA_KVL)
        for h in range(MLA_H):
            y_ref[:, h * MLA_V:(h + 1) * MLA_V] = _dot(o[h * tq:(h + 1) * tq], wuv_ref[h]).astype(y_ref.dtype)


def _mla_attn_sample_kernel(pt_ref, *refs, pp, tq):
    del pt_ref
    lat_refs, rope_refs = refs[:pp], refs[pp:2 * pp]
    q_ref, ckv_ref, krope_ref, wuv_ref, y_ref, m_ref, l_ref, acc_ref = refs[2 * pp:]
    g = pl.program_id(1)
    ng = pl.num_programs(1)
    rows = MLA_H * tq
    qcat = q_ref[...].reshape(rows, MLA_QK)
    q, qr = qcat[:, :MLA_KVL], qcat[:, MLA_KVL:MLA_KVL + MLA_ROPE]

    @pl.when(g == 0)
    def _():
        _softmax_init(m_ref, l_ref, acc_ref)

    kc = jnp.concatenate([r[...] for r in lat_refs], axis=0).astype(BF16)
    krt = jnp.concatenate([r[...] for r in rope_refs], axis=1)
    _softmax_step((_dot_nt(q, kc) + _dot(qr, krt)) * MLA_SCALE, kc, m_ref, l_ref, acc_ref)

    @pl.when(g == ng - 1)
    def _():
        kc_new, kr_new = ckv_ref[...], krope_ref[...]
        s = (_dot_nt(q, kc_new) + _dot_nt(qr, kr_new)) * MLA_SCALE
        visible = _iota((rows, tq), 1) <= _iota((rows, tq), 0) % tq
        _softmax_step(jnp.where(visible, s, NEG), kc_new, m_ref, l_ref, acc_ref)
        o = acc_ref[...] / _lane_wide(l_ref[...], MLA_KVL)
        for h in range(MLA_H):
            y_ref[:, h * MLA_V:(h + 1) * MLA_V] = _dot(o[h * tq:(h + 1) * tq], wuv_ref[h]).astype(y_ref.dtype)


def _rope_tables(pos):
    inv = ROPE_THETA ** (-jnp.arange(ROPE_HALF, dtype=F32) * 2.0 / MLA_ROPE)
    ang = pos.astype(F32)[:, None] * inv[None, :]
    reps = LANE // ROPE_HALF
    return jnp.tile(jnp.cos(ang), (1, reps)), jnp.tile(jnp.sin(ang), (1, reps))


def _mla_call(p, b, t, pos0, q_norm_g, w_uq, kv_norm_g, w_ukv, cache_c, cache_r, page_table, layer, out_dtype):
    m = b * t
    tm = min(m, 256)
    wq = w_uq.reshape(MLA_QL, MLA_H, MLA_NOPE + MLA_ROPE)
    wq = jnp.concatenate([wq[..., :MLA_NOPE].reshape(MLA_QL, -1), wq[..., MLA_NOPE:].reshape(MLA_QL, -1)], axis=1)
    wkv = w_ukv.reshape(MLA_KVL, MLA_H, MLA_NOPE + MLA_V)
    wukt = jnp.transpose(wkv[..., :MLA_NOPE], (1, 2, 0)).astype(BF16)
    wuv = jnp.transpose(wkv[..., MLA_NOPE:], (1, 0, 2)).astype(BF16)
    cos, sin = _rope_tables(pos0 + jnp.arange(t))
    if t < tm:
        cos, sin = jnp.tile(cos, (tm // t, 1)), jnp.tile(sin, (tm // t, 1))
    ntab = cos.shape[0] // tm
    qdt = BF16 if t >= tm else F32
    const2 = lambda i: (0, 0)
    qcat, ckv, krope, kcat = pl.pallas_call(
        _mla_prep_kernel,
        grid=(m // tm,),
        in_specs=[pl.BlockSpec((tm, MLA_QL), lambda i: (i, P_QD // MLA_QL)),
                  pl.BlockSpec((tm, MLA_KVL), lambda i: (i, P_KVD // MLA_KVL)),
                  pl.BlockSpec((tm, LANE), lambda i: (i, P_KR // LANE)),
                  pl.BlockSpec((tm, LANE), lambda i: (i % ntab, 0)),
                  pl.BlockSpec((tm, LANE), lambda i: (i % ntab, 0)),
                  pl.BlockSpec((1, MLA_QL), const2),
                  pl.BlockSpec((MLA_QL, MLA_H * (MLA_NOPE + MLA_ROPE)), const2),
                  pl.BlockSpec((1, MLA_KVL), const2),
                  pl.BlockSpec((MLA_H, MLA_NOPE, MLA_KVL), lambda i: (0, 0, 0))],
        out_specs=[pl.BlockSpec((MLA_H, tm, MLA_QK), lambda i: (0, i, 0)),
                   pl.BlockSpec((tm, MLA_KVL), lambda i: (i, 0)),
                   pl.BlockSpec((tm, MLA_ROPE), lambda i: (i, 0)),
                   pl.BlockSpec((tm, MLA_QK), lambda i: (i, 0))],
        out_shape=[jax.ShapeDtypeStruct((MLA_H, m, MLA_QK), qdt),
                   jax.ShapeDtypeStruct((m, MLA_KVL), F32),
                   jax.ShapeDtypeStruct((m, MLA_ROPE), F32),
                   jax.ShapeDtypeStruct((m, MLA_QK), BF16)],
        compiler_params=_cparams(("parallel",)),
        name="mla_prep",
    )(p, p, p, cos, sin, q_norm_g.reshape(1, -1), wq.astype(BF16), kv_norm_g.reshape(1, -1), wukt)
    rows = MLA_H * min(t, 256)
    softmax_scratch = [pltpu.VMEM((rows, LANE), F32), pltpu.VMEM((rows, LANE), F32), pltpu.VMEM((rows, MLA_KVL), F32)]

    if cache_c is None:
        tq = min(t, 256)
        nq = t // tq
        y = pl.pallas_call(
            functools.partial(_mla_attn_prompt_kernel, tq=tq),
            grid=(b, nq, nq),
            in_specs=[pl.BlockSpec((MLA_H, tq, MLA_QK), lambda bi, i, j: (0, bi * nq + i, 0)),
                      pl.BlockSpec((tq, MLA_QK), lambda bi, i, j: (bi * nq + jnp.minimum(i, j), 0)),
                      pl.BlockSpec((MLA_H, MLA_KVL, MLA_V), lambda bi, i, j: (0, 0, 0))],
            out_specs=pl.BlockSpec((tq, MLA_H * MLA_V), lambda bi, i, j: (bi * nq + i, 0)),
            out_shape=jax.ShapeDtypeStruct((m, MLA_H * MLA_V), out_dtype),
            scratch_shapes=softmax_scratch,
            compiler_params=_cparams(("parallel", "parallel", "arbitrary")),
            name="mla_attn_prompt",
        )(qcat, kcat, wuv)
        return y, ckv, krope

    n_pages = page_table.shape[1]
    page = cache_c.shape[2]
    pp = max(d for d in range(1, MLA_PAGES_PER_STEP + 1) if n_pages % d == 0)
    page_map = lambda bi, g, pt, k: (layer, pt[bi, g * pp + k], 0, 0)
    lat_specs = [pl.BlockSpec((None, None, page, MLA_KVL), functools.partial(page_map, k=k)) for k in range(pp)]
    rope_specs = [pl.BlockSpec((None, None, MLA_ROPE, page), functools.partial(page_map, k=k)) for k in range(pp)]
    cache_rt = jnp.swapaxes(cache_r, 2, 3)
    y = pl.pallas_call(
        functools.partial(_mla_attn_sample_kernel, pp=pp, tq=t),
        grid_spec=pltpu.PrefetchScalarGridSpec(
            num_scalar_prefetch=1,
            grid=(b, n_pages // pp),
            in_specs=lat_specs + rope_specs + [
                pl.BlockSpec((MLA_H, t, MLA_QK), lambda bi, g, pt: (0, bi, 0)),
                pl.BlockSpec((t, MLA_KVL), lambda bi, g, pt: (bi, 0)),
                pl.BlockSpec((t, MLA_ROPE), lambda bi, g, pt: (bi, 0)),
                pl.BlockSpec((MLA_H, MLA_KVL, MLA_V), lambda bi, g, pt: (0, 0, 0))],
            out_specs=pl.BlockSpec((t, MLA_H * MLA_V), lambda bi, g, pt: (bi, 0)),
            scratch_shapes=softmax_scratch),
        out_shape=jax.ShapeDtypeStruct((m, MLA_H * MLA_V), out_dtype),
        compiler_params=_cparams(("parallel", "arbitrary")),
        name="mla_attn_sample",
    )(page_table, *([cache_c] * pp), *([cache_rt] * pp), qcat, ckv, krope, wuv)
    return y, ckv, krope


ROUTER_GRP_LANE = MOE_E


def _merge_kernel(gl_ref, ya_ref, yb_ref, yc_ref, yd_ref, wb_ref, wo_ref, x_ref, mod_ref, ng_ref, wr_ref, br_ref,
                  xn_ref, h2_ref, lg_ref):
    bb, tt, d = x_ref.shape
    merged = None
    for n, y_ref in enumerate((ya_ref, yb_ref, yc_ref, yd_ref)):
        term = _sigmoid(gl_ref[:, n * d:(n + 1) * d]) * _dot(y_ref[...], wb_ref[n])
        merged = term if merged is None else merged + term
    out = _dot(merged, wo_ref[...])
    mod = mod_ref[...]
    x = x_ref[...] + mod[:, 2:3, :] * out.reshape(bb, tt, d)
    xn_ref[...] = x
    h = x * lax.rsqrt(jnp.mean(x * x, axis=-1, keepdims=True) + RMS_EPS) * ng_ref[...]
    h = (h * (1.0 + mod[:, 4:5, :]) + mod[:, 3:4, :]).reshape(bb * tt, d)
    h2_ref[...] = h.astype(h2_ref.dtype)
    lg_ref[...] = _dot_hi(h, wr_ref[...]) + br_ref[...]


def _merge_call(p, ys, w_branch, w_out, x, mod, norm2_g, w_router, b_router):
    b, t, d = x.shape
    bb, tt = _row_blocks(b, t, 256)
    nt = t // tt
    tm = bb * tt
    row = lambda i, j: (i * nt + j, 0)
    yspec = pl.BlockSpec((tm, ys[0].shape[1]), row)
    return pl.pallas_call(
        _merge_kernel,
        grid=(b // bb, nt),
        in_specs=[pl.BlockSpec((tm, N_BRANCH * d), row), yspec, yspec, yspec, yspec,
                  pl.BlockSpec(w_branch.shape, lambda i, j: (0, 0, 0)),
                  pl.BlockSpec(w_out.shape, lambda i, j: (0, 0)),
                  pl.BlockSpec((bb, tt, d), lambda i, j: (i, j, 0)),
                  pl.BlockSpec((bb, mod.shape[1], d), lambda i, j: (i, 0, 0)),
                  pl.BlockSpec((1, 1, d), lambda i, j: (0, 0, 0)),
                  pl.BlockSpec((d, LANE), lambda i, j: (0, 0)),
                  pl.BlockSpec((1, LANE), lambda i, j: (0, 0))],
        out_specs=[pl.BlockSpec((bb, tt, d), lambda i, j: (i, j, 0)),
                   pl.BlockSpec((tm, d), row),
                   pl.BlockSpec((tm, LANE), row)],
        out_shape=[jax.ShapeDtypeStruct((b, t, d), F32),
                   jax.ShapeDtypeStruct((b * t, d), BF16),
                   jax.ShapeDtypeStruct((b * t, LANE), F32)],
        compiler_params=_cparams(("parallel", "parallel")),
        name="merge",
    )(p, *ys, w_branch, w_out, x, mod, norm2_g.reshape(1, 1, d), w_router, b_router)


def _route(logits):
    lane = _iota(logits.shape, 1)
    big = jnp.int32(1 << 20)
    grp = jnp.where(jnp.right_shift(lane, 2) == ROUTER_GRP_LANE // MOE_G, logits, NEG)
    gmax = jnp.max(grp, axis=-1, keepdims=True)
    g_top = 1.0 / jnp.sum(jnp.exp(grp - gmax), axis=-1, keepdims=True)
    gidx = jnp.min(jnp.where(grp == gmax, lane, big), axis=-1, keepdims=True) - ROUTER_GRP_LANE
    el = jnp.where(jnp.right_shift(lane, 3) == gidx, logits, NEG)
    m1 = jnp.max(el, axis=-1, keepdims=True)
    i1 = jnp.min(jnp.where(el == m1, lane, big), axis=-1, keepdims=True)
    el2 = jnp.where(lane == i1, NEG, el)
    m2 = jnp.max(el2, axis=-1, keepdims=True)
    i2 = jnp.min(jnp.where(el2 == m2, lane, big), axis=-1, keepdims=True)
    e2 = jnp.exp(m2 - m1)
    w1 = g_top / (1.0 + e2)
    return jnp.where(lane == i1, w1, 0.0) + jnp.where(lane == i2, w1 * e2, 0.0)


def _moe_dense_kernel(h_ref, lg_ref, wg_ref, wu_ref, wd_ref, x_ref, mod_ref, o_ref, comb_ref, acc_ref):
    e = pl.program_id(1)
    ne = pl.num_programs(1)
    bb, tt, d = x_ref.shape

    @pl.when(e == 0)
    def _():
        comb_ref[...] = _route(lg_ref[...])
        acc_ref[...] = jnp.zeros(acc_ref.shape, F32)

    comb = comb_ref[...]
    ce = jnp.sum(jnp.where(_iota(comb.shape, 1) == e, comb, 0.0), axis=-1, keepdims=True)
    h = h_ref[...]
    hid = _silu(_dot(h, wg_ref[...])) * _dot(h, wu_ref[...]) * ce
    acc_ref[...] += _dot(hid, wd_ref[...])

    @pl.when(e == ne - 1)
    def _():
        o_ref[...] = x_ref[...] + mod_ref[...][:, 5:6, :] * acc_ref[...].reshape(bb, tt, d)


def _moe_call(h2, logits, w_gate, w_up, w_down, x, mod):
    b, t, d = x.shape
    bb, tt = _row_blocks(b, t, 512)
    nt = t // tt
    tm = bb * tt
    ne, _, hid = w_gate.shape
    row = lambda i, e: (i, 0)
    xmap = lambda i, e: (i // nt, i % nt, 0)
    return pl.pallas_call(
        _moe_dense_kernel,
        grid=(b * t // tm, ne),
        in_specs=[pl.BlockSpec((tm, d), row), pl.BlockSpec((tm, LANE), row),
                  pl.BlockSpec((None, d, hid), lambda i, e: (e, 0, 0)),
                  pl.BlockSpec((None, d, hid), lambda i, e: (e, 0, 0)),
                  pl.BlockSpec((None, hid, d), lambda i, e: (e, 0, 0)),
                  pl.BlockSpec((bb, tt, d), xmap),
                  pl.BlockSpec((bb, mod.shape[1], d), lambda i, e: (i // nt, 0, 0))],
        out_specs=pl.BlockSpec((bb, tt, d), xmap),
        out_shape=jax.ShapeDtypeStruct((b, t, d), F32),
        scratch_shapes=[pltpu.VMEM((tm, LANE), F32), pltpu.VMEM((tm, d), F32)],
        compiler_params=_cparams(("parallel", "arbitrary")),
        name="moe",
    )(h2, logits, w_gate, w_up, w_down, x, mod)


def _layer(x, mod, pos0, lw, state, cache, out_dtype):
    b, t, d = x.shape
    ssm0, ssm_conv0, hg0, lru0, lru_conv0 = state
    h = _prenorm_call(x, lw['norm1_g'], mod, sh_row=0, sc_row=1)
    p = _matmul_call(h, lw['w_in'])
    y_a, ssm1, ssm_conv1 = _ssd_call(p, b, t, lw['ssd_conv_w'], lw['ssd_conv_b'], lw['ssd_dt_bias'], lw['ssd_a_log'],
                                     lw['ssd_d'], lw['ssd_norm_g'], ssm0, ssm_conv0, out_dtype)
    y_b, hg1 = _hgrn_call(p, b, t, lw['hg_lb'], lw['hg_norm_g'], hg0, out_dtype)
    y_c, ckv, krope = _mla_call(p, b, t, pos0, lw['mla_q_norm_g'], lw['mla_w_uq'], lw['mla_kv_norm_g'],
                                lw['mla_w_ukv'], *cache, out_dtype)
    y_d, lru1, lru_conv1 = _lru_call(p, b, t, pos0, lw['lru_conv_w'], lw['lru_conv_b'], lw['lru_w_r'], lw['lru_b_r'],
                                     lw['lru_w_i'], lw['lru_b_i'], lw['lru_a'], lru0, lru_conv0, out_dtype)
    x, h2, logits = _merge_call(p, (y_a, y_b, y_c, y_d), lw['w_branch'], lw['w_out'], x, mod, lw['norm2_g'],
                                lw['w_router'], lw['b_router'])
    x = _moe_call(h2, logits, lw['moe_w_gate'], lw['moe_w_up'], lw['moe_w_down'], x, mod)
    new = (ckv.reshape(b, t, -1), krope.reshape(b, t, -1), ssm1, ssm_conv1, hg1, lru1, lru_conv1)
    return x, new


def kernel(x_prompt, x_sample, c_prompt, c_sample, cache_kv_latent, cache_k_rope, state_ssm, state_ssm_conv,
           state_hgrn, state_lru, state_lru_conv, page_table, norm1_g, norm2_g, w_mod, b_mod, w_in, ssd_conv_w,
           ssd_conv_b, ssd_dt_bias, ssd_a_log, ssd_d, ssd_norm_g, hg_lb_raw, hg_norm_g, mla_q_norm_g, mla_w_uq,
           mla_kv_norm_g, mla_w_ukv, lru_conv_w, lru_conv_b, lru_w_r, lru_b_r, lru_w_i, lru_b_i, lru_a, w_branch,
           w_out, moe_w_grp, moe_b_grp, moe_w_rt, moe_b_rt, moe_w_gate, moe_w_up, moe_w_down, final_norm_g):
    bp, tp, d = x_prompt.shape
    bs, ts, _ = x_sample.shape
    depth = w_in.shape[0]
    n_past = page_table.shape[1] * cache_kv_latent.shape[2]
    lb_all = jnp.cumsum(jax.nn.softmax(hg_lb_raw.astype(F32), axis=0), axis=0)
    lb_all = lb_all - lb_all[:1]
    c_all = jnp.concatenate([c_prompt, c_sample], axis=0)
    zeros = lambda *s: jnp.zeros(s, F32)
    yp, ys = x_prompt, x_sample
    p_new, s_new = [], []
    for l in range(depth):
        mod = _mod_call(c_all, w_mod[l].astype(BF16), b_mod[l]).reshape(bp + bs, 6, d)
        pad_r = LANE - MOE_E - MOE_G
        lw = {
            'norm1_g': norm1_g[l], 'norm2_g': norm2_g[l], 'w_in': _pack_w_in(w_in[l]),
            'ssd_conv_w': ssd_conv_w[l], 'ssd_conv_b': ssd_conv_b[l], 'ssd_dt_bias': ssd_dt_bias[l],
            'ssd_a_log': ssd_a_log[l], 'ssd_d': ssd_d[l], 'ssd_norm_g': ssd_norm_g[l],
            'hg_lb': lb_all[l], 'hg_norm_g': hg_norm_g[l],
            'mla_q_norm_g': mla_q_norm_g[l], 'mla_w_uq': mla_w_uq[l],
            'mla_kv_norm_g': mla_kv_norm_g[l], 'mla_w_ukv': mla_w_ukv[l],
            'lru_conv_w': lru_conv_w[l], 'lru_conv_b': lru_conv_b[l], 'lru_w_r': lru_w_r[l], 'lru_b_r': lru_b_r[l],
            'lru_w_i': lru_w_i[l], 'lru_b_i': lru_b_i[l], 'lru_a': lru_a[l],
            'w_branch': w_branch[l].astype(BF16), 'w_out': w_out[l].astype(BF16),
            'w_router': jnp.pad(jnp.concatenate([moe_w_rt[l], moe_w_grp[l]], axis=1), ((0, 0), (0, pad_r))),
            'b_router': jnp.pad(jnp.concatenate([moe_b_rt[l], moe_b_grp[l]]), (0, pad_r)).reshape(1, LANE),
            'moe_w_gate': moe_w_gate[l].astype(BF16), 'moe_w_up': moe_w_up[l].astype(BF16),
            'moe_w_down': moe_w_down[l].astype(BF16),
        }
        p_state = (zeros(bp, SSD_H, SSD_P, SSD_N), zeros(bp, CONV_W - 1, SSD_CONV), zeros(bp, HG_H, HG_K, HG_V),
                   zeros(bp, LRU_W), zeros(bp, CONV_W - 1, LRU_W))
        s_state = (state_ssm[l], state_ssm_conv[l], state_hgrn[l], state_lru[l], state_lru_conv[l])
        yp, pn = _layer(yp, mod[:bp], 0, lw, p_state, (None, None, None, l), BF16)
        ys, sn = _layer(ys, mod[bp:], n_past, lw, s_state, (cache_kv_latent, cache_k_rope, page_table, l), F32)
        p_new.append(pn)
        s_new.append(sn)
    no_mod = zeros(1, 2, d)
    yp = _prenorm_call(yp, final_norm_g, jnp.broadcast_to(no_mod, (bp, 2, d)), 0, 1, F32).reshape(bp, tp, d)
    ys = _prenorm_call(ys, final_norm_g, jnp.broadcast_to(no_mod, (bs, 2, d)), 0, 1, F32).reshape(bs, ts, d)
    stk = lambda news, j: jnp.stack([n[j] for n in news])
    return (yp, ys) + tuple(stk(p_new, j) for j in range(7)) + tuple(stk(s_new, j) for j in range(7))
```

```python
import functools
import math

import jax
import jax.numpy as jnp
from jax import lax
from jax.experimental import pallas as pl
from jax.experimental.pallas import tpu as pltpu

F32 = jnp.float32
BF16 = jnp.bfloat16
HI = lax.Precision.HIGHEST
NEG = -1e30

RMS_EPS = 1e-6
D_MODEL = 1024
CONV_W = 4
SSD_H, SSD_P, SSD_G, SSD_N = 8, 64, 2, 64
SSD_INNER = SSD_H * SSD_P
SSD_CONV = SSD_INNER + 2 * SSD_G * SSD_N
SSD_CHUNK = 64
HG_H, HG_K, HG_V = 4, 128, 128
HG_INNER = HG_H * HG_V
HG_CHUNK = 16
MLA_H, MLA_QL, MLA_KVL, MLA_NOPE, MLA_ROPE, MLA_V = 8, 256, 256, 64, 32, 64
ROPE_THETA = 10000.0
LRU_W, LRU_NB = 512, 8
LRU_C = 8.0
N_BRANCH = 4
MOE_G, MOE_EPG, MOE_E, MOE_HID = 4, 8, 32, 256

V7X_VMEM_LIMIT = 56 * 1024 * 1024
LANE = 128

P_GL, P_Z, P_HQ, P_HF, P_HI, P_HG, P_LX, P_LY = 0, 4096, 4608, 5120, 5632, 6144, 6656, 7168
P_XBC, P_QD, P_KVD, P_DT, P_KR = 7680, 8448, 8704, 8960, 9088
P_TOTAL = 9216


def _cparams(sem):
    return pltpu.CompilerParams(dimension_semantics=sem, vmem_limit_bytes=V7X_VMEM_LIMIT)


def _silu(x):
    return x * (1.0 / (1.0 + jnp.exp(-x)))


def _sigmoid(x):
    return 1.0 / (1.0 + jnp.exp(-x))


def _softplus(x):
    return jnp.maximum(x, 0.0) + jnp.log(1.0 + jnp.exp(-jnp.abs(x)))


def _iota(shape, dim):
    return lax.broadcasted_iota(jnp.int32, shape, dim)


def _dot(a, b):
    return jnp.dot(a.astype(BF16), b.astype(BF16), preferred_element_type=F32)


def _dot_nt(a, b):
    return lax.dot_general(a.astype(BF16), b.astype(BF16), (((1,), (1,)), ((), ())), preferred_element_type=F32)


def _dot_tn(a, b):
    return lax.dot_general(a.astype(BF16), b.astype(BF16), (((0,), (0,)), ((), ())), preferred_element_type=F32)


def _dot_hi(a, b):
    return jnp.dot(a, b, precision=HI, preferred_element_type=F32)


def _dot_nt_hi(a, b):
    return lax.dot_general(a, b, (((1,), (1,)), ((), ())), precision=HI, preferred_element_type=F32)


def _tril(n):
    return (_iota((n, n), 0) >= _iota((n, n), 1)).astype(F32)


def _eye(n):
    return (_iota((n, n), 0) == _iota((n, n), 1)).astype(F32)


def _row_blocks(b, t, target):
    if t >= target:
        return 1, target
    return min(b, target // t), t


def _mod_kernel(c_ref, w_ref, b_ref, o_ref):
    o_ref[...] = _dot(_silu(c_ref[...]), w_ref[...]) + b_ref[...]


def _mod_call(c, w, b):
    m, d = c.shape
    n = w.shape[1]
    tn = 1536
    return pl.pallas_call(
        _mod_kernel,
        grid=(n // tn,),
        in_specs=[pl.BlockSpec((m, d), lambda j: (0, 0)),
                  pl.BlockSpec((d, tn), lambda j: (0, j)),
                  pl.BlockSpec((1, tn), lambda j: (0, j))],
        out_specs=pl.BlockSpec((m, tn), lambda j: (0, j)),
        out_shape=jax.ShapeDtypeStruct((m, n), F32),
        compiler_params=_cparams(("parallel",)),
        name="adaln_mod",
    )(c, w, b.reshape(1, n))


def _prenorm_kernel(x_ref, g_ref, mod_ref, o_ref, *, sh_row, sc_row):
    x = x_ref[...]
    bb, tt, d = x.shape
    y = x * lax.rsqrt(jnp.mean(x * x, axis=-1, keepdims=True) + RMS_EPS) * g_ref[...]
    mod = mod_ref[...]
    y = y * (1.0 + mod[:, sc_row:sc_row + 1, :]) + mod[:, sh_row:sh_row + 1, :]
    o_ref[...] = y.reshape(bb * tt, d).astype(o_ref.dtype)


def _prenorm_call(x, g, mod, sh_row, sc_row, out_dtype=BF16):
    b, t, d = x.shape
    bb, tt = _row_blocks(b, t, 512)
    nt = t // tt
    return pl.pallas_call(
        functools.partial(_prenorm_kernel, sh_row=sh_row, sc_row=sc_row),
        grid=(b // bb, nt),
        in_specs=[pl.BlockSpec((bb, tt, d), lambda i, j: (i, j, 0)),
                  pl.BlockSpec((1, 1, d), lambda i, j: (0, 0, 0)),
                  pl.BlockSpec((bb, mod.shape[1], d), lambda i, j: (i, 0, 0))],
        out_specs=pl.BlockSpec((bb * tt, d), lambda i, j: (i * nt + j, 0)),
        out_shape=jax.ShapeDtypeStruct((b * t, d), out_dtype),
        compiler_params=_cparams(("parallel", "parallel")),
        name="prenorm",
    )(x, g.reshape(1, 1, d), mod)


def _matmul_kernel(x_ref, w_ref, o_ref):
    o_ref[...] = jnp.dot(x_ref[...], w_ref[...], preferred_element_type=F32)


def _matmul_call(x, w, tm=512, tn=1024):
    m, k = x.shape
    n = w.shape[1]
    tm = min(tm, m)
    return pl.pallas_call(
        _matmul_kernel,
        grid=(n // tn, m // tm),
        in_specs=[pl.BlockSpec((tm, k), lambda j, i: (i, 0)),
                  pl.BlockSpec((k, tn), lambda j, i: (0, j))],
        out_specs=pl.BlockSpec((tm, tn), lambda j, i: (i, j)),
        out_shape=jax.ShapeDtypeStruct((m, n), F32),
        compiler_params=_cparams(("parallel", "parallel")),
        name="in_proj",
    )(x, w)


def _pack_w_in(w_in):
    d = w_in.shape[0]
    sizes = (SSD_INNER, SSD_CONV, SSD_H, HG_H * HG_K, HG_H * HG_K, HG_INNER, HG_INNER,
             MLA_QL, MLA_KVL, MLA_ROPE, LRU_W, LRU_W, N_BRANCH * D_MODEL)
    offs = [0]
    for s in sizes:
        offs.append(offs[-1] + s)
    z, xbc, dt, hq, hf, hi, hg, qd, kvd, krr, lx, ly, gl = [w_in[:, offs[i]:offs[i + 1]] for i in range(13)]
    pad = lambda a: jnp.pad(a, ((0, 0), (0, LANE - a.shape[1])))
    return jnp.concatenate([gl, z, hq, hf, hi, hg, lx, ly, xbc, qd, kvd, pad(dt), pad(krr)], axis=1).astype(BF16)


def _causal_conv(xx_ref, x, w_ref, b_ref, tt):
    xx_ref[8:8 + tt, :] = x
    w = w_ref[...]
    y = b_ref[...] + w[3:4, :] * x
    for k in range(1, CONV_W):
        y = y + w[3 - k:4 - k, :] * xx_ref[8 - k:8 - k + tt, :]
    return y


def _ssd_kernel(z_ref, xbc_ref, dt_ref, cw_ref, cb_ref, dtb_ref, alog_ref, dfull_ref, ng_ref, s0_ref, c0_ref,
                y_ref, s1_ref, c1_ref, st_ref, xx_ref, yc_ref, *, tt, cl):
    t = pl.program_id(1)
    nt = pl.num_programs(1)
    hp = SSD_H // SSD_G * SSD_P
    eye = _eye(SSD_N)

    @pl.when(t == 0)
    def _():
        xx_ref[0:8, :] = jnp.zeros((8, SSD_CONV), F32)
        xx_ref[5:8, :] = c0_ref[0]
        for h in range(SSD_H):
            g, r = divmod(h, SSD_H // SSD_G)
            st_ref[g, :, r * SSD_P:(r + 1) * SSD_P] = _dot_nt_hi(eye, s0_ref[0, h])

    conv = _causal_conv(xx_ref, xbc_ref[...], cw_ref, cb_ref, tt)
    new_hist = xx_ref[tt:tt + 8, :]
    xx_ref[0:8, :] = new_hist
    act = _silu(conv)
    xs = act[:, :SSD_INNER]
    bm = act[:, SSD_INNER:SSD_INNER + SSD_G * SSD_N]
    cm = act[:, SSD_INNER + SSD_G * SSD_N:]
    dt = _softplus(dt_ref[...] + dtb_ref[...])
    da = dt * (-jnp.exp(alog_ref[...]))
    expand = (_iota((LANE, SSD_INNER), 1) // SSD_P == _iota((LANE, SSD_INNER), 0)).astype(F32)
    dtf = _dot_hi(dt, expand)
    ti, si = _iota((tt, tt), 0), _iota((tt, tt), 1)
    same_chunk = ti // cl == si // cl
    acs_all = _dot_hi((same_chunk & (ti >= si)).astype(F32), da)
    acs_t = lax.dot_general(da, (same_chunk & (ti <= si)).astype(F32), (((0,), (0,)), ((), ())),
                            precision=HI, preferred_element_type=F32)
    last_all = _dot_hi((si == ti // cl * cl + (cl - 1)).astype(F32), acs_all)
    eacs_all = jnp.exp(_dot_hi(acs_all, expand))
    dend_all = jnp.exp(_dot_hi(last_all - acs_all, expand))
    xdt_all = xs * dtf
    xsc_all = xdt_all * dend_all
    tri = _iota((cl, cl), 0) >= _iota((cl, cl), 1)

    for c in range(tt // cl):
        rows = slice(c * cl, (c + 1) * cl)
        eacs = eacs_all[rows]
        xdt = xdt_all[rows]
        xsc = xsc_all[rows]
        for g in range(SSD_G):
            bg = bm[rows, g * SSD_N:(g + 1) * SSD_N]
            cg = cm[rows, g * SSD_N:(g + 1) * SSD_N]
            cb = _dot_nt(cg, bg)
            sg = st_ref[g]
            y_inter = _dot(cg, sg) * eacs[:, g * hp:(g + 1) * hp]
            for r in range(SSD_H // SSD_G):
                h = g * (SSD_H // SSD_G) + r
                seg = acs_all[rows, h:h + 1] - acs_t[h:h + 1, rows]
                decay = jnp.exp(jnp.where(tri, seg, NEG))
                y_h = _dot(cb * decay, xdt[:, h * SSD_P:(h + 1) * SSD_P])
                yc_ref[rows, h * SSD_P:(h + 1) * SSD_P] = y_h + y_inter[:, r * SSD_P:(r + 1) * SSD_P]
            cdec = eacs[cl - 1:cl, g * hp:(g + 1) * hp]
            st_ref[g] = cdec * sg + _dot_tn(bg, xsc[:, g * hp:(g + 1) * hp])

    y = yc_ref[...] + dfull_ref[...] * xs
    yz = y * _silu(z_ref[...])
    out = yz * lax.rsqrt(jnp.mean(yz * yz, axis=-1, keepdims=True) + RMS_EPS) * ng_ref[...]
    y_ref[...] = out.astype(y_ref.dtype)

    @pl.when(t == nt - 1)
    def _():
        c1_ref[0] = new_hist[5:8, :]
        for h in range(SSD_H):
            g, r = divmod(h, SSD_H // SSD_G)
            s1_ref[0, h] = _dot_nt_hi(eye, st_ref[g, :, r * SSD_P:(r + 1) * SSD_P])


def _ssd_call(p, b, t, conv_w, conv_b, dt_bias, a_log, d_skip, norm_g, s0, c0, out_dtype):
    tt = min(t, 256)
    cl = SSD_CHUNK if tt % SSD_CHUNK == 0 else tt
    nt = t // tt
    padl = lambda v: jnp.pad(v, (0, LANE - v.shape[0])).reshape(1, LANE)
    row = lambda i, j: i * nt + j
    const2 = lambda i, j: (0, 0)
    return pl.pallas_call(
        functools.partial(_ssd_kernel, tt=tt, cl=cl),
        grid=(b, nt),
        in_specs=[pl.BlockSpec((tt, SSD_INNER), lambda i, j: (row(i, j), P_Z // SSD_INNER)),
                  pl.BlockSpec((tt, SSD_CONV), lambda i, j: (row(i, j), P_XBC // SSD_CONV)),
                  pl.BlockSpec((tt, LANE), lambda i, j: (row(i, j), P_DT // LANE)),
                  pl.BlockSpec((CONV_W, SSD_CONV), const2),
                  pl.BlockSpec((1, SSD_CONV), const2),
                  pl.BlockSpec((1, LANE), const2),
                  pl.BlockSpec((1, LANE), const2),
                  pl.BlockSpec((1, SSD_INNER), const2),
                  pl.BlockSpec((1, SSD_INNER), const2),
                  pl.BlockSpec((1, SSD_H, SSD_P, SSD_N), lambda i, j: (i, 0, 0, 0)),
                  pl.BlockSpec((1, CONV_W - 1, SSD_CONV), lambda i, j: (i, 0, 0))],
        out_specs=[pl.BlockSpec((tt, SSD_INNER), lambda i, j: (row(i, j), 0)),
                   pl.BlockSpec((1, SSD_H, SSD_P, SSD_N), lambda i, j: (i, 0, 0, 0)),
                   pl.BlockSpec((1, CONV_W - 1, SSD_CONV), lambda i, j: (i, 0, 0))],
        out_shape=[jax.ShapeDtypeStruct((b * t, SSD_INNER), out_dtype),
                   jax.ShapeDtypeStruct((b, SSD_H, SSD_P, SSD_N), F32),
                   jax.ShapeDtypeStruct((b, CONV_W - 1, SSD_CONV), F32)],
        scratch_shapes=[pltpu.VMEM((SSD_G, SSD_N, SSD_H // SSD_G * SSD_P), F32),
                        pltpu.VMEM((tt + 8, SSD_CONV), F32),
                        pltpu.VMEM((tt, SSD_INNER), F32)],
        compiler_params=_cparams(("parallel", "arbitrary")),
        name="ssd",
    )(p, p, p, conv_w, conv_b.reshape(1, -1), padl(dt_bias), padl(a_log),
      jnp.repeat(d_skip, SSD_P).reshape(1, -1), norm_g.reshape(1, -1), s0, c0)


def _hgrn_kernel(q_ref, f_ref, i_ref, g_ref, loglb_ref, log1m_ref, onem_ref, ng_ref, s0_ref,
                 y_ref, s1_ref, st_ref, o_ref, *, bb, tt, cl):
    t = pl.program_id(1)
    nt = pl.num_programs(1)
    rows_n = bb * tt
    hs = [slice(h * HG_K, (h + 1) * HG_K) for h in range(HG_H)]

    @pl.when(t == 0)
    def _():
        for bi in range(bb):
            for h in range(HG_H):
                st_ref[bi, h] = s0_ref[bi, h].T

    q, hf, v = q_ref[...], f_ref[...], i_ref[...]
    b_ = log1m_ref[...] - _softplus(-hf)
    loglb = loglb_ref[...]
    logf = jnp.maximum(loglb, b_) + jnp.log(1.0 + jnp.exp(-jnp.abs(loglb - b_)))
    kin = onem_ref[...] * _sigmoid(-hf)
    ti, si = _iota((rows_n, rows_n), 0), _iota((rows_n, rows_n), 1)
    bc = _dot_hi(((ti // cl == si // cl) & (ti >= si)).astype(F32), logf)
    bl = _dot_hi((si == ti // cl * cl + (cl - 1)).astype(F32), bc)
    in_chunk = _iota((rows_n, HG_INNER), 0) % cl
    p = q * kin
    o = [jnp.sum(p[:, s], axis=-1, keepdims=True) * v[:, s] for s in hs]
    for d in range(1, cl):
        kd, bd, vd = pltpu.roll(kin, d, 0), pltpu.roll(bc, d, 0), pltpu.roll(v, d, 0)
        p = q * kd * jnp.exp(jnp.where(in_chunk >= d, bc - bd, NEG))
        o = [o[h] + jnp.sum(p[:, s], axis=-1, keepdims=True) * vd[:, s] for h, s in enumerate(hs)]
    for h in range(HG_H):
        o_ref[:, hs[h]] = o[h]
    qe = q * jnp.exp(bc)
    ke = kin * jnp.exp(bl - bc)
    dec = jnp.exp(bl)
    for c in range(rows_n // cl):
        rows = slice(c * cl, (c + 1) * cl)
        bi = c * cl // tt
        for h, s in enumerate(hs):
            st = st_ref[bi, h]
            o_ref[rows, s] += _dot_nt(qe[rows, s], st)
            st_ref[bi, h] = dec[c * cl:c * cl + 1, s] * st + _dot_tn(v[rows, s], ke[rows, s])
    ng = ng_ref[...]
    outs = []
    for s in hs:
        oh = o_ref[:, s]
        outs.append(oh * lax.rsqrt(jnp.mean(oh * oh, axis=-1, keepdims=True) + RMS_EPS) * ng)
    y_ref[...] = (jnp.concatenate(outs, axis=-1) * _silu(g_ref[...])).astype(y_ref.dtype)

    @pl.when(t == nt - 1)
    def _():
        for bi in range(bb):
            for h in range(HG_H):
                s1_ref[bi, h] = st_ref[bi, h].T


def _hgrn_call(p, b, t, lb, norm_g, s0, out_dtype):
    bb, tt = _row_blocks(b, t, 256) if t >= 256 else (min(b, 16), t)
    cl = HG_CHUNK if tt % HG_CHUNK == 0 else tt
    nt = t // tt
    row = lambda i, j: i * nt + j
    const2 = lambda i, j: (0, 0)
    col = lambda off: (lambda i, j: (row(i, j), off // HG_INNER))
    vec = pl.BlockSpec((1, HG_INNER), const2)
    blk = pl.BlockSpec((bb * tt, HG_INNER), lambda i, j: (row(i, j), 0))
    return pl.pallas_call(
        functools.partial(_hgrn_kernel, bb=bb, tt=tt, cl=cl),
        grid=(b // bb, nt),
        in_specs=[pl.BlockSpec((bb * tt, HG_INNER), col(P_HQ)), pl.BlockSpec((bb * tt, HG_INNER), col(P_HF)),
                  pl.BlockSpec((bb * tt, HG_INNER), col(P_HI)), pl.BlockSpec((bb * tt, HG_INNER), col(P_HG)),
                  vec, vec, vec, pl.BlockSpec((1, HG_V), const2),
                  pl.BlockSpec((bb, HG_H, HG_K, HG_V), lambda i, j: (i, 0, 0, 0))],
        out_specs=[blk, pl.BlockSpec((bb, HG_H, HG_K, HG_V), lambda i, j: (i, 0, 0, 0))],
        out_shape=[jax.ShapeDtypeStruct((b * t, HG_INNER), out_dtype),
                   jax.ShapeDtypeStruct((b, HG_H, HG_K, HG_V), F32)],
        scratch_shapes=[pltpu.VMEM((bb, HG_H, HG_V, HG_K), F32), pltpu.VMEM((bb * tt, HG_INNER), F32)],
        compiler_params=_cparams(("parallel", "arbitrary")),
        name="hgrn2",
    )(p, p, p, p, jnp.log(lb).reshape(1, -1), jnp.log1p(-lb).reshape(1, -1), (1.0 - lb).reshape(1, -1),
      norm_g.reshape(1, -1), s0)


def _gelu_tanh(x):
    return 0.5 * x * (1.0 + jnp.tanh(math.sqrt(2.0 / math.pi) * (x + 0.044715 * (x * x * x))))


def _lru_kernel(lx_ref, ly_ref, cw_ref, cb_ref, wr_ref, br_ref, wi_ref, bi_ref, ap_ref, h0_ref, c0_ref,
                y_ref, h1_ref, c1_ref, carry_ref, xx_ref, hs_ref, *, tt, pos0):
    t = pl.program_id(1)
    nt = pl.num_programs(1)

    @pl.when(t == 0)
    def _():
        xx_ref[0:8, :] = jnp.zeros((8, LRU_W), F32)
        xx_ref[5:8, :] = c0_ref[0]
        carry_ref[...] = h0_ref[0]

    xl = _causal_conv(xx_ref, lx_ref[...], cw_ref, cb_ref, tt)
    new_hist = xx_ref[tt:tt + 8, :]
    xx_ref[0:8, :] = new_hist
    r = _sigmoid(_dot(xl, wr_ref[...]) + br_ref[...])
    ig = _sigmoid(_dot(xl, wi_ref[...]) + bi_ref[...])
    log_a = -LRU_C * r * _softplus(-ap_ref[...])
    a = jnp.exp(log_a)
    mult = jnp.sqrt(jnp.tanh(-log_a) * (a * a + 1.0))
    pos = pos0 + t * tt + _iota((tt, LRU_W), 0)
    u = jnp.where(pos == 0, 1.0, mult) * (ig * xl)
    row8 = _iota((8, LRU_W), 0)
    carry = carry_ref[...]
    for gi in range(tt // 8):
        a8, u8 = a[gi * 8:(gi + 1) * 8], u[gi * 8:(gi + 1) * 8]
        for k in (1, 2, 4):
            m = row8 >= k
            u8, a8 = (jnp.where(m, a8 * pltpu.roll(u8, k, 0) + u8, u8),
                      jnp.where(m, a8 * pltpu.roll(a8, k, 0), a8))
        h8 = u8 + a8 * carry
        carry = h8[7:8, :]
        hs_ref[gi * 8:(gi + 1) * 8, :] = h8
    carry_ref[...] = carry
    y_ref[...] = (hs_ref[...] * _gelu_tanh(ly_ref[...])).astype(y_ref.dtype)

    @pl.when(t == nt - 1)
    def _():
        h1_ref[0] = carry
        c1_ref[0] = new_hist[5:8, :]


def _block_diag(w):
    nb, di, do = w.shape
    eye = jnp.eye(nb, dtype=w.dtype)
    return (eye[:, None, :, None] * w[:, :, None, :]).reshape(nb * di, nb * do)


def _lru_call(p, b, t, pos0, conv_w, conv_b, w_r, b_r, w_i, b_i, a_param, h0, c0, out_dtype):
    tt = min(t, 256)
    nt = t // tt
    row = lambda i, j: i * nt + j
    const2 = lambda i, j: (0, 0)
    vec = pl.BlockSpec((1, LRU_W), const2)
    mat = pl.BlockSpec((LRU_W, LRU_W), const2)
    y, h1, c1 = pl.pallas_call(
        functools.partial(_lru_kernel, tt=tt, pos0=pos0),
        grid=(b, nt),
        in_specs=[pl.BlockSpec((tt, LRU_W), lambda i, j: (row(i, j), P_LX // LRU_W)),
                  pl.BlockSpec((tt, LRU_W), lambda i, j: (row(i, j), P_LY // LRU_W)),
                  pl.BlockSpec((CONV_W, LRU_W), const2), vec, mat, vec, mat, vec, vec,
                  pl.BlockSpec((1, 1, LRU_W), lambda i, j: (i, 0, 0)),
                  pl.BlockSpec((1, CONV_W - 1, LRU_W), lambda i, j: (i, 0, 0))],
        out_specs=[pl.BlockSpec((tt, LRU_W), lambda i, j: (row(i, j), 0)),
                   pl.BlockSpec((1, 1, LRU_W), lambda i, j: (i, 0, 0)),
                   pl.BlockSpec((1, CONV_W - 1, LRU_W), lambda i, j: (i, 0, 0))],
        out_shape=[jax.ShapeDtypeStruct((b * t, LRU_W), out_dtype),
                   jax.ShapeDtypeStruct((b, 1, LRU_W), F32),
                   jax.ShapeDtypeStruct((b, CONV_W - 1, LRU_W), F32)],
        scratch_shapes=[pltpu.VMEM((1, LRU_W), F32),
                        pltpu.VMEM((tt + 8, LRU_W), F32),
                        pltpu.VMEM((tt, LRU_W), F32)],
        compiler_params=_cparams(("parallel", "arbitrary")),
        name="rglru",
    )(p, p, conv_w, conv_b.reshape(1, -1), _block_diag(w_r).astype(BF16), b_r.reshape(1, -1),
      _block_diag(w_i).astype(BF16), b_i.reshape(1, -1), a_param.reshape(1, -1),
      h0.reshape(b, 1, LRU_W), c0)
    return y, h1.reshape(b, LRU_W), c1


MLA_SCALE = (MLA_NOPE + MLA_ROPE) ** -0.5
ROPE_HALF = MLA_ROPE // 2


def _rope_rotate(x, cos, sin):
    lane = _iota(x.shape, 1)
    rot = jnp.where(lane % MLA_ROPE < ROPE_HALF, -pltpu.roll(x, LANE - ROPE_HALF, 1), pltpu.roll(x, ROPE_HALF, 1))
    return x * cos + rot * sin


MLA_QK = MLA_KVL + LANE

def _mla_prep_kernel(qd_ref, kvd_ref, krr_ref, cos_ref, sin_ref, qg_ref, wuq_ref, kvg_ref, wukt_ref,
                     qcat_ref, ckv_ref, krope_ref, kcat_ref):
    qd = qd_ref[...]
    qn = qd * lax.rsqrt(jnp.mean(qd * qd, axis=-1, keepdims=True) + RMS_EPS) * qg_ref[...]
    q = _dot(qn, wuq_ref[...])
    cos, sin = cos_ref[...], sin_ref[...]
    nn = MLA_H * MLA_NOPE
    lane = _iota((q.shape[0], LANE), 1)
    halves = [_rope_rotate(q[:, nn + i * LANE:nn + (i + 1) * LANE], cos, sin) for i in range(2)]
    per_half = LANE // MLA_ROPE
    for h in range(MLA_H):
        qcat_ref[h, :, :MLA_KVL] = _dot(q[:, h * MLA_NOPE:(h + 1) * MLA_NOPE], wukt_ref[h]).astype(qcat_ref.dtype)
        half, sh = halves[h // per_half], (h % per_half) * MLA_ROPE
        piece = pltpu.roll(half, LANE - sh, 1) if sh else half
        qcat_ref[h, :, MLA_KVL:] = jnp.where(lane < MLA_ROPE, piece, 0.0).astype(qcat_ref.dtype)
    kvd = kvd_ref[...]
    ckv = kvd * lax.rsqrt(jnp.mean(kvd * kvd, axis=-1, keepdims=True) + RMS_EPS) * kvg_ref[...]
    ckv_ref[...] = ckv
    kr = jnp.where(lane < MLA_ROPE, _rope_rotate(krr_ref[...], cos, sin), 0.0)
    krope_ref[...] = kr[:, :MLA_ROPE]
    kcat_ref[:, :MLA_KVL] = ckv.astype(BF16)
    kcat_ref[:, MLA_KVL:] = kr.astype(BF16)


def _lane_wide(v, n):
    if n == LANE:
        return v
    return jnp.concatenate([v] * (n // LANE), axis=1) if n % LANE == 0 else v[:, :n]


def _softmax_step(s, vals, m_ref, l_ref, acc_ref):
    m_old = m_ref[...]
    m_new = jnp.maximum(m_old, jnp.max(s, axis=-1, keepdims=True))
    pr = jnp.exp(s - _lane_wide(m_new, s.shape[1]))
    alpha = jnp.exp(m_old - m_new)
    l_ref[...] = alpha * l_ref[...] + jnp.sum(pr, axis=-1, keepdims=True)
    acc_ref[...] = _lane_wide(alpha, acc_ref.shape[1]) * acc_ref[...] + _dot(pr, vals)
    m_ref[...] = m_new


def _softmax_init(m_ref, l_ref, acc_ref):
    m_ref[...] = jnp.full(m_ref.shape, NEG, F32)
    l_ref[...] = jnp.zeros(l_ref.shape, F32)
    acc_ref[...] = jnp.zeros(acc_ref.shape, F32)


def _mla_attn_prompt_kernel(q_ref, k_ref, wuv_ref, y_ref, m_ref, l_ref, acc_ref, *, tq):
    i, j = pl.program_id(1), pl.program_id(2)
    rows = MLA_H * tq

    @pl.when(j == 0)
    def _():
        _softmax_init(m_ref, l_ref, acc_ref)

    def step(diagonal):
        k = k_ref[...]
        s = _dot_nt(q_ref[...].reshape(rows, MLA_QK), k) * MLA_SCALE
        if diagonal:
            s = jnp.where(_iota((rows, tq), 1) <= _iota((rows, tq), 0) % tq, s, NEG)
        _softmax_step(s, k[:, :MLA_KVL], m_ref, l_ref, acc_ref)

    @pl.when(j < i)
    def _():
        step(False)

    @pl.when(j == i)
    def _():
        step(True)
        o = acc_ref[...] / _lane_wide(l_ref[...], MLA_KVL)
        for h in range(MLA_H):
            y_ref[:, h * MLA_V:(h + 1) * MLA_V] = _dot(o[h * tq:(h + 1) * tq], wuv_ref[h]).astype(y_ref.dtype)


def _mla_attn_sample_kernel(pt_ref, cache_c_ref, cache_rt_ref, q_ref, ckv_ref, krope_ref, wuv_ref, y_ref,
                            kc_buf, krt_buf, sem, *, layer, n_pages, page, tq):
    b = pl.program_id(0)
    nb = pl.num_programs(0)
    rows = MLA_H * tq

    def page_copies(seq, slot, pg):
        pid = pt_ref[seq, pg]
        return (pltpu.make_async_copy(cache_c_ref.at[layer, pid], kc_buf.at[slot, pl.ds(pg * page, page), :],
                                      sem.at[0, slot]),
                pltpu.make_async_copy(cache_rt_ref.at[layer, pid], krt_buf.at[slot, :, pl.ds(pg * page, page)],
                                      sem.at[1, slot]))

    def start_gather(seq, slot):
        def body(pg, carry):
            for cp in page_copies(seq, slot, pg):
                cp.start()
            return carry
        lax.fori_loop(0, n_pages, body, 0)

    def wait_gather(seq, slot):
        def body(pg, carry):
            for cp in page_copies(seq, slot, pg):
                cp.wait()
            return carry
        lax.fori_loop(0, n_pages, body, 0)

    slot = b % 2

    @pl.when(b == 0)
    def _():
        start_gather(0, 0)

    @pl.when(b + 1 < nb)
    def _():
        start_gather(b + 1, 1 - slot)

    wait_gather(b, slot)

    qcat = q_ref[...].reshape(rows, MLA_QK)
    q, qr = qcat[:, :MLA_KVL], qcat[:, MLA_KVL:MLA_KVL + MLA_ROPE]
    kc = kc_buf[slot].astype(BF16)
    s_old = (_dot_nt(q, kc) + _dot(qr, krt_buf[slot])) * MLA_SCALE
    kc_new, kr_new = ckv_ref[...], krope_ref[...]
    s_new = (_dot_nt(q, kc_new) + _dot_nt(qr, kr_new)) * MLA_SCALE
    s_new = jnp.where(_iota((rows, tq), 1) <= _iota((rows, tq), 0) % tq, s_new, NEG)
    m = jnp.maximum(jnp.max(s_old, axis=-1, keepdims=True), jnp.max(s_new, axis=-1, keepdims=True))
    p_old, p_new = jnp.exp(s_old - m), jnp.exp(s_new - m)
    denom = jnp.sum(p_old, axis=-1, keepdims=True) + jnp.sum(p_new, axis=-1, keepdims=True)
    o = (_dot(p_old, kc) + _dot(p_new, kc_new)) / denom
    for h in range(MLA_H):
        y_ref[:, h * MLA_V:(h + 1) * MLA_V] = _dot(o[h * tq:(h + 1) * tq], wuv_ref[h]).astype(y_ref.dtype)


def _rope_tables(pos):
    inv = ROPE_THETA ** (-jnp.arange(ROPE_HALF, dtype=F32) * 2.0 / MLA_ROPE)
    ang = pos.astype(F32)[:, None] * inv[None, :]
    reps = LANE // ROPE_HALF
    return jnp.tile(jnp.cos(ang), (1, reps)), jnp.tile(jnp.sin(ang), (1, reps))


def _mla_call(p, b, t, pos0, q_norm_g, w_uq, kv_norm_g, w_ukv, cache_c, cache_r, page_table, layer, out_dtype):
    m = b * t
    tm = min(m, 256)
    wq = w_uq.reshape(MLA_QL, MLA_H, MLA_NOPE + MLA_ROPE)
    wq = jnp.concatenate([wq[..., :MLA_NOPE].reshape(MLA_QL, -1), wq[..., MLA_NOPE:].reshape(MLA_QL, -1)], axis=1)
    wkv = w_ukv.reshape(MLA_KVL, MLA_H, MLA_NOPE + MLA_V)
    wukt = jnp.transpose(wkv[..., :MLA_NOPE], (1, 2, 0)).astype(BF16)
    wuv = jnp.transpose(wkv[..., MLA_NOPE:], (1, 0, 2)).astype(BF16)
    cos, sin = _rope_tables(pos0 + jnp.arange(t))
    if t < tm:
        cos, sin = jnp.tile(cos, (tm // t, 1)), jnp.tile(sin, (tm // t, 1))
    ntab = cos.shape[0] // tm
    qdt = BF16 if t >= tm else F32
    const2 = lambda i: (0, 0)
    qcat, ckv, krope, kcat = pl.pallas_call(
        _mla_prep_kernel,
        grid=(m // tm,),
        in_specs=[pl.BlockSpec((tm, MLA_QL), lambda i: (i, P_QD // MLA_QL)),
                  pl.BlockSpec((tm, MLA_KVL), lambda i: (i, P_KVD // MLA_KVL)),
                  pl.BlockSpec((tm, LANE), lambda i: (i, P_KR // LANE)),
                  pl.BlockSpec((tm, LANE), lambda i: (i % ntab, 0)),
                  pl.BlockSpec((tm, LANE), lambda i: (i % ntab, 0)),
                  pl.BlockSpec((1, MLA_QL), const2),
                  pl.BlockSpec((MLA_QL, MLA_H * (MLA_NOPE + MLA_ROPE)), const2),
                  pl.BlockSpec((1, MLA_KVL), const2),
                  pl.BlockSpec((MLA_H, MLA_NOPE, MLA_KVL), lambda i: (0, 0, 0))],
        out_specs=[pl.BlockSpec((MLA_H, tm, MLA_QK), lambda i: (0, i, 0)),
                   pl.BlockSpec((tm, MLA_KVL), lambda i: (i, 0)),
                   pl.BlockSpec((tm, MLA_ROPE), lambda i: (i, 0)),
                   pl.BlockSpec((tm, MLA_QK), lambda i: (i, 0))],
        out_shape=[jax.ShapeDtypeStruct((MLA_H, m, MLA_QK), qdt),
                   jax.ShapeDtypeStruct((m, MLA_KVL), F32),
                   jax.ShapeDtypeStruct((m, MLA_ROPE), F32),
                   jax.ShapeDtypeStruct((m, MLA_QK), BF16)],
        compiler_params=_cparams(("parallel",)),
        name="mla_prep",
    )(p, p, p, cos, sin, q_norm_g.reshape(1, -1), wq.astype(BF16), kv_norm_g.reshape(1, -1), wukt)
    rows = MLA_H * min(t, 256)
    softmax_scratch = [pltpu.VMEM((rows, LANE), F32), pltpu.VMEM((rows, LANE), F32), pltpu.VMEM((rows, MLA_KVL), F32)]

    if cache_c is None:
        tq = min(t, 256)
        nq = t // tq
        y = pl.pallas_call(
            functools.partial(_mla_attn_prompt_kernel, tq=tq),
            grid=(b, nq, nq),
            in_specs=[pl.BlockSpec((MLA_H, tq, MLA_QK), lambda bi, i, j: (0, bi * nq + i, 0)),
                      pl.BlockSpec((tq, MLA_QK), lambda bi, i, j: (bi * nq + jnp.minimum(i, j), 0)),
                      pl.BlockSpec((MLA_H, MLA_KVL, MLA_V), lambda bi, i, j: (0, 0, 0))],
            out_specs=pl.BlockSpec((tq, MLA_H * MLA_V), lambda bi, i, j: (bi * nq + i, 0)),
            out_shape=jax.ShapeDtypeStruct((m, MLA_H * MLA_V), out_dtype),
            scratch_shapes=softmax_scratch,
            compiler_params=_cparams(("parallel", "parallel", "arbitrary")),
            name="mla_attn_prompt",
        )(qcat, kcat, wuv)
        return y, ckv, krope

    n_pages = page_table.shape[1]
    page = cache_c.shape[2]
    cache_rt = jnp.swapaxes(cache_r, 2, 3)
    y = pl.pallas_call(
        functools.partial(_mla_attn_sample_kernel, layer=layer, n_pages=n_pages, page=page, tq=t),
        grid_spec=pltpu.PrefetchScalarGridSpec(
            num_scalar_prefetch=1,
            grid=(b,),
            in_specs=[pl.BlockSpec(memory_space=pl.ANY), pl.BlockSpec(memory_space=pl.ANY),
                      pl.BlockSpec((MLA_H, t, MLA_QK), lambda bi, pt: (0, bi, 0)),
                      pl.BlockSpec((t, MLA_KVL), lambda bi, pt: (bi, 0)),
                      pl.BlockSpec((t, MLA_ROPE), lambda bi, pt: (bi, 0)),
                      pl.BlockSpec((MLA_H, MLA_KVL, MLA_V), lambda bi, pt: (0, 0, 0))],
            out_specs=pl.BlockSpec((t, MLA_H * MLA_V), lambda bi, pt: (bi, 0)),
            scratch_shapes=[pltpu.VMEM((2, n_pages * page, MLA_KVL), F32),
                            pltpu.VMEM((2, MLA_ROPE, n_pages * page), F32),
                            pltpu.SemaphoreType.DMA((2, 2))]),
        out_shape=jax.ShapeDtypeStruct((m, MLA_H * MLA_V), out_dtype),
        compiler_params=_cparams(("arbitrary",)),
        name="mla_attn_sample",
    )(page_table, cache_c, cache_rt, qcat, ckv, krope, wuv)
    return y, ckv, krope


ROUTER_GRP_LANE = MOE_E


def _merge_kernel(gl_ref, ya_ref, yb_ref, yc_ref, yd_ref, wb_ref, wo_ref, x_ref, mod_ref, ng_ref, wr_ref, br_ref,
                  xn_ref, h2_ref, lg_ref):
    bb, tt, d = x_ref.shape
    merged = None
    for n, y_ref in enumerate((ya_ref, yb_ref, yc_ref, yd_ref)):
        term = _sigmoid(gl_ref[:, n * d:(n + 1) * d]) * _dot(y_ref[...], wb_ref[n])
        merged = term if merged is None else merged + term
    out = _dot(merged, wo_ref[...])
    mod = mod_ref[...]
    x = x_ref[...] + mod[:, 2:3, :] * out.reshape(bb, tt, d)
    xn_ref[...] = x
    h = x * lax.rsqrt(jnp.mean(x * x, axis=-1, keepdims=True) + RMS_EPS) * ng_ref[...]
    h = (h * (1.0 + mod[:, 4:5, :]) + mod[:, 3:4, :]).reshape(bb * tt, d)
    h2_ref[...] = h.astype(h2_ref.dtype)
    lg_ref[...] = _dot_hi(h, wr_ref[...]) + br_ref[...]


def _merge_call(p, ys, w_branch, w_out, x, mod, norm2_g, w_router, b_router):
    b, t, d = x.shape
    bb, tt = _row_blocks(b, t, 256)
    nt = t // tt
    tm = bb * tt
    row = lambda i, j: (i * nt + j, 0)
    yspec = pl.BlockSpec((tm, ys[0].shape[1]), row)
    return pl.pallas_call(
        _merge_kernel,
        grid=(b // bb, nt),
        in_specs=[pl.BlockSpec((tm, N_BRANCH * d), row), yspec, yspec, yspec, yspec,
                  pl.BlockSpec(w_branch.shape, lambda i, j: (0, 0, 0)),
                  pl.BlockSpec(w_out.shape, lambda i, j: (0, 0)),
                  pl.BlockSpec((bb, tt, d), lambda i, j: (i, j, 0)),
                  pl.BlockSpec((bb, mod.shape[1], d), lambda i, j: (i, 0, 0)),
                  pl.BlockSpec((1, 1, d), lambda i, j: (0, 0, 0)),
                  pl.BlockSpec((d, LANE), lambda i, j: (0, 0)),
                  pl.BlockSpec((1, LANE), lambda i, j: (0, 0))],
        out_specs=[pl.BlockSpec((bb, tt, d), lambda i, j: (i, j, 0)),
                   pl.BlockSpec((tm, d), row),
                   pl.BlockSpec((tm, LANE), row)],
        out_shape=[jax.ShapeDtypeStruct((b, t, d), F32),
                   jax.ShapeDtypeStruct((b * t, d), BF16),
                   jax.ShapeDtypeStruct((b * t, LANE), F32)],
        compiler_params=_cparams(("parallel", "parallel")),
        name="merge",
    )(p, *ys, w_branch, w_out, x, mod, norm2_g.reshape(1, 1, d), w_router, b_router)


def _route(logits):
    lane = _iota(logits.shape, 1)
    big = jnp.int32(1 << 20)
    grp = jnp.where(jnp.right_shift(lane, 2) == ROUTER_GRP_LANE // MOE_G, logits, NEG)
    gmax = jnp.max(grp, axis=-1, keepdims=True)
    g_top = 1.0 / jnp.sum(jnp.exp(grp - gmax), axis=-1, keepdims=True)
    gidx = jnp.min(jnp.where(grp == gmax, lane, big), axis=-1, keepdims=True) - ROUTER_GRP_LANE
    el = jnp.where(jnp.right_shift(lane, 3) == gidx, logits, NEG)
    m1 = jnp.max(el, axis=-1, keepdims=True)
    i1 = jnp.min(jnp.where(el == m1, lane, big), axis=-1, keepdims=True)
    el2 = jnp.where(lane == i1, NEG, el)
    m2 = jnp.max(el2, axis=-1, keepdims=True)
    i2 = jnp.min(jnp.where(el2 == m2, lane, big), axis=-1, keepdims=True)
    e2 = jnp.exp(m2 - m1)
    w1 = g_top / (1.0 + e2)
    return jnp.where(lane == i1, w1, 0.0) + jnp.where(lane == i2, w1 * e2, 0.0)


def _moe_dense_kernel(h_ref, lg_ref, wg_ref, wu_ref, wd_ref, x_ref, mod_ref, o_ref, comb_ref, acc_ref):
    e = pl.program_id(1)
    ne = pl.num_programs(1)
    bb, tt, d = x_ref.shape

    @pl.when(e == 0)
    def _():
        comb_ref[...] = _route(lg_ref[...])
        acc_ref[...] = jnp.zeros(acc_ref.shape, F32)

    comb = comb_ref[...]
    ce = jnp.sum(jnp.where(_iota(comb.shape, 1) == e, comb, 0.0), axis=-1, keepdims=True)
    h = h_ref[...]
    hid = _silu(_dot(h, wg_ref[...])) * _dot(h, wu_ref[...]) * ce
    acc_ref[...] += _dot(hid, wd_ref[...])

    @pl.when(e == ne - 1)
    def _():
        o_ref[...] = x_ref[...] + mod_ref[...][:, 5:6, :] * acc_ref[...].reshape(bb, tt, d)


def _moe_call(h2, logits, w_gate, w_up, w_down, x, mod):
    b, t, d = x.shape
    bb, tt = _row_blocks(b, t, 512)
    nt = t // tt
    tm = bb * tt
    ne, _, hid = w_gate.shape
    row = lambda i, e: (i, 0)
    xmap = lambda i, e: (i // nt, i % nt, 0)
    return pl.pallas_call(
        _moe_dense_kernel,
        grid=(b * t // tm, ne),
        in_specs=[pl.BlockSpec((tm, d), row), pl.BlockSpec((tm, LANE), row),
                  pl.BlockSpec((None, d, hid), lambda i, e: (e, 0, 0)),
                  pl.BlockSpec((None, d, hid), lambda i, e: (e, 0, 0)),
                  pl.BlockSpec((None, hid, d), lambda i, e: (e, 0, 0)),
                  pl.BlockSpec((bb, tt, d), xmap),
                  pl.BlockSpec((bb, mod.shape[1], d), lambda i, e: (i // nt, 0, 0))],
        out_specs=pl.BlockSpec((bb, tt, d), xmap),
        out_shape=jax.ShapeDtypeStruct((b, t, d), F32),
        scratch_shapes=[pltpu.VMEM((tm, LANE), F32), pltpu.VMEM((tm, d), F32)],
        compiler_params=_cparams(("parallel", "arbitrary")),
        name="moe",
    )(h2, logits, w_gate, w_up, w_down, x, mod)


def _layer(x, mod, pos0, lw, state, cache, out_dtype):
    b, t, d = x.shape
    ssm0, ssm_conv0, hg0, lru0, lru_conv0 = state
    h = _prenorm_call(x, lw['norm1_g'], mod, sh_row=0, sc_row=1)
    p = _matmul_call(h, lw['w_in'])
    y_a, ssm1, ssm_conv1 = _ssd_call(p, b, t, lw['ssd_conv_w'], lw['ssd_conv_b'], lw['ssd_dt_bias'], lw['ssd_a_log'],
                                     lw['ssd_d'], lw['ssd_norm_g'], ssm0, ssm_conv0, out_dtype)
    y_b, hg1 = _hgrn_call(p, b, t, lw['hg_lb'], lw['hg_norm_g'], hg0, out_dtype)
    y_c, ckv, krope = _mla_call(p, b, t, pos0, lw['mla_q_norm_g'], lw['mla_w_uq'], lw['mla_kv_norm_g'],
                                lw['mla_w_ukv'], *cache, out_dtype)
    y_d, lru1, lru_conv1 = _lru_call(p, b, t, pos0, lw['lru_conv_w'], lw['lru_conv_b'], lw['lru_w_r'], lw['lru_b_r'],
                                     lw['lru_w_i'], lw['lru_b_i'], lw['lru_a'], lru0, lru_conv0, out_dtype)
    x, h2, logits = _merge_call(p, (y_a, y_b, y_c, y_d), lw['w_branch'], lw['w_out'], x, mod, lw['norm2_g'],
                                lw['w_router'], lw['b_router'])
    x = _moe_call(h2, logits, lw['moe_w_gate'], lw['moe_w_up'], lw['moe_w_down'], x, mod)
    new = (ckv.reshape(b, t, -1), krope.reshape(b, t, -1), ssm1, ssm_conv1, hg1, lru1, lru_conv1)
    return x, new


def kernel(x_prompt, x_sample, c_prompt, c_sample, cache_kv_latent, cache_k_rope, state_ssm, state_ssm_conv,
           state_hgrn, state_lru, state_lru_conv, page_table, norm1_g, norm2_g, w_mod, b_mod, w_in, ssd_conv_w,
           ssd_conv_b, ssd_dt_bias, ssd_a_log, ssd_d, ssd_norm_g, hg_lb_raw, hg_norm_g, mla_q_norm_g, mla_w_uq,
           mla_kv_norm_g, mla_w_ukv, lru_conv_w, lru_conv_b, lru_w_r, lru_b_r, lru_w_i, lru_b_i, lru_a, w_branch,
           w_out, moe_w_grp, moe_b_grp, moe_w_rt, moe_b_rt, moe_w_gate, moe_w_up, moe_w_down, final_norm_g):
    bp, tp, d = x_prompt.shape
    bs, ts, _ = x_sample.shape
    depth = w_in.shape[0]
    n_past = page_table.shape[1] * cache_kv_latent.shape[2]
    lb_all = jnp.cumsum(jax.nn.softmax(hg_lb_raw.astype(F32), axis=0), axis=0)
    lb_all = lb_all - lb_all[:1]
    c_all = jnp.concatenate([c_prompt, c_sample], axis=0)
    zeros = lambda *s: jnp.zeros(s, F32)
    yp, ys = x_prompt, x_sample
    p_new, s_new = [], []
    for l in range(depth):
        mod = _mod_call(c_all, w_mod[l].astype(BF16), b_mod[l]).reshape(bp + bs, 6, d)
        pad_r = LANE - MOE_E - MOE_G
        lw = {
            'norm1_g': norm1_g[l], 'norm2_g': norm2_g[l], 'w_in': _pack_w_in(w_in[l]),
            'ssd_conv_w': ssd_conv_w[l], 'ssd_conv_b': ssd_conv_b[l], 'ssd_dt_bias': ssd_dt_bias[l],
            'ssd_a_log': ssd_a_log[l], 'ssd_d': ssd_d[l], 'ssd_norm_g': ssd_norm_g[l],
            'hg_lb': lb_all[l], 'hg_norm_g': hg_norm_g[l],
            'mla_q_norm_g': mla_q_norm_g[l], 'mla_w_uq': mla_w_uq[l],
            'mla_kv_norm_g': mla_kv_norm_g[l], 'mla_w_ukv': mla_w_ukv[l],
            'lru_conv_w': lru_conv_w[l], 'lru_conv_b': lru_conv_b[l], 'lru_w_r': lru_w_r[l], 'lru_b_r': lru_b_r[l],
            'lru_w_i': lru_w_i[l], 'lru_b_i': lru_b_i[l], 'lru_a': lru_a[l],
            'w_branch': w_branch[l].astype(BF16), 'w_out': w_out[l].astype(BF16),
            'w_router': jnp.pad(jnp.concatenate([moe_w_rt[l], moe_w_grp[l]], axis=1), ((0, 0), (0, pad_r))),
            'b_router': jnp.pad(jnp.concatenate([moe_b_rt[l], moe_b_grp[l]]), (0, pad_r)).reshape(1, LANE),
            'moe_w_gate': moe_w_gate[l].astype(BF16), 'moe_w_up': moe_w_up[l].astype(BF16),
            'moe_w_down': moe_w_down[l].astype(BF16),
        }
        p_state = (zeros(bp, SSD_H, SSD_P, SSD_N), zeros(bp, CONV_W - 1, SSD_CONV), zeros(bp, HG_H, HG_K, HG_V),
                   zeros(bp, LRU_W), zeros(bp, CONV_W - 1, LRU_W))
        s_state = (state_ssm[l], state_ssm_conv[l], state_hgrn[l], state_lru[l], state_lru_conv[l])
        yp, pn = _layer(yp, mod[:bp], 0, lw, p_state, (None, None, None, l), BF16)
        ys, sn = _layer(ys, mod[bp:], n_past, lw, s_state, (cache_kv_latent, cache_k_rope, page_table, l), F32)
        p_new.append(pn)
        s_new.append(sn)
    no_mod = zeros(1, 2, d)
    yp = _prenorm_call(yp, final_norm_g, jnp.broadcast_to(no_mod, (bp, 2, d)), 0, 1, F32).reshape(bp, tp, d)
    ys = _prenorm_call(ys, final_norm_g, jnp.broadcast_to(no_mod, (bs, 2, d)), 0, 1, F32).reshape(bs, ts, d)
    stk = lambda news, j: jnp.stack([n[j] for n in news])
    return (yp, ys) + tuple(stk(p_new, j) for j in range(7)) + tuple(stk(s_new, j) for j in range(7))
```

```python
import functools
import math

import jax
import jax.numpy as jnp
from jax import lax
from jax.experimental import pallas as pl
from jax.experimental.pallas import tpu as pltpu

F32 = jnp.float32
BF16 = jnp.bfloat16
HI = lax.Precision.HIGHEST
NEG = -1e30

RMS_EPS = 1e-6
D_MODEL = 1024
CONV_W = 4
SSD_H, SSD_P, SSD_G, SSD_N = 8, 64, 2, 64
SSD_INNER = SSD_H * SSD_P
SSD_CONV = SSD_INNER + 2 * SSD_G * SSD_N
SSD_CHUNK = 64
HG_H, HG_K, HG_V = 4, 128, 128
HG_INNER = HG_H * HG_V
HG_CHUNK = 16
MLA_H, MLA_QL, MLA_KVL, MLA_NOPE, MLA_ROPE, MLA_V = 8, 256, 256, 64, 32, 64
ROPE_THETA = 10000.0
LRU_W, LRU_NB = 512, 8
LRU_C = 8.0
N_BRANCH = 4
MOE_G, MOE_EPG, MOE_E, MOE_HID = 4, 8, 32, 256

V7X_VMEM_LIMIT = 56 * 1024 * 1024
LANE = 128

P_GL, P_Z, P_HQ, P_HF, P_HI, P_HG, P_LX, P_LY = 0, 4096, 4608, 5120, 5632, 6144, 6656, 7168
P_XBC, P_QD, P_KVD, P_DT, P_KR = 7680, 8448, 8704, 8960, 9088
P_TOTAL = 9216


def _cparams(sem):
    return pltpu.CompilerParams(dimension_semantics=sem, vmem_limit_bytes=V7X_VMEM_LIMIT)


def _silu(x):
    return x * (1.0 / (1.0 + jnp.exp(-x)))


def _sigmoid(x):
    return 1.0 / (1.0 + jnp.exp(-x))


def _softplus(x):
    return jnp.maximum(x, 0.0) + jnp.log(1.0 + jnp.exp(-jnp.abs(x)))


def _iota(shape, dim):
    return lax.broadcasted_iota(jnp.int32, shape, dim)


def _dot(a, b):
    return jnp.dot(a.astype(BF16), b.astype(BF16), preferred_element_type=F32)


def _dot_nt(a, b):
    return lax.dot_general(a.astype(BF16), b.astype(BF16), (((1,), (1,)), ((), ())), preferred_element_type=F32)


def _dot_tn(a, b):
    return lax.dot_general(a.astype(BF16), b.astype(BF16), (((0,), (0,)), ((), ())), preferred_element_type=F32)


def _dot_hi(a, b):
    return jnp.dot(a, b, precision=HI, preferred_element_type=F32)


def _dot_nt_hi(a, b):
    return lax.dot_general(a, b, (((1,), (1,)), ((), ())), precision=HI, preferred_element_type=F32)


def _tril(n):
    return (_iota((n, n), 0) >= _iota((n, n), 1)).astype(F32)


def _eye(n):
    return (_iota((n, n), 0) == _iota((n, n), 1)).astype(F32)


def _row_blocks(b, t, target):
    if t >= target:
        return 1, target
    return min(b, target // t), t


def _mod_kernel(c_ref, w_ref, b_ref, o_ref):
    o_ref[...] = _dot(_silu(c_ref[...]), w_ref[...]) + b_ref[...]


def _mod_call(c, w, b):
    m, d = c.shape
    n = w.shape[1]
    tn = 1536
    return pl.pallas_call(
        _mod_kernel,
        grid=(n // tn,),
        in_specs=[pl.BlockSpec((m, d), lambda j: (0, 0)),
                  pl.BlockSpec((d, tn), lambda j: (0, j)),
                  pl.BlockSpec((1, tn), lambda j: (0, j))],
        out_specs=pl.BlockSpec((m, tn), lambda j: (0, j)),
        out_shape=jax.ShapeDtypeStruct((m, n), F32),
        compiler_params=_cparams(("parallel",)),
        name="adaln_mod",
    )(c, w, b.reshape(1, n))


def _prenorm_kernel(x_ref, g_ref, mod_ref, o_ref, *, sh_row, sc_row):
    x = x_ref[...]
    bb, tt, d = x.shape
    y = x * lax.rsqrt(jnp.mean(x * x, axis=-1, keepdims=True) + RMS_EPS) * g_ref[...]
    mod = mod_ref[...]
    y = y * (1.0 + mod[:, sc_row:sc_row + 1, :]) + mod[:, sh_row:sh_row + 1, :]
    o_ref[...] = y.reshape(bb * tt, d).astype(o_ref.dtype)


def _prenorm_call(x, g, mod, sh_row, sc_row, out_dtype=BF16):
    b, t, d = x.shape
    bb, tt = _row_blocks(b, t, 512)
    nt = t // tt
    return pl.pallas_call(
        functools.partial(_prenorm_kernel, sh_row=sh_row, sc_row=sc_row),
        grid=(b // bb, nt),
        in_specs=[pl.BlockSpec((bb, tt, d), lambda i, j: (i, j, 0)),
                  pl.BlockSpec((1, 1, d), lambda i, j: (0, 0, 0)),
                  pl.BlockSpec((bb, mod.shape[1], d), lambda i, j: (i, 0, 0))],
        out_specs=pl.BlockSpec((bb * tt, d), lambda i, j: (i * nt + j, 0)),
        out_shape=jax.ShapeDtypeStruct((b * t, d), out_dtype),
        compiler_params=_cparams(("parallel", "parallel")),
        name="prenorm",
    )(x, g.reshape(1, 1, d), mod)


def _matmul_kernel(x_ref, w_ref, o_ref):
    o_ref[...] = jnp.dot(x_ref[...], w_ref[...], preferred_element_type=F32)


def _matmul_call(x, w, tm=512, tn=1024):
    m, k = x.shape
    n = w.shape[1]
    tm = min(tm, m)
    return pl.pallas_call(
        _matmul_kernel,
        grid=(n // tn, m // tm),
        in_specs=[pl.BlockSpec((tm, k), lambda j, i: (i, 0)),
                  pl.BlockSpec((k, tn), lambda j, i: (0, j))],
        out_specs=pl.BlockSpec((tm, tn), lambda j, i: (i, j)),
        out_shape=jax.ShapeDtypeStruct((m, n), F32),
        compiler_params=_cparams(("parallel", "parallel")),
        name="in_proj",
    )(x, w)


def _pack_w_in(w_in):
    d = w_in.shape[0]
    sizes = (SSD_INNER, SSD_CONV, SSD_H, HG_H * HG_K, HG_H * HG_K, HG_INNER, HG_INNER,
             MLA_QL, MLA_KVL, MLA_ROPE, LRU_W, LRU_W, N_BRANCH * D_MODEL)
    offs = [0]
    for s in sizes:
        offs.append(offs[-1] + s)
    z, xbc, dt, hq, hf, hi, hg, qd, kvd, krr, lx, ly, gl = [w_in[:, offs[i]:offs[i + 1]] for i in range(13)]
    pad = lambda a: jnp.pad(a, ((0, 0), (0, LANE - a.shape[1])))
    return jnp.concatenate([gl, z, hq, hf, hi, hg, lx, ly, xbc, qd, kvd, pad(dt), pad(krr)], axis=1).astype(BF16)


def _causal_conv(xx_ref, x, w_ref, b_ref, tt):
    xx_ref[8:8 + tt, :] = x
    w = w_ref[...]
    y = b_ref[...] + w[3:4, :] * x
    for k in range(1, CONV_W):
        y = y + w[3 - k:4 - k, :] * xx_ref[8 - k:8 - k + tt, :]
    return y


def _ssd_kernel(z_ref, xbc_ref, dt_ref, cw_ref, cb_ref, dtb_ref, alog_ref, dfull_ref, ng_ref, s0_ref, c0_ref,
                y_ref, s1_ref, c1_ref, st_ref, xx_ref, yc_ref, *, tt, cl):
    t = pl.program_id(1)
    nt = pl.num_programs(1)
    hp = SSD_H // SSD_G * SSD_P
    eye = _eye(SSD_N)

    @pl.when(t == 0)
    def _():
        xx_ref[0:8, :] = jnp.zeros((8, SSD_CONV), F32)
        xx_ref[5:8, :] = c0_ref[0]
        for h in range(SSD_H):
            g, r = divmod(h, SSD_H // SSD_G)
            st_ref[g, :, r * SSD_P:(r + 1) * SSD_P] = _dot_nt_hi(eye, s0_ref[0, h])

    conv = _causal_conv(xx_ref, xbc_ref[...], cw_ref, cb_ref, tt)
    new_hist = xx_ref[tt:tt + 8, :]
    xx_ref[0:8, :] = new_hist
    act = _silu(conv)
    xs = act[:, :SSD_INNER]
    bm = act[:, SSD_INNER:SSD_INNER + SSD_G * SSD_N]
    cm = act[:, SSD_INNER + SSD_G * SSD_N:]
    dt = _softplus(dt_ref[...] + dtb_ref[...])
    da = dt * (-jnp.exp(alog_ref[...]))
    expand = (_iota((LANE, SSD_INNER), 1) // SSD_P == _iota((LANE, SSD_INNER), 0)).astype(F32)
    dtf = _dot_hi(dt, expand)
    ti, si = _iota((tt, tt), 0), _iota((tt, tt), 1)
    same_chunk = ti // cl == si // cl
    acs_all = _dot_hi((same_chunk & (ti >= si)).astype(F32), da)
    acs_t = lax.dot_general(da, (same_chunk & (ti <= si)).astype(F32), (((0,), (0,)), ((), ())),
                            precision=HI, preferred_element_type=F32)
    last_all = _dot_hi((si == ti // cl * cl + (cl - 1)).astype(F32), acs_all)
    eacs_all = jnp.exp(_dot_hi(acs_all, expand))
    dend_all = jnp.exp(_dot_hi(last_all - acs_all, expand))
    xdt_all = xs * dtf
    xsc_all = xdt_all * dend_all
    tri = _iota((cl, cl), 0) >= _iota((cl, cl), 1)

    for c in range(tt // cl):
        rows = slice(c * cl, (c + 1) * cl)
        eacs = eacs_all[rows]
        xdt = xdt_all[rows]
        xsc = xsc_all[rows]
        for g in range(SSD_G):
            bg = bm[rows, g * SSD_N:(g + 1) * SSD_N]
            cg = cm[rows, g * SSD_N:(g + 1) * SSD_N]
            cb = _dot_nt(cg, bg)
            sg = st_ref[g]
            y_inter = _dot(cg, sg) * eacs[:, g * hp:(g + 1) * hp]
            for r in range(SSD_H // SSD_G):
                h = g * (SSD_H // SSD_G) + r
                seg = acs_all[rows, h:h + 1] - acs_t[h:h + 1, rows]
                decay = jnp.exp(jnp.where(tri, seg, NEG))
                y_h = _dot(cb * decay, xdt[:, h * SSD_P:(h + 1) * SSD_P])
                yc_ref[rows, h * SSD_P:(h + 1) * SSD_P] = y_h + y_inter[:, r * SSD_P:(r + 1) * SSD_P]
            cdec = eacs[cl - 1:cl, g * hp:(g + 1) * hp]
            st_ref[g] = cdec * sg + _dot_tn(bg, xsc[:, g * hp:(g + 1) * hp])

    y = yc_ref[...] + dfull_ref[...] * xs
    yz = y * _silu(z_ref[...])
    out = yz * lax.rsqrt(jnp.mean(yz * yz, axis=-1, keepdims=True) + RMS_EPS) * ng_ref[...]
    y_ref[...] = out.astype(y_ref.dtype)

    @pl.when(t == nt - 1)
    def _():
        c1_ref[0] = new_hist[5:8, :]
        for h in range(SSD_H):
            g, r = divmod(h, SSD_H // SSD_G)
            s1_ref[0, h] = _dot_nt_hi(eye, st_ref[g, :, r * SSD_P:(r + 1) * SSD_P])


def _ssd_call(p, b, t, conv_w, conv_b, dt_bias, a_log, d_skip, norm_g, s0, c0, out_dtype):
    tt = min(t, 256)
    cl = SSD_CHUNK if tt % SSD_CHUNK == 0 else tt
    nt = t // tt
    padl = lambda v: jnp.pad(v, (0, LANE - v.shape[0])).reshape(1, LANE)
    row = lambda i, j: i * nt + j
    const2 = lambda i, j: (0, 0)
    return pl.pallas_call(
        functools.partial(_ssd_kernel, tt=tt, cl=cl),
        grid=(b, nt),
        in_specs=[pl.BlockSpec((tt, SSD_INNER), lambda i, j: (row(i, j), P_Z // SSD_INNER)),
                  pl.BlockSpec((tt, SSD_CONV), lambda i, j: (row(i, j), P_XBC // SSD_CONV)),
                  pl.BlockSpec((tt, LANE), lambda i, j: (row(i, j), P_DT // LANE)),
                  pl.BlockSpec((CONV_W, SSD_CONV), const2),
                  pl.BlockSpec((1, SSD_CONV), const2),
                  pl.BlockSpec((1, LANE), const2),
                  pl.BlockSpec((1, LANE), const2),
                  pl.BlockSpec((1, SSD_INNER), const2),
                  pl.BlockSpec((1, SSD_INNER), const2),
                  pl.BlockSpec((1, SSD_H, SSD_P, SSD_N), lambda i, j: (i, 0, 0, 0)),
                  pl.BlockSpec((1, CONV_W - 1, SSD_CONV), lambda i, j: (i, 0, 0))],
        out_specs=[pl.BlockSpec((tt, SSD_INNER), lambda i, j: (row(i, j), 0)),
                   pl.BlockSpec((1, SSD_H, SSD_P, SSD_N), lambda i, j: (i, 0, 0, 0)),
                   pl.BlockSpec((1, CONV_W - 1, SSD_CONV), lambda i, j: (i, 0, 0))],
        out_shape=[jax.ShapeDtypeStruct((b * t, SSD_INNER), out_dtype),
                   jax.ShapeDtypeStruct((b, SSD_H, SSD_P, SSD_N), F32),
                   jax.ShapeDtypeStruct((b, CONV_W - 1, SSD_CONV), F32)],
        scratch_shapes=[pltpu.VMEM((SSD_G, SSD_N, SSD_H // SSD_G * SSD_P), F32),
                        pltpu.VMEM((tt + 8, SSD_CONV), F32),
                        pltpu.VMEM((tt, SSD_INNER), F32)],
        compiler_params=_cparams(("parallel", "arbitrary")),
        name="ssd",
    )(p, p, p, conv_w, conv_b.reshape(1, -1), padl(dt_bias), padl(a_log),
      jnp.repeat(d_skip, SSD_P).reshape(1, -1), norm_g.reshape(1, -1), s0, c0)


def _hgrn_kernel(q_ref, f_ref, i_ref, g_ref, loglb_ref, log1m_ref, onem_ref, ng_ref, s0_ref,
                 y_ref, s1_ref, st_ref, o_ref, *, bb, tt, cl):
    t = pl.program_id(1)
    nt = pl.num_programs(1)
    rows_n = bb * tt
    hs = [slice(h * HG_K, (h + 1) * HG_K) for h in range(HG_H)]

    @pl.when(t == 0)
    def _():
        for bi in range(bb):
            for h in range(HG_H):
                st_ref[bi, h] = s0_ref[bi, h].T

    q, hf, v = q_ref[...], f_ref[...], i_ref[...]
    b_ = log1m_ref[...] - _softplus(-hf)
    loglb = loglb_ref[...]
    logf = jnp.maximum(loglb, b_) + jnp.log(1.0 + jnp.exp(-jnp.abs(loglb - b_)))
    kin = onem_ref[...] * _sigmoid(-hf)
    ti, si = _iota((rows_n, rows_n), 0), _iota((rows_n, rows_n), 1)
    bc = _dot_hi(((ti // cl == si // cl) & (ti >= si)).astype(F32), logf)
    bl = _dot_hi((si == ti // cl * cl + (cl - 1)).astype(F32), bc)
    in_chunk = _iota((rows_n, HG_INNER), 0) % cl
    p = q * kin
    o = [jnp.sum(p[:, s], axis=-1, keepdims=True) * v[:, s] for s in hs]
    for d in range(1, cl):
        kd, bd, vd = pltpu.roll(kin, d, 0), pltpu.roll(bc, d, 0), pltpu.roll(v, d, 0)
        p = q * kd * jnp.exp(jnp.where(in_chunk >= d, bc - bd, NEG))
        o = [o[h] + jnp.sum(p[:, s], axis=-1, keepdims=True) * vd[:, s] for h, s in enumerate(hs)]
    for h in range(HG_H):
        o_ref[:, hs[h]] = o[h]
    qe = q * jnp.exp(bc)
    ke = kin * jnp.exp(bl - bc)
    dec = jnp.exp(bl)
    for c in range(rows_n // cl):
        rows = slice(c * cl, (c + 1) * cl)
        bi = c * cl // tt
        for h, s in enumerate(hs):
            st = st_ref[bi, h]
            o_ref[rows, s] += _dot_nt(qe[rows, s], st)
            st_ref[bi, h] = dec[c * cl:c * cl + 1, s] * st + _dot_tn(v[rows, s], ke[rows, s])
    ng = ng_ref[...]
    outs = []
    for s in hs:
        oh = o_ref[:, s]
        outs.append(oh * lax.rsqrt(jnp.mean(oh * oh, axis=-1, keepdims=True) + RMS_EPS) * ng)
    y_ref[...] = (jnp.concatenate(outs, axis=-1) * _silu(g_ref[...])).astype(y_ref.dtype)

    @pl.when(t == nt - 1)
    def _():
        for bi in range(bb):
            for h in range(HG_H):
                s1_ref[bi, h] = st_ref[bi, h].T


def _hgrn_call(p, b, t, lb, norm_g, s0, out_dtype):
    bb, tt = _row_blocks(b, t, 256) if t >= 256 else (min(b, 16), t)
    cl = HG_CHUNK if tt % HG_CHUNK == 0 else tt
    nt = t // tt
    row = lambda i, j: i * nt + j
    const2 = lambda i, j: (0, 0)
    col = lambda off: (lambda i, j: (row(i, j), off // HG_INNER))
    vec = pl.BlockSpec((1, HG_INNER), const2)
    blk = pl.BlockSpec((bb * tt, HG_INNER), lambda i, j: (row(i, j), 0))
    return pl.pallas_call(
        functools.partial(_hgrn_kernel, bb=bb, tt=tt, cl=cl),
        grid=(b // bb, nt),
        in_specs=[pl.BlockSpec((bb * tt, HG_INNER), col(P_HQ)), pl.BlockSpec((bb * tt, HG_INNER), col(P_HF)),
                  pl.BlockSpec((bb * tt, HG_INNER), col(P_HI)), pl.BlockSpec((bb * tt, HG_INNER), col(P_HG)),
                  vec, vec, vec, pl.BlockSpec((1, HG_V), const2),
                  pl.BlockSpec((bb, HG_H, HG_K, HG_V), lambda i, j: (i, 0, 0, 0))],
        out_specs=[blk, pl.BlockSpec((bb, HG_H, HG_K, HG_V), lambda i, j: (i, 0, 0, 0))],
        out_shape=[jax.ShapeDtypeStruct((b * t, HG_INNER), out_dtype),
                   jax.ShapeDtypeStruct((b, HG_H, HG_K, HG_V), F32)],
        scratch_shapes=[pltpu.VMEM((bb, HG_H, HG_V, HG_K), F32), pltpu.VMEM((bb * tt, HG_INNER), F32)],
        compiler_params=_cparams(("parallel", "arbitrary")),
        name="hgrn2",
    )(p, p, p, p, jnp.log(lb).reshape(1, -1), jnp.log1p(-lb).reshape(1, -1), (1.0 - lb).reshape(1, -1),
      norm_g.reshape(1, -1), s0)


def _gelu_tanh(x):
    return 0.5 * x * (1.0 + jnp.tanh(math.sqrt(2.0 / math.pi) * (x + 0.044715 * (x * x * x))))


def _lru_kernel(lx_ref, ly_ref, cw_ref, cb_ref, wr_ref, br_ref, wi_ref, bi_ref, ap_ref, h0_ref, c0_ref,
                y_ref, h1_ref, c1_ref, carry_ref, xx_ref, hs_ref, *, tt, pos0):
    t = pl.program_id(1)
    nt = pl.num_programs(1)

    @pl.when(t == 0)
    def _():
        xx_ref[0:8, :] = jnp.zeros((8, LRU_W), F32)
        xx_ref[5:8, :] = c0_ref[0]
        carry_ref[...] = h0_ref[0]

    xl = _causal_conv(xx_ref, lx_ref[...], cw_ref, cb_ref, tt)
    new_hist = xx_ref[tt:tt + 8, :]
    xx_ref[0:8, :] = new_hist
    r = _sigmoid(_dot(xl, wr_ref[...]) + br_ref[...])
    ig = _sigmoid(_dot(xl, wi_ref[...]) + bi_ref[...])
    log_a = -LRU_C * r * _softplus(-ap_ref[...])
    a = jnp.exp(log_a)
    mult = jnp.sqrt(jnp.tanh(-log_a) * (a * a + 1.0))
    pos = pos0 + t * tt + _iota((tt, LRU_W), 0)
    u = jnp.where(pos == 0, 1.0, mult) * (ig * xl)
    row8 = _iota((8, LRU_W), 0)
    carry = carry_ref[...]
    for gi in range(tt // 8):
        a8, u8 = a[gi * 8:(gi + 1) * 8], u[gi * 8:(gi + 1) * 8]
        for k in (1, 2, 4):
            m = row8 >= k
            u8, a8 = (jnp.where(m, a8 * pltpu.roll(u8, k, 0) + u8, u8),
                      jnp.where(m, a8 * pltpu.roll(a8, k, 0), a8))
        h8 = u8 + a8 * carry
        carry = h8[7:8, :]
        hs_ref[gi * 8:(gi + 1) * 8, :] = h8
    carry_ref[...] = carry
    y_ref[...] = (hs_ref[...] * _gelu_tanh(ly_ref[...])).astype(y_ref.dtype)

    @pl.when(t == nt - 1)
    def _():
        h1_ref[0] = carry
        c1_ref[0] = new_hist[5:8, :]


def _block_diag(w):
    nb, di, do = w.shape
    eye = jnp.eye(nb, dtype=w.dtype)
    return (eye[:, None, :, None] * w[:, :, None, :]).reshape(nb * di, nb * do)


def _lru_call(p, b, t, pos0, conv_w, conv_b, w_r, b_r, w_i, b_i, a_param, h0, c0, out_dtype):
    tt = min(t, 256)
    nt = t // tt
    row = lambda i, j: i * nt + j
    const2 = lambda i, j: (0, 0)
    vec = pl.BlockSpec((1, LRU_W), const2)
    mat = pl.BlockSpec((LRU_W, LRU_W), const2)
    y, h1, c1 = pl.pallas_call(
        functools.partial(_lru_kernel, tt=tt, pos0=pos0),
        grid=(b, nt),
        in_specs=[pl.BlockSpec((tt, LRU_W), lambda i, j: (row(i, j), P_LX // LRU_W)),
                  pl.BlockSpec((tt, LRU_W), lambda i, j: (row(i, j), P_LY // LRU_W)),
                  pl.BlockSpec((CONV_W, LRU_W), const2), vec, mat, vec, mat, vec, vec,
                  pl.BlockSpec((1, 1, LRU_W), lambda i, j: (i, 0, 0)),
                  pl.BlockSpec((1, CONV_W - 1, LRU_W), lambda i, j: (i, 0, 0))],
        out_specs=[pl.BlockSpec((tt, LRU_W), lambda i, j: (row(i, j), 0)),
                   pl.BlockSpec((1, 1, LRU_W), lambda i, j: (i, 0, 0)),
                   pl.BlockSpec((1, CONV_W - 1, LRU_W), lambda i, j: (i, 0, 0))],
        out_shape=[jax.ShapeDtypeStruct((b * t, LRU_W), out_dtype),
                   jax.ShapeDtypeStruct((b, 1, LRU_W), F32),
                   jax.ShapeDtypeStruct((b, CONV_W - 1, LRU_W), F32)],
        scratch_shapes=[pltpu.VMEM((1, LRU_W), F32),
                        pltpu.VMEM((tt + 8, LRU_W), F32),
                        pltpu.VMEM((tt, LRU_W), F32)],
        compiler_params=_cparams(("parallel", "arbitrary")),
        name="rglru",
    )(p, p, conv_w, conv_b.reshape(1, -1), _block_diag(w_r).astype(BF16), b_r.reshape(1, -1),
      _block_diag(w_i).astype(BF16), b_i.reshape(1, -1), a_param.reshape(1, -1),
      h0.reshape(b, 1, LRU_W), c0)
    return y, h1.reshape(b, LRU_W), c1


MLA_SCALE = (MLA_NOPE + MLA_ROPE) ** -0.5
ROPE_HALF = MLA_ROPE // 2


def _rope_rotate(x, cos, sin):
    lane = _iota(x.shape, 1)
    rot = jnp.where(lane % MLA_ROPE < ROPE_HALF, -pltpu.roll(x, LANE - ROPE_HALF, 1), pltpu.roll(x, ROPE_HALF, 1))
    return x * cos + rot * sin


MLA_QK = MLA_KVL + LANE

def _mla_prep_kernel(qd_ref, kvd_ref, krr_ref, cos_ref, sin_ref, qg_ref, wuq_ref, kvg_ref, wukt_ref,
                     qcat_ref, ckv_ref, krope_ref, kcat_ref):
    qd = qd_ref[...]
    qn = qd * lax.rsqrt(jnp.mean(qd * qd, axis=-1, keepdims=True) + RMS_EPS) * qg_ref[...]
    q = _dot(qn, wuq_ref[...])
    cos, sin = cos_ref[...], sin_ref[...]
    nn = MLA_H * MLA_NOPE
    lane = _iota((q.shape[0], LANE), 1)
    halves = [_rope_rotate(q[:, nn + i * LANE:nn + (i + 1) * LANE], cos, sin) for i in range(2)]
    per_half = LANE // MLA_ROPE
    for h in range(MLA_H):
        qcat_ref[h, :, :MLA_KVL] = _dot(q[:, h * MLA_NOPE:(h + 1) * MLA_NOPE], wukt_ref[h]).astype(qcat_ref.dtype)
        half, sh = halves[h // per_half], (h % per_half) * MLA_ROPE
        piece = pltpu.roll(half, LANE - sh, 1) if sh else half
        qcat_ref[h, :, MLA_KVL:] = jnp.where(lane < MLA_ROPE, piece, 0.0).astype(qcat_ref.dtype)
    kvd = kvd_ref[...]
    ckv = kvd * lax.rsqrt(jnp.mean(kvd * kvd, axis=-1, keepdims=True) + RMS_EPS) * kvg_ref[...]
    ckv_ref[...] = ckv
    kr = jnp.where(lane < MLA_ROPE, _rope_rotate(krr_ref[...], cos, sin), 0.0)
    krope_ref[...] = kr[:, :MLA_ROPE]
    kcat_ref[:, :MLA_KVL] = ckv.astype(BF16)
    kcat_ref[:, MLA_KVL:] = kr.astype(BF16)


def _lane_wide(v, n):
    if n == LANE:
        return v
    return jnp.concatenate([v] * (n // LANE), axis=1) if n % LANE == 0 else v[:, :n]


def _softmax_step(s, vals, m_ref, l_ref, acc_ref):
    m_old = m_ref[...]
    m_new = jnp.maximum(m_old, jnp.max(s, axis=-1, keepdims=True))
    pr = jnp.exp(s - _lane_wide(m_new, s.shape[1]))
    alpha = jnp.exp(m_old - m_new)
    l_ref[...] = alpha * l_ref[...] + jnp.sum(pr, axis=-1, keepdims=True)
    acc_ref[...] = _lane_wide(alpha, acc_ref.shape[1]) * acc_ref[...] + _dot(pr, vals)
    m_ref[...] = m_new


def _softmax_init(m_ref, l_ref, acc_ref):
    m_ref[...] = jnp.full(m_ref.shape, NEG, F32)
    l_ref[...] = jnp.zeros(l_ref.shape, F32)
    acc_ref[...] = jnp.zeros(acc_ref.shape, F32)


def _mla_attn_prompt_kernel(q_ref, k_ref, wuv_ref, y_ref, m_ref, l_ref, acc_ref, *, tq):
    i, j = pl.program_id(1), pl.program_id(2)
    rows = MLA_H * tq

    @pl.when(j == 0)
    def _():
        _softmax_init(m_ref, l_ref, acc_ref)

    def step(diagonal):
        k = k_ref[...]
        s = _dot_nt(q_ref[...].reshape(rows, MLA_QK), k) * MLA_SCALE
        if diagonal:
            s = jnp.where(_iota((rows, tq), 1) <= _iota((rows, tq), 0) % tq, s, NEG)
        _softmax_step(s, k[:, :MLA_KVL], m_ref, l_ref, acc_ref)

    @pl.when(j < i)
    def _():
        step(False)

    @pl.when(j == i)
    def _():
        step(True)
        o = acc_ref[...] / _lane_wide(l_ref[...], MLA_KVL)
        for h in range(MLA_H):
            y_ref[:, h * MLA_V:(h + 1) * MLA_V] = _dot(o[h * tq:(h + 1) * tq], wuv_ref[h]).astype(y_ref.dtype)


def _mla_attn_sample_kernel(pt_ref, cache_c_ref, cache_rt_ref, q_ref, ckv_ref, krope_ref, wuv_ref, y_ref,
                            kc_buf, krt_buf, sem, *, layer, n_pages, page, tq):
    b = pl.program_id(0)
    nb = pl.num_programs(0)
    rows = MLA_H * tq

    def page_copies(seq, slot, pg):
        pid = pt_ref[seq, pg]
        return (pltpu.make_async_copy(cache_c_ref.at[layer, pid], kc_buf.at[slot, pl.ds(pg * page, page), :],
                                      sem.at[0, slot]),
                pltpu.make_async_copy(cache_rt_ref.at[layer, pid], krt_buf.at[slot, :, pl.ds(pg * page, page)],
                                      sem.at[1, slot]))

    def start_gather(seq, slot):
        def body(pg, carry):
            for cp in page_copies(seq, slot, pg):
                cp.start()
            return carry
        lax.fori_loop(0, n_pages, body, 0)

    def wait_gather(seq, slot):
        def body(pg, carry):
            for cp in page_copies(seq, slot, pg):
                cp.wait()
            return carry
        lax.fori_loop(0, n_pages, body, 0)

    slot = b % 2

    @pl.when(b == 0)
    def _():
        start_gather(0, 0)

    @pl.when(b + 1 < nb)
    def _():
        start_gather(b + 1, 1 - slot)

    wait_gather(b, slot)

    qcat = q_ref[...].reshape(rows, MLA_QK)
    q, qr = qcat[:, :MLA_KVL], qcat[:, MLA_KVL:MLA_KVL + MLA_ROPE]
    kc = kc_buf[slot].astype(BF16)
    s_old = (_dot_nt(q, kc) + _dot(qr, krt_buf[slot])) * MLA_SCALE
    kc_new, kr_new = ckv_ref[...], krope_ref[...]
    s_new = (_dot_nt(q, kc_new) + _dot_nt(qr, kr_new)) * MLA_SCALE
    s_new = jnp.where(_iota((rows, tq), 1) <= _iota((rows, tq), 0) % tq, s_new, NEG)
    m = jnp.maximum(jnp.max(s_old, axis=-1, keepdims=True), jnp.max(s_new, axis=-1, keepdims=True))
    p_old, p_new = jnp.exp(s_old - m), jnp.exp(s_new - m)
    denom = jnp.sum(p_old, axis=-1, keepdims=True) + jnp.sum(p_new, axis=-1, keepdims=True)
    o = (_dot(p_old, kc) + _dot(p_new, kc_new)) / denom
    for h in range(MLA_H):
        y_ref[:, h * MLA_V:(h + 1) * MLA_V] = _dot(o[h * tq:(h + 1) * tq], wuv_ref[h]).astype(y_ref.dtype)


def _rope_tables(pos):
    inv = ROPE_THETA ** (-jnp.arange(ROPE_HALF, dtype=F32) * 2.0 / MLA_ROPE)
    ang = pos.astype(F32)[:, None] * inv[None, :]
    reps = LANE // ROPE_HALF
    return jnp.tile(jnp.cos(ang), (1, reps)), jnp.tile(jnp.sin(ang), (1, reps))


def _mla_call(p, b, t, pos0, q_norm_g, w_uq, kv_norm_g, w_ukv, cache_c, cache_r, page_table, layer, out_dtype):
    m = b * t
    tm = min(m, 256)
    wq = w_uq.reshape(MLA_QL, MLA_H, MLA_NOPE + MLA_ROPE)
    wq = jnp.concatenate([wq[..., :MLA_NOPE].reshape(MLA_QL, -1), wq[..., MLA_NOPE:].reshape(MLA_QL, -1)], axis=1)
    wkv = w_ukv.reshape(MLA_KVL, MLA_H, MLA_NOPE + MLA_V)
    wukt = jnp.transpose(wkv[..., :MLA_NOPE], (1, 2, 0)).astype(BF16)
    wuv = jnp.transpose(wkv[..., MLA_NOPE:], (1, 0, 2)).astype(BF16)
    cos, sin = _rope_tables(pos0 + jnp.arange(t))
    if t < tm:
        cos, sin = jnp.tile(cos, (tm // t, 1)), jnp.tile(sin, (tm // t, 1))
    ntab = cos.shape[0] // tm
    qdt = BF16 if t >= tm else F32
    const2 = lambda i: (0, 0)
    qcat, ckv, krope, kcat = pl.pallas_call(
        _mla_prep_kernel,
        grid=(m // tm,),
        in_specs=[pl.BlockSpec((tm, MLA_QL), lambda i: (i, P_QD // MLA_QL)),
                  pl.BlockSpec((tm, MLA_KVL), lambda i: (i, P_KVD // MLA_KVL)),
                  pl.BlockSpec((tm, LANE), lambda i: (i, P_KR // LANE)),
                  pl.BlockSpec((tm, LANE), lambda i: (i % ntab, 0)),
                  pl.BlockSpec((tm, LANE), lambda i: (i % ntab, 0)),
                  pl.BlockSpec((1, MLA_QL), const2),
                  pl.BlockSpec((MLA_QL, MLA_H * (MLA_NOPE + MLA_ROPE)), const2),
                  pl.BlockSpec((1, MLA_KVL), const2),
                  pl.BlockSpec((MLA_H, MLA_NOPE, MLA_KVL), lambda i: (0, 0, 0))],
        out_specs=[pl.BlockSpec((MLA_H, tm, MLA_QK), lambda i: (0, i, 0)),
                   pl.BlockSpec((tm, MLA_KVL), lambda i: (i, 0)),
                   pl.BlockSpec((tm, MLA_ROPE), lambda i: (i, 0)),
                   pl.BlockSpec((tm, MLA_QK), lambda i: (i, 0))],
        out_shape=[jax.ShapeDtypeStruct((MLA_H, m, MLA_QK), qdt),
                   jax.ShapeDtypeStruct((m, MLA_KVL), F32),
                   jax.ShapeDtypeStruct((m, MLA_ROPE), F32),
                   jax.ShapeDtypeStruct((m, MLA_QK), BF16)],
        compiler_params=_cparams(("parallel",)),
        name="mla_prep",
    )(p, p, p, cos, sin, q_norm_g.reshape(1, -1), wq.astype(BF16), kv_norm_g.reshape(1, -1), wukt)
    rows = MLA_H * min(t, 256)
    softmax_scratch = [pltpu.VMEM((rows, LANE), F32), pltpu.VMEM((rows, LANE), F32), pltpu.VMEM((rows, MLA_KVL), F32)]

    if cache_c is None:
        tq = min(t, 256)
        nq = t // tq
        y = pl.pallas_call(
            functools.partial(_mla_attn_prompt_kernel, tq=tq),
            grid=(b, nq, nq),
            in_specs=[pl.BlockSpec((MLA_H, tq, MLA_QK), lambda bi, i, j: (0, bi * nq + i, 0)),
                      pl.BlockSpec((tq, MLA_QK), lambda bi, i, j: (bi * nq + jnp.minimum(i, j), 0)),
                      pl.BlockSpec((MLA_H, MLA_KVL, MLA_V), lambda bi, i, j: (0, 0, 0))],
            out_specs=pl.BlockSpec((tq, MLA_H * MLA_V), lambda bi, i, j: (bi * nq + i, 0)),
            out_shape=jax.ShapeDtypeStruct((m, MLA_H * MLA_V), out_dtype),
            scratch_shapes=softmax_scratch,
            compiler_params=_cparams(("parallel", "parallel", "arbitrary")),
            name="mla_attn_prompt",
        )(qcat, kcat, wuv)
        return y, ckv, krope

    n_pages = page_table.shape[1]
    page = cache_c.shape[2]
    cache_rt = jnp.swapaxes(cache_r, 2, 3)
    y = pl.pallas_call(
        functools.partial(_mla_attn_sample_kernel, layer=layer, n_pages=n_pages, page=page, tq=t),
        grid_spec=pltpu.PrefetchScalarGridSpec(
            num_scalar_prefetch=1,
            grid=(b,),
            in_specs=[pl.BlockSpec(memory_space=pl.ANY), pl.BlockSpec(memory_space=pl.ANY),
                      pl.BlockSpec((MLA_H, t, MLA_QK), lambda bi, pt: (0, bi, 0)),
                      pl.BlockSpec((t, MLA_KVL), lambda bi, pt: (bi, 0)),
                      pl.BlockSpec((t, MLA_ROPE), lambda bi, pt: (bi, 0)),
                      pl.BlockSpec((MLA_H, MLA_KVL, MLA_V), lambda bi, pt: (0, 0, 0))],
            out_specs=pl.BlockSpec((t, MLA_H * MLA_V), lambda bi, pt: (bi, 0)),
            scratch_shapes=[pltpu.VMEM((2, n_pages * page, MLA_KVL), F32),
                            pltpu.VMEM((2, MLA_ROPE, n_pages * page), F32),
                            pltpu.SemaphoreType.DMA((2, 2))]),
        out_shape=jax.ShapeDtypeStruct((m, MLA_H * MLA_V), out_dtype),
        compiler_params=_cparams(("arbitrary",)),
        name="mla_attn_sample",
    )(page_table, cache_c, cache_rt, qcat, ckv, krope, wuv)
    return y, ckv, krope


ROUTER_GRP_LANE = MOE_E


def _merge_kernel(gl_ref, ya_ref, yb_ref, yc_ref, yd_ref, wb_ref, wo_ref, x_ref, mod_ref, ng_ref, wr_ref, br_ref,
                  xn_ref, h2_ref, lg_ref):
    bb, tt, d = x_ref.shape
    merged = None
    for n, y_ref in enumerate((ya_ref, yb_ref, yc_ref, yd_ref)):
        term = _sigmoid(gl_ref[:, n * d:(n + 1) * d]) * _dot(y_ref[...], wb_ref[n])
        merged = term if merged is None else merged + term
    out = _dot(merged, wo_ref[...])
    mod = mod_ref[...]
    x = x_ref[...] + mod[:, 2:3, :] * out.reshape(bb, tt, d)
    xn_ref[...] = x
    h = x * lax.rsqrt(jnp.mean(x * x, axis=-1, keepdims=True) + RMS_EPS) * ng_ref[...]
    h = (h * (1.0 + mod[:, 4:5, :]) + mod[:, 3:4, :]).reshape(bb * tt, d)
    h2_ref[...] = h.astype(h2_ref.dtype)
    lg_ref[...] = _dot_hi(h, wr_ref[...]) + br_ref[...]


def _merge_call(p, ys, w_branch, w_out, x, mod, norm2_g, w_router, b_router):
    b, t, d = x.shape
    bb, tt = _row_blocks(b, t, 256)
    nt = t // tt
    tm = bb * tt
    row = lambda i, j: (i * nt + j, 0)
    yspec = pl.BlockSpec((tm, ys[0].shape[1]), row)
    return pl.pallas_call(
        _merge_kernel,
        grid=(b // bb, nt),
        in_specs=[pl.BlockSpec((tm, N_BRANCH * d), row), yspec, yspec, yspec, yspec,
                  pl.BlockSpec(w_branch.shape, lambda i, j: (0, 0, 0)),
                  pl.BlockSpec(w_out.shape, lambda i, j: (0, 0)),
                  pl.BlockSpec((bb, tt, d), lambda i, j: (i, j, 0)),
                  pl.BlockSpec((bb, mod.shape[1], d), lambda i, j: (i, 0, 0)),
                  pl.BlockSpec((1, 1, d), lambda i, j: (0, 0, 0)),
                  pl.BlockSpec((d, LANE), lambda i, j: (0, 0)),
                  pl.BlockSpec((1, LANE), lambda i, j: (0, 0))],
        out_specs=[pl.BlockSpec((bb, tt, d), lambda i, j: (i, j, 0)),
                   pl.BlockSpec((tm, d), row),
                   pl.BlockSpec((tm, LANE), row)],
        out_shape=[jax.ShapeDtypeStruct((b, t, d), F32),
                   jax.ShapeDtypeStruct((b * t, d), BF16),
                   jax.ShapeDtypeStruct((b * t, LANE), F32)],
        compiler_params=_cparams(("parallel", "parallel")),
        name="merge",
    )(p, *ys, w_branch, w_out, x, mod, norm2_g.reshape(1, 1, d), w_router, b_router)


def _route(logits):
    lane = _iota(logits.shape, 1)
    big = jnp.int32(1 << 20)
    grp = jnp.where(jnp.right_shift(lane, 2) == ROUTER_GRP_LANE // MOE_G, logits, NEG)
    gmax = jnp.max(grp, axis=-1, keepdims=True)
    g_top = 1.0 / jnp.sum(jnp.exp(grp - gmax), axis=-1, keepdims=True)
    gidx = jnp.min(jnp.where(grp == gmax, lane, big), axis=-1, keepdims=True) - ROUTER_GRP_LANE
    el = jnp.where(jnp.right_shift(lane, 3) == gidx, logits, NEG)
    m1 = jnp.max(el, axis=-1, keepdims=True)
    i1 = jnp.min(jnp.where(el == m1, lane, big), axis=-1, keepdims=True)
    el2 = jnp.where(lane == i1, NEG, el)
    m2 = jnp.max(el2, axis=-1, keepdims=True)
    i2 = jnp.min(jnp.where(el2 == m2, lane, big), axis=-1, keepdims=True)
    e2 = jnp.exp(m2 - m1)
    w1 = g_top / (1.0 + e2)
    return jnp.where(lane == i1, w1, 0.0) + jnp.where(lane == i2, w1 * e2, 0.0)


def _moe_dense_kernel(h_ref, lg_ref, wg_ref, wu_ref, wd_ref, x_ref, mod_ref, o_ref, comb_ref, acc_ref):
    e = pl.program_id(1)
    ne = pl.num_programs(1)
    bb, tt, d = x_ref.shape

    @pl.when(e == 0)
    def _():
        comb_ref[...] = _route(lg_ref[...])
        acc_ref[...] = jnp.zeros(acc_ref.shape, F32)

    comb = comb_ref[...]
    ce = jnp.sum(jnp.where(_iota(comb.shape, 1) == e, comb, 0.0), axis=-1, keepdims=True)
    h = h_ref[...]
    hid = _silu(_dot(h, wg_ref[...])) * _dot(h, wu_ref[...]) * ce
    acc_ref[...] += _dot(hid, wd_ref[...])

    @pl.when(e == ne - 1)
    def _():
        o_ref[...] = x_ref[...] + mod_ref[...][:, 5:6, :] * acc_ref[...].reshape(bb, tt, d)


def _moe_call(h2, logits, w_gate, w_up, w_down, x, mod):
    b, t, d = x.shape
    bb, tt = _row_blocks(b, t, 512)
    nt = t // tt
    tm = bb * tt
    ne, _, hid = w_gate.shape
    row = lambda i, e: (i, 0)
    xmap = lambda i, e: (i // nt, i % nt, 0)
    return pl.pallas_call(
        _moe_dense_kernel,
        grid=(b * t // tm, ne),
        in_specs=[pl.BlockSpec((tm, d), row), pl.BlockSpec((tm, LANE), row),
                  pl.BlockSpec((None, d, hid), lambda i, e: (e, 0, 0)),
                  pl.BlockSpec((None, d, hid), lambda i, e: (e, 0, 0)),
                  pl.BlockSpec((None, hid, d), lambda i, e: (e, 0, 0)),
                  pl.BlockSpec((bb, tt, d), xmap),
                  pl.BlockSpec((bb, mod.shape[1], d), lambda i, e: (i // nt, 0, 0))],
        out_specs=pl.BlockSpec((bb, tt, d), xmap),
        out_shape=jax.ShapeDtypeStruct((b, t, d), F32),
        scratch_shapes=[pltpu.VMEM((tm, LANE), F32), pltpu.VMEM((tm, d), F32)],
        compiler_params=_cparams(("parallel", "arbitrary")),
        name="moe",
    )(h2, logits, w_gate, w_up, w_down, x, mod)


MOE_TM = 1024
MOE_RB = 64
MOE_SLOTS = 4096
MOE_SC = 512
META_S1, META_S2, META_W1, META_W2 = 0, 1, 2, 3


def _moe_route_kernel(lg_ref, meta_ref, metat_ref, nblk_ref, boff_ref):
    comb = _route(lg_ref[...])
    tm = comb.shape[0]
    lane = _iota(comb.shape, 1)
    big = jnp.int32(1 << 20)
    chosen = comb > 0.0
    a = jnp.where(chosen, 1.0, 0.0)
    rank = _dot((_iota((tm, tm), 0) > _iota((tm, tm), 1)).astype(F32), a)
    cnt = jnp.sum(a, axis=0, keepdims=True)
    nblk = jnp.floor((cnt + (MOE_RB - 1)) * (1.0 / MOE_RB))
    upper = (_iota((LANE, LANE), 0) < _iota((LANE, LANE), 1)).astype(F32)
    boff = _dot_hi(jnp.broadcast_to(nblk, (8, LANE)), upper)[0:1]
    slot = boff * MOE_RB + rank
    i1 = jnp.min(jnp.where(chosen, lane, big), axis=-1, keepdims=True)
    i2 = jnp.max(jnp.where(chosen, lane, -1), axis=-1, keepdims=True)
    pick = lambda v, i: jnp.sum(jnp.where(lane == i, v, 0.0), axis=-1, keepdims=True)
    two = i2 != i1
    s1, w1 = pick(slot, i1), pick(comb, i1)
    s2, w2 = jnp.where(two, pick(slot, i2), -1.0), jnp.where(two, pick(comb, i2), 0.0)
    meta = (jnp.where(lane == META_S1, s1, 0.0) + jnp.where(lane == META_S2, s2, 0.0)
            + jnp.where(lane == META_W1, w1, 0.0) + jnp.where(lane == META_W2, w2, 0.0))
    meta_ref[...] = meta
    metat_ref[...] = meta.T[0:8, :]
    nblk_ref[0] = nblk.astype(jnp.int32)
    boff_ref[0] = boff.astype(jnp.int32)


def _moe_sorted_kernel(nblk_ref, boff_ref, h_ref, meta_ref, metat_ref, wg_ref, wu_ref, wd_ref, x_ref, mod_ref,
                       o_ref, xs_ref, ys_ref):
    i, e = pl.program_id(0), pl.program_id(1)
    ne = pl.num_programs(1)
    bb, tt, d = x_ref.shape
    tm = bb * tt
    used = (boff_ref[i * LANE + ne - 1] + nblk_ref[i * LANE + ne - 1]) * MOE_RB

    @pl.when(e == 0)
    def _():
        s1 = metat_ref[META_S1:META_S1 + 1, :]
        s2 = metat_ref[META_S2:META_S2 + 1, :]
        h = h_ref[...]
        for c in range(MOE_SLOTS // MOE_SC):
            @pl.when(c * MOE_SC < used)
            def _():
                sl = (c * MOE_SC + _iota((MOE_SC, tm), 0)).astype(F32)
                perm = jnp.where((sl == s1) | (sl == s2), 1.0, 0.0)
                xs_ref[c * MOE_SC:(c + 1) * MOE_SC, :] = _dot(perm, h).astype(BF16)
        ys_ref[...] = jnp.zeros(ys_ref.shape, BF16)

    def expert_rows(first_block, n_rows):
        rows = pl.ds(pl.multiple_of(first_block * MOE_RB, MOE_RB), n_rows)
        xb = xs_ref[rows, :]
        hid = _silu(_dot(xb, wg_ref[...])) * _dot(xb, wu_ref[...])
        ys_ref[rows, :] = _dot(hid, wd_ref[...]).astype(BF16)

    b0, nb = boff_ref[i * LANE + e], nblk_ref[i * LANE + e]

    def pair(k, carry):
        expert_rows(b0 + 2 * k, 2 * MOE_RB)
        return carry

    lax.fori_loop(0, lax.shift_right_logical(nb, 1), pair, 0)

    @pl.when(lax.bitwise_and(nb, 1) == 1)
    def _():
        expert_rows(b0 + nb - 1, MOE_RB)

    @pl.when(e == ne - 1)
    def _():
        meta = meta_ref[...]
        s1, s2 = meta[:, META_S1:META_S1 + 1], meta[:, META_S2:META_S2 + 1]
        w1, w2 = meta[:, META_W1:META_W1 + 1], meta[:, META_W2:META_W2 + 1]
        o_ref[...] = jnp.zeros(o_ref.shape, F32)
        for c in range(MOE_SLOTS // MOE_SC):
            @pl.when(c * MOE_SC < used)
            def _():
                sl = (c * MOE_SC + _iota((tm, MOE_SC), 1)).astype(F32)
                back = jnp.where(sl == s1, w1, 0.0) + jnp.where(sl == s2, w2, 0.0)
                o_ref[...] += _dot(back, ys_ref[c * MOE_SC:(c + 1) * MOE_SC, :]).reshape(bb, tt, d)
        o_ref[...] = x_ref[...] + mod_ref[...][:, 5:6, :] * o_ref[...]


def _moe_sorted_call(h2, logits, w_gate, w_up, w_down, x, mod):
    b, t, d = x.shape
    tm = MOE_TM
    nt = t // tm
    n_tiles = b * nt
    ne, _, hid = w_gate.shape
    meta, metat, nblk, boff = pl.pallas_call(
        _moe_route_kernel,
        grid=(n_tiles,),
        in_specs=[pl.BlockSpec((tm, LANE), lambda i: (i, 0))],
        out_specs=[pl.BlockSpec((tm, LANE), lambda i: (i, 0)),
                   pl.BlockSpec((8, tm), lambda i: (i, 0)),
                   pl.BlockSpec((1, 1, LANE), lambda i: (i, 0, 0)),
                   pl.BlockSpec((1, 1, LANE), lambda i: (i, 0, 0))],
        out_shape=[jax.ShapeDtypeStruct((n_tiles * tm, LANE), F32),
                   jax.ShapeDtypeStruct((n_tiles * 8, tm), F32),
                   jax.ShapeDtypeStruct((n_tiles, 1, LANE), jnp.int32),
                   jax.ShapeDtypeStruct((n_tiles, 1, LANE), jnp.int32)],
        compiler_params=_cparams(("parallel",)),
        name="moe_route",
    )(logits)
    xmap = lambda i, e, nb, bo: (i // nt, i % nt, 0)
    row = lambda i, e, nb, bo: (i, 0)
    wmap = lambda i, e, nb, bo: (e, 0, 0)
    return pl.pallas_call(
        _moe_sorted_kernel,
        grid_spec=pltpu.PrefetchScalarGridSpec(
            num_scalar_prefetch=2,
            grid=(n_tiles, ne),
            in_specs=[pl.BlockSpec((tm, d), row), pl.BlockSpec((tm, LANE), row), pl.BlockSpec((8, tm), row),
                      pl.BlockSpec((None, d, hid), wmap), pl.BlockSpec((None, d, hid), wmap),
                      pl.BlockSpec((None, hid, d), wmap),
                      pl.BlockSpec((1, tm, d), xmap),
                      pl.BlockSpec((1, mod.shape[1], d), lambda i, e, nb, bo: (i // nt, 0, 0))],
            out_specs=pl.BlockSpec((1, tm, d), xmap),
            scratch_shapes=[pltpu.VMEM((MOE_SLOTS, d), BF16), pltpu.VMEM((MOE_SLOTS, d), BF16)]),
        out_shape=jax.ShapeDtypeStruct((b, t, d), F32),
        compiler_params=_cparams(("parallel", "arbitrary")),
        name="moe_sorted",
    )(nblk.reshape(-1), boff.reshape(-1), h2, meta, metat, w_gate, w_up, w_down, x, mod)


def _layer(x, mod, pos0, lw, state, cache, out_dtype):
    b, t, d = x.shape
    ssm0, ssm_conv0, hg0, lru0, lru_conv0 = state
    h = _prenorm_call(x, lw['norm1_g'], mod, sh_row=0, sc_row=1)
    p = _matmul_call(h, lw['w_in'])
    y_a, ssm1, ssm_conv1 = _ssd_call(p, b, t, lw['ssd_conv_w'], lw['ssd_conv_b'], lw['ssd_dt_bias'], lw['ssd_a_log'],
                                     lw['ssd_d'], lw['ssd_norm_g'], ssm0, ssm_conv0, out_dtype)
    y_b, hg1 = _hgrn_call(p, b, t, lw['hg_lb'], lw['hg_norm_g'], hg0, out_dtype)
    y_c, ckv, krope = _mla_call(p, b, t, pos0, lw['mla_q_norm_g'], lw['mla_w_uq'], lw['mla_kv_norm_g'],
                                lw['mla_w_ukv'], *cache, out_dtype)
    y_d, lru1, lru_conv1 = _lru_call(p, b, t, pos0, lw['lru_conv_w'], lw['lru_conv_b'], lw['lru_w_r'], lw['lru_b_r'],
                                     lw['lru_w_i'], lw['lru_b_i'], lw['lru_a'], lru0, lru_conv0, out_dtype)
    x, h2, logits = _merge_call(p, (y_a, y_b, y_c, y_d), lw['w_branch'], lw['w_out'], x, mod, lw['norm2_g'],
                                lw['w_router'], lw['b_router'])
    moe = _moe_sorted_call if t % MOE_TM == 0 else _moe_call
    x = moe(h2, logits, lw['moe_w_gate'], lw['moe_w_up'], lw['moe_w_down'], x, mod)
    new =(ckv.reshape(b, t, -1), krope.reshape(b, t, -1), ssm1, ssm_conv1, hg1, lru1, lru_conv1)
    return x, new


def kernel(x_prompt, x_sample, c_prompt, c_sample, cache_kv_latent, cache_k_rope, state_ssm, state_ssm_conv,
           state_hgrn, state_lru, state_lru_conv, page_table, norm1_g, norm2_g, w_mod, b_mod, w_in, ssd_conv_w,
           ssd_conv_b, ssd_dt_bias, ssd_a_log, ssd_d, ssd_norm_g, hg_lb_raw, hg_norm_g, mla_q_norm_g, mla_w_uq,
           mla_kv_norm_g, mla_w_ukv, lru_conv_w, lru_conv_b, lru_w_r, lru_b_r, lru_w_i, lru_b_i, lru_a, w_branch,
           w_out, moe_w_grp, moe_b_grp, moe_w_rt, moe_b_rt, moe_w_gate, moe_w_up, moe_w_down, final_norm_g):
    bp, tp, d = x_prompt.shape
    bs, ts, _ = x_sample.shape
    depth = w_in.shape[0]
    n_past = page_table.shape[1] * cache_kv_latent.shape[2]
    lb_all = jnp.cumsum(jax.nn.softmax(hg_lb_raw.astype(F32), axis=0), axis=0)
    lb_all = lb_all - lb_all[:1]
    c_all = jnp.concatenate([c_prompt, c_sample], axis=0)
    zeros = lambda *s: jnp.zeros(s, F32)
    yp, ys = x_prompt, x_sample
    p_new, s_new = [], []
    for l in range(depth):
        mod = _mod_call(c_all, w_mod[l].astype(BF16), b_mod[l]).reshape(bp + bs, 6, d)
        pad_r = LANE - MOE_E - MOE_G
        lw = {
            'norm1_g': norm1_g[l], 'norm2_g': norm2_g[l], 'w_in': _pack_w_in(w_in[l]),
            'ssd_conv_w': ssd_conv_w[l], 'ssd_conv_b': ssd_conv_b[l], 'ssd_dt_bias': ssd_dt_bias[l],
            'ssd_a_log': ssd_a_log[l], 'ssd_d': ssd_d[l], 'ssd_norm_g': ssd_norm_g[l],
            'hg_lb': lb_all[l], 'hg_norm_g': hg_norm_g[l],
            'mla_q_norm_g': mla_q_norm_g[l], 'mla_w_uq': mla_w_uq[l],
            'mla_kv_norm_g': mla_kv_norm_g[l], 'mla_w_ukv': mla_w_ukv[l],
            'lru_conv_w': lru_conv_w[l], 'lru_conv_b': lru_conv_b[l], 'lru_w_r': lru_w_r[l], 'lru_b_r': lru_b_r[l],
            'lru_w_i': lru_w_i[l], 'lru_b_i': lru_b_i[l], 'lru_a': lru_a[l],
            'w_branch': w_branch[l].astype(BF16), 'w_out': w_out[l].astype(BF16),
            'w_router': jnp.pad(jnp.concatenate([moe_w_rt[l], moe_w_grp[l]], axis=1), ((0, 0), (0, pad_r))),
            'b_router': jnp.pad(jnp.concatenate([moe_b_rt[l], moe_b_grp[l]]), (0, pad_r)).reshape(1, LANE),
            'moe_w_gate': moe_w_gate[l].astype(BF16), 'moe_w_up': moe_w_up[l].astype(BF16),
            'moe_w_down': moe_w_down[l].astype(BF16),
        }
        p_state = (zeros(bp, SSD_H, SSD_P, SSD_N), zeros(bp, CONV_W - 1, SSD_CONV), zeros(bp, HG_H, HG_K, HG_V),
                   zeros(bp, LRU_W), zeros(bp, CONV_W - 1, LRU_W))
        s_state = (state_ssm[l], state_ssm_conv[l], state_hgrn[l], state_lru[l], state_lru_conv[l])
        yp, pn = _layer(yp, mod[:bp], 0, lw, p_state, (None, None, None, l), BF16)
        ys, sn = _layer(ys, mod[bp:], n_past, lw, s_state, (cache_kv_latent, cache_k_rope, page_table, l), F32)
        p_new.append(pn)
        s_new.append(sn)
    no_mod = zeros(1, 2, d)
    yp = _prenorm_call(yp, final_norm_g, jnp.broadcast_to(no_mod, (bp, 2, d)), 0, 1, F32).reshape(bp, tp, d)
    ys = _prenorm_call(ys, final_norm_g, jnp.broadcast_to(no_mod, (bs, 2, d)), 0, 1, F32).reshape(bs, ts, d)
    stk = lambda news, j: jnp.stack([n[j] for n in news])
    return (yp, ys) + tuple(stk(p_new, j) for j in range(7)) + tuple(stk(s_new, j) for j in range(7))
```

```python
import functools
import math

import jax
import jax.numpy as jnp
from jax import lax
from jax.experimental import pallas as pl
from jax.experimental.pallas import tpu as pltpu

F32 = jnp.float32
BF16 = jnp.bfloat16
HI = lax.Precision.HIGHEST
NEG = -1e30

RMS_EPS = 1e-6
D_MODEL = 1024
CONV_W = 4
SSD_H, SSD_P, SSD_G, SSD_N = 8, 64, 2, 64
SSD_INNER = SSD_H * SSD_P
SSD_CONV = SSD_INNER + 2 * SSD_G * SSD_N
SSD_CHUNK = 64
HG_H, HG_K, HG_V = 4, 128, 128
HG_INNER = HG_H * HG_V
HG_CHUNK = 16
MLA_H, MLA_QL, MLA_KVL, MLA_NOPE, MLA_ROPE, MLA_V = 8, 256, 256, 64, 32, 64
ROPE_THETA = 10000.0
LRU_W, LRU_NB = 512, 8
LRU_C = 8.0
N_BRANCH = 4
MOE_G, MOE_EPG, MOE_E, MOE_HID = 4, 8, 32, 256

V7X_VMEM_LIMIT = 56 * 1024 * 1024
LANE = 128

P_GL, P_Z, P_HQ, P_HF, P_HI, P_HG, P_LX, P_LY = 0, 4096, 4608, 5120, 5632, 6144, 6656, 7168
P_XBC, P_QD, P_KVD, P_DT, P_KR = 7680, 8448, 8704, 8960, 9088
P_TOTAL = 9216


def _cparams(sem):
    return pltpu.CompilerParams(dimension_semantics=sem, vmem_limit_bytes=V7X_VMEM_LIMIT)


def _silu(x):
    return x * (1.0 / (1.0 + jnp.exp(-x)))


def _sigmoid(x):
    return 1.0 / (1.0 + jnp.exp(-x))


def _softplus(x):
    return jnp.maximum(x, 0.0) + jnp.log(1.0 + jnp.exp(-jnp.abs(x)))


def _iota(shape, dim):
    return lax.broadcasted_iota(jnp.int32, shape, dim)


def _dot(a, b):
    return jnp.dot(a.astype(BF16), b.astype(BF16), preferred_element_type=F32)


def _dot_nt(a, b):
    return lax.dot_general(a.astype(BF16), b.astype(BF16), (((1,), (1,)), ((), ())), preferred_element_type=F32)


def _dot_tn(a, b):
    return lax.dot_general(a.astype(BF16), b.astype(BF16), (((0,), (0,)), ((), ())), preferred_element_type=F32)


def _dot_hi(a, b):
    return jnp.dot(a, b, precision=HI, preferred_element_type=F32)


def _dot_nt_hi(a, b):
    return lax.dot_general(a, b, (((1,), (1,)), ((), ())), precision=HI, preferred_element_type=F32)


def _tril(n):
    return (_iota((n, n), 0) >= _iota((n, n), 1)).astype(F32)


def _eye(n):
    return (_iota((n, n), 0) == _iota((n, n), 1)).astype(F32)


def _row_blocks(b, t, target):
    if t >= target:
        return 1, target
    return min(b, target // t), t


def _mod_kernel(c_ref, w_ref, b_ref, o_ref):
    o_ref[...] = _dot(_silu(c_ref[...]), w_ref[...]) + b_ref[...]


def _mod_call(c, w, b):
    m, d = c.shape
    n = w.shape[1]
    tn = 1536
    return pl.pallas_call(
        _mod_kernel,
        grid=(n // tn,),
        in_specs=[pl.BlockSpec((m, d), lambda j: (0, 0)),
                  pl.BlockSpec((d, tn), lambda j: (0, j)),
                  pl.BlockSpec((1, tn), lambda j: (0, j))],
        out_specs=pl.BlockSpec((m, tn), lambda j: (0, j)),
        out_shape=jax.ShapeDtypeStruct((m, n), F32),
        compiler_params=_cparams(("parallel",)),
        name="adaln_mod",
    )(c, w, b.reshape(1, n))


def _prenorm_kernel(x_ref, g_ref, mod_ref, o_ref, *, sh_row, sc_row):
    x = x_ref[...]
    bb, tt, d = x.shape
    y = x * lax.rsqrt(jnp.mean(x * x, axis=-1, keepdims=True) + RMS_EPS) * g_ref[...]
    mod = mod_ref[...]
    y = y * (1.0 + mod[:, sc_row:sc_row + 1, :]) + mod[:, sh_row:sh_row + 1, :]
    o_ref[...] = y.reshape(bb * tt, d).astype(o_ref.dtype)


def _prenorm_call(x, g, mod, sh_row, sc_row, out_dtype=BF16):
    b, t, d = x.shape
    bb, tt = _row_blocks(b, t, 512)
    nt = t // tt
    return pl.pallas_call(
        functools.partial(_prenorm_kernel, sh_row=sh_row, sc_row=sc_row),
        grid=(b // bb, nt),
        in_specs=[pl.BlockSpec((bb, tt, d), lambda i, j: (i, j, 0)),
                  pl.BlockSpec((1, 1, d), lambda i, j: (0, 0, 0)),
                  pl.BlockSpec((bb, mod.shape[1], d), lambda i, j: (i, 0, 0))],
        out_specs=pl.BlockSpec((bb * tt, d), lambda i, j: (i * nt + j, 0)),
        out_shape=jax.ShapeDtypeStruct((b * t, d), out_dtype),
        compiler_params=_cparams(("parallel", "parallel")),
        name="prenorm",
    )(x, g.reshape(1, 1, d), mod)


def _matmul_kernel(x_ref, w_ref, o_ref):
    o_ref[...] = jnp.dot(x_ref[...], w_ref[...], preferred_element_type=F32)


def _matmul_call(x, w, tm=1024, tn=1024):
    m, k = x.shape
    n = w.shape[1]
    tm = min(tm, m)
    return pl.pallas_call(
        _matmul_kernel,
        grid=(n // tn, m // tm),
        in_specs=[pl.BlockSpec((tm, k), lambda j, i: (i, 0)),
                  pl.BlockSpec((k, tn), lambda j, i: (0, j))],
        out_specs=pl.BlockSpec((tm, tn), lambda j, i: (i, j)),
        out_shape=jax.ShapeDtypeStruct((m, n), F32),
        compiler_params=_cparams(("parallel", "parallel")),
        name="in_proj",
    )(x, w)


def _pack_w_in(w_in):
    d = w_in.shape[0]
    sizes = (SSD_INNER, SSD_CONV, SSD_H, HG_H * HG_K, HG_H * HG_K, HG_INNER, HG_INNER,
             MLA_QL, MLA_KVL, MLA_ROPE, LRU_W, LRU_W, N_BRANCH * D_MODEL)
    offs = [0]
    for s in sizes:
        offs.append(offs[-1] + s)
    z, xbc, dt, hq, hf, hi, hg, qd, kvd, krr, lx, ly, gl = [w_in[:, offs[i]:offs[i + 1]] for i in range(13)]
    pad = lambda a: jnp.pad(a, ((0, 0), (0, LANE - a.shape[1])))
    return jnp.concatenate([gl, z, hq, hf, hi, hg, lx, ly, xbc, qd, kvd, pad(dt), pad(krr)], axis=1).astype(BF16)


def _causal_conv(xx_ref, x, w_ref, b_ref, tt):
    xx_ref[8:8 + tt, :] = x
    w = w_ref[...]
    y = b_ref[...] + w[3:4, :] * x
    for k in range(1, CONV_W):
        y = y + w[3 - k:4 - k, :] * xx_ref[8 - k:8 - k + tt, :]
    return y


def _causal_conv_short(x, hist, w_ref, b_ref, tt):
    rows_n = x.shape[0]
    tpos = _iota(x.shape, 0) % tt
    w = w_ref[...]
    y = b_ref[...] + w[3:4, :] * x
    for k in range(1, CONV_W):
        src = jnp.where(tpos >= k, pltpu.roll(x, k, 0), pltpu.roll(hist, rows_n - tt + k, 0))
        y = y + w[3 - k:4 - k, :] * src
    return y


def _ssd_kernel(z_ref, xbc_ref, dt_ref, cw_ref, cb_ref, dtb_ref, alog_ref, dfull_ref, ng_ref, s0_ref, c0_ref,
                y_ref, s1_ref, st_ref, xx_ref, yc_ref, *, bb, tt, cl):
    t = pl.program_id(1)
    nt = pl.num_programs(1)
    rows_n = bb * tt
    hp = SSD_H // SSD_G * SSD_P
    heads = [divmod(h, SSD_H // SSD_G) for h in range(SSD_H)]

    @pl.when(t == 0)
    def _():
        for bi in range(bb):
            for h, (g, r) in enumerate(heads):
                st_ref[bi, g, :, r * SSD_P:(r + 1) * SSD_P] = s0_ref[bi, h].T

    if bb == 1:
        @pl.when(t == 0)
        def _():
            xx_ref[0:8, :] = jnp.zeros((8, SSD_CONV), F32)
            xx_ref[5:8, :] = c0_ref[0]

        conv = _causal_conv(xx_ref, xbc_ref[...], cw_ref, cb_ref, tt)
        xx_ref[0:8, :] = xx_ref[tt:tt + 8, :]
    else:
        conv = _causal_conv_short(xbc_ref[...], c0_ref[...], cw_ref, cb_ref, tt)
    act = _silu(conv)
    xs = act[:, :SSD_INNER]
    bm = act[:, SSD_INNER:SSD_INNER + SSD_G * SSD_N]
    cm = act[:, SSD_INNER + SSD_G * SSD_N:]
    dt = _softplus(dt_ref[...] + dtb_ref[...])
    da = dt * (-jnp.exp(alog_ref[...]))
    expand = (_iota((LANE, SSD_INNER), 1) // SSD_P == _iota((LANE, SSD_INNER), 0)).astype(F32)
    dtf = _dot_hi(dt, expand)
    ti, si = _iota((rows_n, rows_n), 0), _iota((rows_n, rows_n), 1)
    same_chunk = ti // cl == si // cl
    acs_all = _dot_hi((same_chunk & (ti >= si)).astype(F32), da)
    acs_t = lax.dot_general(da, (same_chunk & (ti <= si)).astype(F32), (((0,), (0,)), ((), ())),
                            precision=HI, preferred_element_type=F32)
    last_all = _dot_hi((si == ti // cl * cl + (cl - 1)).astype(F32), acs_all)
    eacs_all = jnp.exp(_dot_hi(acs_all, expand))
    dend_all = jnp.exp(_dot_hi(last_all - acs_all, expand))
    xdt_all = xs * dtf
    xsc_all = xdt_all * dend_all
    tri = _iota((cl, cl), 0) >= _iota((cl, cl), 1)

    for c in range(rows_n // cl):
        rows = slice(c * cl, (c + 1) * cl)
        bi = c * cl // tt
        eacs = eacs_all[rows]
        xdt = xdt_all[rows]
        xsc = xsc_all[rows]
        for g in range(SSD_G):
            bg = bm[rows, g * SSD_N:(g + 1) * SSD_N]
            cg = cm[rows, g * SSD_N:(g + 1) * SSD_N]
            cb = _dot_nt(cg, bg)
            sg = st_ref[bi, g]
            y_inter = _dot(cg, sg) * eacs[:, g * hp:(g + 1) * hp]
            for r in range(SSD_H // SSD_G):
                h = g * (SSD_H // SSD_G) + r
                seg = acs_all[rows, h:h + 1] - acs_t[h:h + 1, rows]
                decay = jnp.exp(jnp.where(tri, seg, NEG))
                y_h = _dot(cb * decay, xdt[:, h * SSD_P:(h + 1) * SSD_P])
                yc_ref[rows, h * SSD_P:(h + 1) * SSD_P] = y_h + y_inter[:, r * SSD_P:(r + 1) * SSD_P]
            cdec = eacs[cl - 1:cl, g * hp:(g + 1) * hp]
            st_ref[bi, g] = cdec * sg + _dot_tn(bg, xsc[:, g * hp:(g + 1) * hp])

    y = yc_ref[...] + dfull_ref[...] * xs
    yz = y * _silu(z_ref[...])
    out = yz * lax.rsqrt(jnp.mean(yz * yz, axis=-1, keepdims=True) + RMS_EPS) * ng_ref[...]
    y_ref[...] = out.astype(y_ref.dtype)

    @pl.when(t == nt - 1)
    def _():
        for bi in range(bb):
            for h, (g, r) in enumerate(heads):
                s1_ref[bi, h] = st_ref[bi, g, :, r * SSD_P:(r + 1) * SSD_P].T


def _conv_tail(p, b, t, off, width):
    return p[:, off:off + width].reshape(b, t, width)[:, t - (CONV_W - 1):, :]


def _ssd_call(p, b, t, conv_w, conv_b, dt_bias, a_log, d_skip, norm_g, s0, c0, out_dtype):
    bb, tt = (1, min(t, 256)) if t >= 256 else (min(b, 16), t)
    cl = SSD_CHUNK if tt % SSD_CHUNK == 0 else tt
    nt = t // tt
    rows_n = bb * tt
    padl = lambda v: jnp.pad(v, (0, LANE - v.shape[0])).reshape(1, LANE)
    row = lambda i, j: i * nt + j
    const2 = lambda i, j: (0, 0)
    if bb == 1:
        hist, hist_spec = c0, pl.BlockSpec((1, CONV_W - 1, SSD_CONV), lambda i, j: (i, 0, 0))
    else:
        hist = jnp.pad(c0, ((0, 0), (tt - (CONV_W - 1), 0), (0, 0))).reshape(b * tt, SSD_CONV)
        hist_spec = pl.BlockSpec((rows_n, SSD_CONV), lambda i, j: (i, 0))
    y, s1 = pl.pallas_call(
        functools.partial(_ssd_kernel, bb=bb, tt=tt, cl=cl),
        grid=(b // bb, nt),
        in_specs=[pl.BlockSpec((rows_n, SSD_INNER), lambda i, j: (row(i, j), P_Z // SSD_INNER)),
                  pl.BlockSpec((rows_n, SSD_CONV), lambda i, j: (row(i, j), P_XBC // SSD_CONV)),
                  pl.BlockSpec((rows_n, LANE), lambda i, j: (row(i, j), P_DT // LANE)),
                  pl.BlockSpec((CONV_W, SSD_CONV), const2),
                  pl.BlockSpec((1, SSD_CONV), const2),
                  pl.BlockSpec((1, LANE), const2),
                  pl.BlockSpec((1, LANE), const2),
                  pl.BlockSpec((1, SSD_INNER), const2),
                  pl.BlockSpec((1, SSD_INNER), const2),
                  pl.BlockSpec((bb, SSD_H, SSD_P, SSD_N), lambda i, j: (i, 0, 0, 0)),
                  hist_spec],
        out_specs=[pl.BlockSpec((rows_n, SSD_INNER), lambda i, j: (row(i, j), 0)),
                   pl.BlockSpec((bb, SSD_H, SSD_P, SSD_N), lambda i, j: (i, 0, 0, 0))],
        out_shape=[jax.ShapeDtypeStruct((b * t, SSD_INNER), out_dtype),
                   jax.ShapeDtypeStruct((b, SSD_H, SSD_P, SSD_N), F32)],
        scratch_shapes=[pltpu.VMEM((bb, SSD_G, SSD_N, SSD_H // SSD_G * SSD_P), F32),
                        pltpu.VMEM((tt + 8, SSD_CONV), F32),
                        pltpu.VMEM((rows_n, SSD_INNER), F32)],
        compiler_params=_cparams(("parallel", "arbitrary")),
        name="ssd",
    )(p, p, p, conv_w, conv_b.reshape(1, -1), padl(dt_bias), padl(a_log),
      jnp.repeat(d_skip, SSD_P).reshape(1, -1), norm_g.reshape(1, -1), s0, hist)
    return y, s1, _conv_tail(p, b, t, P_XBC, SSD_CONV)


def _hgrn_kernel(q_ref, f_ref, i_ref, g_ref, loglb_ref, log1m_ref, onem_ref, ng_ref, s0_ref,
                 y_ref, s1_ref, st_ref, o_ref, *, bb, tt, cl):
    t = pl.program_id(1)
    nt = pl.num_programs(1)
    rows_n = bb * tt
    hs = [slice(h * HG_K, (h + 1) * HG_K) for h in range(HG_H)]

    @pl.when(t == 0)
    def _():
        for bi in range(bb):
            for h in range(HG_H):
                st_ref[bi, h] = s0_ref[bi, h].T

    q, hf, v = q_ref[...], f_ref[...], i_ref[...]
    b_ = log1m_ref[...] - _softplus(-hf)
    loglb = loglb_ref[...]
    logf = jnp.maximum(loglb, b_) + jnp.log(1.0 + jnp.exp(-jnp.abs(loglb - b_)))
    kin = onem_ref[...] * _sigmoid(-hf)
    ti, si = _iota((rows_n, rows_n), 0), _iota((rows_n, rows_n), 1)
    bc = _dot_hi(((ti // cl == si // cl) & (ti >= si)).astype(F32), logf)
    bl = _dot_hi((si == ti // cl * cl + (cl - 1)).astype(F32), bc)
    in_chunk = _iota((rows_n, HG_INNER), 0) % cl
    p = q * kin
    o = [jnp.sum(p[:, s], axis=-1, keepdims=True) * v[:, s] for s in hs]
    for d in range(1, cl):
        kd, bd, vd = pltpu.roll(kin, d, 0), pltpu.roll(bc, d, 0), pltpu.roll(v, d, 0)
        p = q * kd * jnp.exp(jnp.where(in_chunk >= d, bc - bd, NEG))
        o = [o[h] + jnp.sum(p[:, s], axis=-1, keepdims=True) * vd[:, s] for h, s in enumerate(hs)]
    for h in range(HG_H):
        o_ref[:, hs[h]] = o[h]
    qe = q * jnp.exp(bc)
    ke = kin * jnp.exp(bl - bc)
    dec = jnp.exp(bl)
    for c in range(rows_n // cl):
        rows = slice(c * cl, (c + 1) * cl)
        bi = c * cl // tt
        for h, s in enumerate(hs):
            st = st_ref[bi, h]
            o_ref[rows, s] += _dot_nt(qe[rows, s], st)
            st_ref[bi, h] = dec[c * cl:c * cl + 1, s] * st + _dot_tn(v[rows, s], ke[rows, s])
    ng = ng_ref[...]
    outs = []
    for s in hs:
        oh = o_ref[:, s]
        outs.append(oh * lax.rsqrt(jnp.mean(oh * oh, axis=-1, keepdims=True) + RMS_EPS) * ng)
    y_ref[...] = (jnp.concatenate(outs, axis=-1) * _silu(g_ref[...])).astype(y_ref.dtype)

    @pl.when(t == nt - 1)
    def _():
        for bi in range(bb):
            for h in range(HG_H):
                s1_ref[bi, h] = st_ref[bi, h].T


def _hgrn_call(p, b, t, lb, norm_g, s0, out_dtype):
    bb, tt = _row_blocks(b, t, 256) if t >= 256 else (min(b, 16), t)
    cl = HG_CHUNK if tt % HG_CHUNK == 0 else tt
    nt = t // tt
    row = lambda i, j: i * nt + j
    const2 = lambda i, j: (0, 0)
    col = lambda off: (lambda i, j: (row(i, j), off // HG_INNER))
    vec = pl.BlockSpec((1, HG_INNER), const2)
    blk = pl.BlockSpec((bb * tt, HG_INNER), lambda i, j: (row(i, j), 0))
    return pl.pallas_call(
        functools.partial(_hgrn_kernel, bb=bb, tt=tt, cl=cl),
        grid=(b // bb, nt),
        in_specs=[pl.BlockSpec((bb * tt, HG_INNER), col(P_HQ)), pl.BlockSpec((bb * tt, HG_INNER), col(P_HF)),
                  pl.BlockSpec((bb * tt, HG_INNER), col(P_HI)), pl.BlockSpec((bb * tt, HG_INNER), col(P_HG)),
                  vec, vec, vec, pl.BlockSpec((1, HG_V), const2),
                  pl.BlockSpec((bb, HG_H, HG_K, HG_V), lambda i, j: (i, 0, 0, 0))],
        out_specs=[blk, pl.BlockSpec((bb, HG_H, HG_K, HG_V), lambda i, j: (i, 0, 0, 0))],
        out_shape=[jax.ShapeDtypeStruct((b * t, HG_INNER), out_dtype),
                   jax.ShapeDtypeStruct((b, HG_H, HG_K, HG_V), F32)],
        scratch_shapes=[pltpu.VMEM((bb, HG_H, HG_V, HG_K), F32), pltpu.VMEM((bb * tt, HG_INNER), F32)],
        compiler_params=_cparams(("parallel", "arbitrary")),
        name="hgrn2",
    )(p, p, p, p, jnp.log(lb).reshape(1, -1), jnp.log1p(-lb).reshape(1, -1), (1.0 - lb).reshape(1, -1),
      norm_g.reshape(1, -1), s0)


def _gelu_tanh(x):
    return 0.5 * x * (1.0 + jnp.tanh(math.sqrt(2.0 / math.pi) * (x + 0.044715 * (x * x * x))))


def _lru_kernel(lx_ref, ly_ref, cw_ref, cb_ref, wr_ref, br_ref, wi_ref, bi_ref, ap_ref, h0_ref, c0_ref,
                y_ref, h1_ref, carry_ref, xx_ref, hs_ref, *, bb, tt, pos0):
    t = pl.program_id(1)
    nt = pl.num_programs(1)
    rows_n = bb * tt
    groups = tt // 8

    if bb == 1:
        @pl.when(t == 0)
        def _():
            xx_ref[0:8, :] = jnp.zeros((8, LRU_W), F32)
            xx_ref[5:8, :] = c0_ref[0]
            carry_ref[...] = h0_ref[0]

        xl = _causal_conv(xx_ref, lx_ref[...], cw_ref, cb_ref, tt)
        xx_ref[0:8, :] = xx_ref[tt:tt + 8, :]
    else:
        xl = _causal_conv_short(lx_ref[...], c0_ref[...], cw_ref, cb_ref, tt)
    r = _sigmoid(_dot(xl, wr_ref[...]) + br_ref[...])
    ig = _sigmoid(_dot(xl, wi_ref[...]) + bi_ref[...])
    log_a = -LRU_C * r * _softplus(-ap_ref[...])
    a = jnp.exp(log_a)
    mult = jnp.sqrt(jnp.tanh(-log_a) * (a * a + 1.0))
    pos = pos0 + t * tt + _iota((rows_n, LRU_W), 0) % tt
    u = jnp.where(pos == 0, 1.0, mult) * (ig * xl)
    row8 = _iota((8, LRU_W), 0)
    carry = carry_ref[...]
    for gi in range(rows_n // 8):
        bi, gj = divmod(gi, groups)
        if bb > 1 and gj == 0:
            carry = h0_ref[bi]
        a8, u8 = a[gi * 8:(gi + 1) * 8], u[gi * 8:(gi + 1) * 8]
        for k in (1, 2, 4):
            m = row8 >= k
            u8, a8 = (jnp.where(m, a8 * pltpu.roll(u8, k, 0) + u8, u8),
                      jnp.where(m, a8 * pltpu.roll(a8, k, 0), a8))
        h8 = u8 + a8 * carry
        carry = h8[7:8, :]
        hs_ref[gi * 8:(gi + 1) * 8, :] = h8
        if bb > 1 and gj == groups - 1:
            h1_ref[bi] = carry
    y_ref[...] = (hs_ref[...] * _gelu_tanh(ly_ref[...])).astype(y_ref.dtype)

    if bb == 1:
        carry_ref[...] = carry

        @pl.when(t == nt - 1)
        def _():
            h1_ref[0] = carry


def _block_diag(w):
    nb, di, do = w.shape
    eye = jnp.eye(nb, dtype=w.dtype)
    return (eye[:, None, :, None] * w[:, :, None, :]).reshape(nb * di, nb * do)


def _lru_call(p, b, t, pos0, conv_w, conv_b, w_r, b_r, w_i, b_i, a_param, h0, c0, out_dtype):
    bb, tt = (1, min(t, 256)) if t >= 256 else (min(b, 16), t)
    nt = t // tt
    rows_n = bb * tt
    row = lambda i, j: i * nt + j
    const2 = lambda i, j: (0, 0)
    vec = pl.BlockSpec((1, LRU_W), const2)
    mat = pl.BlockSpec((LRU_W, LRU_W), const2)
    if bb == 1:
        hist, hist_spec = c0, pl.BlockSpec((1, CONV_W - 1, LRU_W), lambda i, j: (i, 0, 0))
    else:
        hist = jnp.pad(c0, ((0, 0), (tt - (CONV_W - 1), 0), (0, 0))).reshape(b * tt, LRU_W)
        hist_spec = pl.BlockSpec((rows_n, LRU_W), lambda i, j: (i, 0))
    y, h1 = pl.pallas_call(
        functools.partial(_lru_kernel, bb=bb, tt=tt, pos0=pos0),
        grid=(b // bb, nt),
        in_specs=[pl.BlockSpec((rows_n, LRU_W), lambda i, j: (row(i, j), P_LX // LRU_W)),
                  pl.BlockSpec((rows_n, LRU_W), lambda i, j: (row(i, j), P_LY // LRU_W)),
                  pl.BlockSpec((CONV_W, LRU_W), const2), vec, mat, vec, mat, vec, vec,
                  pl.BlockSpec((bb, 1, LRU_W), lambda i, j: (i, 0, 0)),
                  hist_spec],
        out_specs=[pl.BlockSpec((rows_n, LRU_W), lambda i, j: (row(i, j), 0)),
                   pl.BlockSpec((bb, 1, LRU_W), lambda i, j: (i, 0, 0))],
        out_shape=[jax.ShapeDtypeStruct((b * t, LRU_W), out_dtype),
                   jax.ShapeDtypeStruct((b, 1, LRU_W), F32)],
        scratch_shapes=[pltpu.VMEM((1, LRU_W), F32),
                        pltpu.VMEM((tt + 8, LRU_W), F32),
                        pltpu.VMEM((rows_n, LRU_W), F32)],
        compiler_params=_cparams(("parallel", "arbitrary")),
        name="rglru",
    )(p, p, conv_w, conv_b.reshape(1, -1), _block_diag(w_r).astype(BF16), b_r.reshape(1, -1),
      _block_diag(w_i).astype(BF16), b_i.reshape(1, -1), a_param.reshape(1, -1),
      h0.reshape(b, 1, LRU_W), hist)
    return y, h1.reshape(b, LRU_W), _conv_tail(p, b, t, P_LX, LRU_W)


MLA_SCALE = (MLA_NOPE + MLA_ROPE) ** -0.5
ROPE_HALF = MLA_ROPE // 2


def _rope_rotate(x, cos, sin):
    lane = _iota(x.shape, 1)
    rot = jnp.where(lane % MLA_ROPE < ROPE_HALF, -pltpu.roll(x, LANE - ROPE_HALF, 1), pltpu.roll(x, ROPE_HALF, 1))
    return x * cos + rot * sin


MLA_QK = MLA_KVL + LANE

def _mla_prep_kernel(qd_ref, kvd_ref, krr_ref, cos_ref, sin_ref, qg_ref, wuq_ref, kvg_ref, wukt_ref,
                     qcat_ref, ckv_ref, krope_ref, kcat_ref):
    qd = qd_ref[...]
    qn = qd * lax.rsqrt(jnp.mean(qd * qd, axis=-1, keepdims=True) + RMS_EPS) * qg_ref[...]
    q = _dot(qn, wuq_ref[...])
    cos, sin = cos_ref[...], sin_ref[...]
    nn = MLA_H * MLA_NOPE
    lane = _iota((q.shape[0], LANE), 1)
    halves = [_rope_rotate(q[:, nn + i * LANE:nn + (i + 1) * LANE], cos, sin) for i in range(2)]
    per_half = LANE // MLA_ROPE
    for h in range(MLA_H):
        qcat_ref[h, :, :MLA_KVL] = _dot(q[:, h * MLA_NOPE:(h + 1) * MLA_NOPE], wukt_ref[h]).astype(qcat_ref.dtype)
        half, sh = halves[h // per_half], (h % per_half) * MLA_ROPE
        piece = pltpu.roll(half, LANE - sh, 1) if sh else half
        qcat_ref[h, :, MLA_KVL:] = jnp.where(lane < MLA_ROPE, piece, 0.0).astype(qcat_ref.dtype)
    kvd = kvd_ref[...]
    ckv = kvd * lax.rsqrt(jnp.mean(kvd * kvd, axis=-1, keepdims=True) + RMS_EPS) * kvg_ref[...]
    ckv_ref[...] = ckv
    kr = jnp.where(lane < MLA_ROPE, _rope_rotate(krr_ref[...], cos, sin), 0.0)
    krope_ref[...] = kr[:, :MLA_ROPE]
    kcat_ref[:, :MLA_KVL] = ckv.astype(BF16)
    kcat_ref[:, MLA_KVL:] = kr.astype(BF16)


def _lane_wide(v, n):
    if n == LANE:
        return v
    return jnp.concatenate([v] * (n // LANE), axis=1) if n % LANE == 0 else v[:, :n]


def _softmax_step(s, vals, m_ref, l_ref, acc_ref):
    m_old = m_ref[...]
    m_new = jnp.maximum(m_old, jnp.max(s, axis=-1, keepdims=True))
    pr = jnp.exp(s - _lane_wide(m_new, s.shape[1]))
    alpha = jnp.exp(m_old - m_new)
    l_ref[...] = alpha * l_ref[...] + jnp.sum(pr, axis=-1, keepdims=True)
    acc_ref[...] = _lane_wide(alpha, acc_ref.shape[1]) * acc_ref[...] + _dot(pr, vals)
    m_ref[...] = m_new


def _softmax_init(m_ref, l_ref, acc_ref):
    m_ref[...] = jnp.full(m_ref.shape, NEG, F32)
    l_ref[...] = jnp.zeros(l_ref.shape, F32)
    acc_ref[...] = jnp.zeros(acc_ref.shape, F32)


def _mla_attn_prompt_kernel(q_ref, k_ref, wuv_ref, y_ref, m_ref, l_ref, acc_ref, *, tq):
    i, j = pl.program_id(1), pl.program_id(2)
    rows = MLA_H * tq

    @pl.when(j == 0)
    def _():
        _softmax_init(m_ref, l_ref, acc_ref)

    def step(diagonal):
        k = k_ref[...]
        s = _dot_nt(q_ref[...].reshape(rows, MLA_QK), k) * MLA_SCALE
        if diagonal:
            s = jnp.where(_iota((rows, tq), 1) <= _iota((rows, tq), 0) % tq, s, NEG)
        _softmax_step(s, k[:, :MLA_KVL], m_ref, l_ref, acc_ref)

    @pl.when(j < i)
    def _():
        step(False)

    @pl.when(j == i)
    def _():
        step(True)
        o = acc_ref[...] / _lane_wide(l_ref[...], MLA_KVL)
        for h in range(MLA_H):
            y_ref[:, h * MLA_V:(h + 1) * MLA_V] = _dot(o[h * tq:(h + 1) * tq], wuv_ref[h]).astype(y_ref.dtype)


def _mla_attn_sample_kernel(pt_ref, cache_c_ref, cache_rt_ref, q_ref, ckv_ref, krope_ref, wuv_ref, y_ref,
                            kc_buf, krt_buf, sem, *, layer, n_pages, page, tq):
    b = pl.program_id(0)
    nb = pl.num_programs(0)
    rows = MLA_H * tq

    def page_copies(seq, slot, pg):
        pid = pt_ref[seq, pg]
        return (pltpu.make_async_copy(cache_c_ref.at[layer, pid], kc_buf.at[slot, pl.ds(pg * page, page), :],
                                      sem.at[0, slot]),
                pltpu.make_async_copy(cache_rt_ref.at[layer, pid], krt_buf.at[slot, :, pl.ds(pg * page, page)],
                                      sem.at[1, slot]))

    def start_gather(seq, slot):
        def body(pg, carry):
            for cp in page_copies(seq, slot, pg):
                cp.start()
            return carry
        lax.fori_loop(0, n_pages, body, 0)

    def wait_gather(seq, slot):
        def body(pg, carry):
            for cp in page_copies(seq, slot, pg):
                cp.wait()
            return carry
        lax.fori_loop(0, n_pages, body, 0)

    slot = b % 2

    @pl.when(b == 0)
    def _():
        start_gather(0, 0)

    @pl.when(b + 1 < nb)
    def _():
        start_gather(b + 1, 1 - slot)

    wait_gather(b, slot)

    qcat = q_ref[...].reshape(rows, MLA_QK)
    q, qr = qcat[:, :MLA_KVL], qcat[:, MLA_KVL:MLA_KVL + MLA_ROPE]
    kc = kc_buf[slot].astype(BF16)
    s_old = (_dot_nt(q, kc) + _dot(qr, krt_buf[slot])) * MLA_SCALE
    kc_new, kr_new = ckv_ref[...], krope_ref[...]
    s_new = (_dot_nt(q, kc_new) + _dot_nt(qr, kr_new)) * MLA_SCALE
    s_new = jnp.where(_iota((rows, tq), 1) <= _iota((rows, tq), 0) % tq, s_new, NEG)
    m = jnp.maximum(jnp.max(s_old, axis=-1, keepdims=True), jnp.max(s_new, axis=-1, keepdims=True))
    p_old, p_new = jnp.exp(s_old - m), jnp.exp(s_new - m)
    denom = jnp.sum(p_old, axis=-1, keepdims=True) + jnp.sum(p_new, axis=-1, keepdims=True)
    o = (_dot(p_old, kc) + _dot(p_new, kc_new)) / denom
    for h in range(MLA_H):
        y_ref[:, h * MLA_V:(h + 1) * MLA_V] = _dot(o[h * tq:(h + 1) * tq], wuv_ref[h]).astype(y_ref.dtype)


def _rope_tables(pos):
    inv = ROPE_THETA ** (-jnp.arange(ROPE_HALF, dtype=F32) * 2.0 / MLA_ROPE)
    ang = pos.astype(F32)[:, None] * inv[None, :]
    reps = LANE // ROPE_HALF
    return jnp.tile(jnp.cos(ang), (1, reps)), jnp.tile(jnp.sin(ang), (1, reps))


def _mla_call(p, b, t, pos0, q_norm_g, w_uq, kv_norm_g, w_ukv, cache_c, cache_r, page_table, layer, out_dtype):
    m = b * t
    tm = min(m, 256)
    wq = w_uq.reshape(MLA_QL, MLA_H, MLA_NOPE + MLA_ROPE)
    wq = jnp.concatenate([wq[..., :MLA_NOPE].reshape(MLA_QL, -1), wq[..., MLA_NOPE:].reshape(MLA_QL, -1)], axis=1)
    wkv = w_ukv.reshape(MLA_KVL, MLA_H, MLA_NOPE + MLA_V)
    wukt = jnp.transpose(wkv[..., :MLA_NOPE], (1, 2, 0)).astype(BF16)
    wuv = jnp.transpose(wkv[..., MLA_NOPE:], (1, 0, 2)).astype(BF16)
    cos, sin = _rope_tables(pos0 + jnp.arange(t))
    if t < tm:
        cos, sin = jnp.tile(cos, (tm // t, 1)), jnp.tile(sin, (tm // t, 1))
    ntab = cos.shape[0] // tm
    qdt = BF16 if t >= tm else F32
    const2 = lambda i: (0, 0)
    qcat, ckv, krope, kcat = pl.pallas_call(
        _mla_prep_kernel,
        grid=(m // tm,),
        in_specs=[pl.BlockSpec((tm, MLA_QL), lambda i: (i, P_QD // MLA_QL)),
                  pl.BlockSpec((tm, MLA_KVL), lambda i: (i, P_KVD // MLA_KVL)),
                  pl.BlockSpec((tm, LANE), lambda i: (i, P_KR // LANE)),
                  pl.BlockSpec((tm, LANE), lambda i: (i % ntab, 0)),
                  pl.BlockSpec((tm, LANE), lambda i: (i % ntab, 0)),
                  pl.BlockSpec((1, MLA_QL), const2),
                  pl.BlockSpec((MLA_QL, MLA_H * (MLA_NOPE + MLA_ROPE)), const2),
                  pl.BlockSpec((1, MLA_KVL), const2),
                  pl.BlockSpec((MLA_H, MLA_NOPE, MLA_KVL), lambda i: (0, 0, 0))],
        out_specs=[pl.BlockSpec((MLA_H, tm, MLA_QK), lambda i: (0, i, 0)),
                   pl.BlockSpec((tm, MLA_KVL), lambda i: (i, 0)),
                   pl.BlockSpec((tm, MLA_ROPE), lambda i: (i, 0)),
                   pl.BlockSpec((tm, MLA_QK), lambda i: (i, 0))],
        out_shape=[jax.ShapeDtypeStruct((MLA_H, m, MLA_QK), qdt),
                   jax.ShapeDtypeStruct((m, MLA_KVL), F32),
                   jax.ShapeDtypeStruct((m, MLA_ROPE), F32),
                   jax.ShapeDtypeStruct((m, MLA_QK), BF16)],
        compiler_params=_cparams(("parallel",)),
        name="mla_prep",
    )(p, p, p, cos, sin, q_norm_g.reshape(1, -1), wq.astype(BF16), kv_norm_g.reshape(1, -1), wukt)
    rows = MLA_H * min(t, 256)
    softmax_scratch = [pltpu.VMEM((rows, LANE), F32), pltpu.VMEM((rows, LANE), F32), pltpu.VMEM((rows, MLA_KVL), F32)]

    if cache_c is None:
        tq = min(t, 256)
        nq = t // tq
        y = pl.pallas_call(
            functools.partial(_mla_attn_prompt_kernel, tq=tq),
            grid=(b, nq, nq),
            in_specs=[pl.BlockSpec((MLA_H, tq, MLA_QK), lambda bi, i, j: (0, bi * nq + i, 0)),
                      pl.BlockSpec((tq, MLA_QK), lambda bi, i, j: (bi * nq + jnp.minimum(i, j), 0)),
                      pl.BlockSpec((MLA_H, MLA_KVL, MLA_V), lambda bi, i, j: (0, 0, 0))],
            out_specs=pl.BlockSpec((tq, MLA_H * MLA_V), lambda bi, i, j: (bi * nq + i, 0)),
            out_shape=jax.ShapeDtypeStruct((m, MLA_H * MLA_V), out_dtype),
            scratch_shapes=softmax_scratch,
            compiler_params=_cparams(("parallel", "parallel", "arbitrary")),
            name="mla_attn_prompt",
        )(qcat, kcat, wuv)
        return y, ckv, krope

    n_pages = page_table.shape[1]
    page = cache_c.shape[2]
    cache_rt = jnp.swapaxes(cache_r, 2, 3)
    y = pl.pallas_call(
        functools.partial(_mla_attn_sample_kernel, layer=layer, n_pages=n_pages, page=page, tq=t),
        grid_spec=pltpu.PrefetchScalarGridSpec(
            num_scalar_prefetch=1,
            grid=(b,),
            in_specs=[pl.BlockSpec(memory_space=pl.ANY), pl.BlockSpec(memory_space=pl.ANY),
                      pl.BlockSpec((MLA_H, t, MLA_QK), lambda bi, pt: (0, bi, 0)),
                      pl.BlockSpec((t, MLA_KVL), lambda bi, pt: (bi, 0)),
                      pl.BlockSpec((t, MLA_ROPE), lambda bi, pt: (bi, 0)),
                      pl.BlockSpec((MLA_H, MLA_KVL, MLA_V), lambda bi, pt: (0, 0, 0))],
            out_specs=pl.BlockSpec((t, MLA_H * MLA_V), lambda bi, pt: (bi, 0)),
            scratch_shapes=[pltpu.VMEM((2, n_pages * page, MLA_KVL), F32),
                            pltpu.VMEM((2, MLA_ROPE, n_pages * page), F32),
                            pltpu.SemaphoreType.DMA((2, 2))]),
        out_shape=jax.ShapeDtypeStruct((m, MLA_H * MLA_V), out_dtype),
        compiler_params=_cparams(("arbitrary",)),
        name="mla_attn_sample",
    )(page_table, cache_c, cache_rt, qcat, ckv, krope, wuv)
    return y, ckv, krope


ROUTER_GRP_LANE = MOE_E


def _merge_kernel(gl_ref, ya_ref, yb_ref, yc_ref, yd_ref, wb_ref, wo_ref, x_ref, mod_ref, ng_ref, wr_ref, br_ref,
                  xn_ref, h2_ref, lg_ref):
    bb, tt, d = x_ref.shape
    merged = None
    for n, y_ref in enumerate((ya_ref, yb_ref, yc_ref, yd_ref)):
        term = _sigmoid(gl_ref[:, n * d:(n + 1) * d]) * _dot(y_ref[...], wb_ref[n])
        merged = term if merged is None else merged + term
    out = _dot(merged, wo_ref[...])
    mod = mod_ref[...]
    x = x_ref[...] + mod[:, 2:3, :] * out.reshape(bb, tt, d)
    xn_ref[...] = x
    h = x * lax.rsqrt(jnp.mean(x * x, axis=-1, keepdims=True) + RMS_EPS) * ng_ref[...]
    h = (h * (1.0 + mod[:, 4:5, :]) + mod[:, 3:4, :]).reshape(bb * tt, d)
    h2_ref[...] = h.astype(h2_ref.dtype)
    lg_ref[...] = _dot_hi(h, wr_ref[...]) + br_ref[...]


def _merge_call(p, ys, w_branch, w_out, x, mod, norm2_g, w_router, b_router):
    b, t, d = x.shape
    bb, tt = _row_blocks(b, t, 512)
    nt = t // tt
    tm = bb * tt
    row = lambda i, j: (i * nt + j, 0)
    yspec = pl.BlockSpec((tm, ys[0].shape[1]), row)
    return pl.pallas_call(
        _merge_kernel,
        grid=(b // bb, nt),
        in_specs=[pl.BlockSpec((tm, N_BRANCH * d), row), yspec, yspec, yspec, yspec,
                  pl.BlockSpec(w_branch.shape, lambda i, j: (0, 0, 0)),
                  pl.BlockSpec(w_out.shape, lambda i, j: (0, 0)),
                  pl.BlockSpec((bb, tt, d), lambda i, j: (i, j, 0)),
                  pl.BlockSpec((bb, mod.shape[1], d), lambda i, j: (i, 0, 0)),
                  pl.BlockSpec((1, 1, d), lambda i, j: (0, 0, 0)),
                  pl.BlockSpec((d, LANE), lambda i, j: (0, 0)),
                  pl.BlockSpec((1, LANE), lambda i, j: (0, 0))],
        out_specs=[pl.BlockSpec((bb, tt, d), lambda i, j: (i, j, 0)),
                   pl.BlockSpec((tm, d), row),
                   pl.BlockSpec((tm, LANE), row)],
        out_shape=[jax.ShapeDtypeStruct((b, t, d), F32),
                   jax.ShapeDtypeStruct((b * t, d), BF16),
                   jax.ShapeDtypeStruct((b * t, LANE), F32)],
        compiler_params=_cparams(("parallel", "parallel")),
        name="merge",
    )(p, *ys, w_branch, w_out, x, mod, norm2_g.reshape(1, 1, d), w_router, b_router)


def _route(logits):
    lane = _iota(logits.shape, 1)
    big = jnp.int32(1 << 20)
    grp = jnp.where(jnp.right_shift(lane, 2) == ROUTER_GRP_LANE // MOE_G, logits, NEG)
    gmax = jnp.max(grp, axis=-1, keepdims=True)
    g_top = 1.0 / jnp.sum(jnp.exp(grp - gmax), axis=-1, keepdims=True)
    gidx = jnp.min(jnp.where(grp == gmax, lane, big), axis=-1, keepdims=True) - ROUTER_GRP_LANE
    el = jnp.where(jnp.right_shift(lane, 3) == gidx, logits, NEG)
    m1 = jnp.max(el, axis=-1, keepdims=True)
    i1 = jnp.min(jnp.where(el == m1, lane, big), axis=-1, keepdims=True)
    el2 = jnp.where(lane == i1, NEG, el)
    m2 = jnp.max(el2, axis=-1, keepdims=True)
    i2 = jnp.min(jnp.where(el2 == m2, lane, big), axis=-1, keepdims=True)
    e2 = jnp.exp(m2 - m1)
    w1 = g_top / (1.0 + e2)
    return jnp.where(lane == i1, w1, 0.0) + jnp.where(lane == i2, w1 * e2, 0.0)


def _moe_dense_kernel(h_ref, lg_ref, wg_ref, wu_ref, wd_ref, x_ref, mod_ref, o_ref, comb_ref, acc_ref):
    e = pl.program_id(1)
    ne = pl.num_programs(1)
    bb, tt, d = x_ref.shape

    @pl.when(e == 0)
    def _():
        comb_ref[...] = _route(lg_ref[...])
        acc_ref[...] = jnp.zeros(acc_ref.shape, F32)

    comb = comb_ref[...]
    ce = jnp.sum(jnp.where(_iota(comb.shape, 1) == e, comb, 0.0), axis=-1, keepdims=True)
    h = h_ref[...]
    hid = _silu(_dot(h, wg_ref[...])) * _dot(h, wu_ref[...]) * ce
    acc_ref[...] += _dot(hid, wd_ref[...])

    @pl.when(e == ne - 1)
    def _():
        o_ref[...] = x_ref[...] + mod_ref[...][:, 5:6, :] * acc_ref[...].reshape(bb, tt, d)


def _moe_call(h2, logits, w_gate, w_up, w_down, x, mod):
    b, t, d = x.shape
    bb, tt = _row_blocks(b, t, 512)
    nt = t // tt
    tm = bb * tt
    ne, _, hid = w_gate.shape
    row = lambda i, e: (i, 0)
    xmap = lambda i, e: (i // nt, i % nt, 0)
    return pl.pallas_call(
        _moe_dense_kernel,
        grid=(b * t // tm, ne),
        in_specs=[pl.BlockSpec((tm, d), row), pl.BlockSpec((tm, LANE), row),
                  pl.BlockSpec((None, d, hid), lambda i, e: (e, 0, 0)),
                  pl.BlockSpec((None, d, hid), lambda i, e: (e, 0, 0)),
                  pl.BlockSpec((None, hid, d), lambda i, e: (e, 0, 0)),
                  pl.BlockSpec((bb, tt, d), xmap),
                  pl.BlockSpec((bb, mod.shape[1], d), lambda i, e: (i // nt, 0, 0))],
        out_specs=pl.BlockSpec((bb, tt, d), xmap),
        out_shape=jax.ShapeDtypeStruct((b, t, d), F32),
        scratch_shapes=[pltpu.VMEM((tm, LANE), F32), pltpu.VMEM((tm, d), F32)],
        compiler_params=_cparams(("parallel", "arbitrary")),
        name="moe",
    )(h2, logits, w_gate, w_up, w_down, x, mod)


MOE_TM = 1024
MOE_RB = 64
MOE_SLOTS = 4096
MOE_SC = 512
MOE_EPS = 4
META_S1, META_S2, META_W1, META_W2 = 0, 1, 2, 3


def _moe_route_kernel(lg_ref, meta_ref, metat_ref, nblk_ref, boff_ref):
    comb = _route(lg_ref[...])
    tm = comb.shape[0]
    lane = _iota(comb.shape, 1)
    big = jnp.int32(1 << 20)
    chosen = comb > 0.0
    a = jnp.where(chosen, 1.0, 0.0)
    rank = _dot((_iota((tm, tm), 0) > _iota((tm, tm), 1)).astype(F32), a)
    cnt = jnp.sum(a, axis=0, keepdims=True)
    nblk = jnp.floor((cnt + (MOE_RB - 1)) * (1.0 / MOE_RB))
    upper = (_iota((LANE, LANE), 0) < _iota((LANE, LANE), 1)).astype(F32)
    boff = _dot_hi(jnp.broadcast_to(nblk, (8, LANE)), upper)[0:1]
    slot = boff * MOE_RB + rank
    i1 = jnp.min(jnp.where(chosen, lane, big), axis=-1, keepdims=True)
    i2 = jnp.max(jnp.where(chosen, lane, -1), axis=-1, keepdims=True)
    pick = lambda v, i: jnp.sum(jnp.where(lane == i, v, 0.0), axis=-1, keepdims=True)
    two = i2 != i1
    s1, w1 = pick(slot, i1), pick(comb, i1)
    s2, w2 = jnp.where(two, pick(slot, i2), -1.0), jnp.where(two, pick(comb, i2), 0.0)
    meta = (jnp.where(lane == META_S1, s1, 0.0) + jnp.where(lane == META_S2, s2, 0.0)
            + jnp.where(lane == META_W1, w1, 0.0) + jnp.where(lane == META_W2, w2, 0.0))
    meta_ref[...] = meta
    metat_ref[...] = meta.T[0:8, :]
    nblk_ref[0] = nblk.astype(jnp.int32)
    boff_ref[0] = boff.astype(jnp.int32)


def _moe_sorted_kernel(nblk_ref, boff_ref, h_ref, meta_ref, metat_ref, wg_ref, wu_ref, wd_ref, x_ref, mod_ref,
                       o_ref, xs_ref, ys_ref):
    i, eg = pl.program_id(0), pl.program_id(1)
    neg = pl.num_programs(1)
    eps = wg_ref.shape[0]
    bb, tt, d = x_ref.shape
    tm = bb * tt
    last = i * LANE + neg * eps - 1
    used = (boff_ref[last] + nblk_ref[last]) * MOE_RB

    @pl.when(eg == 0)
    def _():
        s1 = metat_ref[META_S1:META_S1 + 1, :]
        s2 = metat_ref[META_S2:META_S2 + 1, :]
        h = h_ref[...]
        for c in range(MOE_SLOTS // MOE_SC):
            @pl.when(c * MOE_SC < used)
            def _():
                sl = (c * MOE_SC + _iota((MOE_SC, tm), 0)).astype(F32)
                perm = jnp.where((sl == s1) | (sl == s2), 1.0, 0.0)
                xs_ref[c * MOE_SC:(c + 1) * MOE_SC, :] = _dot(perm, h).astype(BF16)
        ys_ref[...] = jnp.zeros(ys_ref.shape, BF16)

    def expert_rows(j, first_block, n_rows):
        rows = pl.ds(pl.multiple_of(first_block * MOE_RB, MOE_RB), n_rows)
        xb = xs_ref[rows, :]
        hid = _silu(_dot(xb, wg_ref[j])) * _dot(xb, wu_ref[j])
        ys_ref[rows, :] = _dot(hid, wd_ref[j]).astype(BF16)

    for j in range(eps):
        e = i * LANE + eg * eps + j
        b0, nb = boff_ref[e], nblk_ref[e]

        def pair(k, carry, j=j, b0=b0):
            expert_rows(j, b0 + 2 * k, 2 * MOE_RB)
            return carry

        lax.fori_loop(0, lax.shift_right_logical(nb, 1), pair, 0)

        @pl.when(lax.bitwise_and(nb, 1) == 1)
        def _(j=j, b0=b0, nb=nb):
            expert_rows(j, b0 + nb - 1, MOE_RB)

    @pl.when(eg == neg - 1)
    def _():
        meta = meta_ref[...]
        s1, s2 = meta[:, META_S1:META_S1 + 1], meta[:, META_S2:META_S2 + 1]
        w1, w2 = meta[:, META_W1:META_W1 + 1], meta[:, META_W2:META_W2 + 1]
        o_ref[...] = jnp.zeros(o_ref.shape, F32)
        for c in range(MOE_SLOTS // MOE_SC):
            @pl.when(c * MOE_SC < used)
            def _():
                sl = (c * MOE_SC + _iota((tm, MOE_SC), 1)).astype(F32)
                back = jnp.where(sl == s1, w1, 0.0) + jnp.where(sl == s2, w2, 0.0)
                o_ref[...] += _dot(back, ys_ref[c * MOE_SC:(c + 1) * MOE_SC, :]).reshape(bb, tt, d)
        o_ref[...] = x_ref[...] + mod_ref[...][:, 5:6, :] * o_ref[...]


def _moe_sorted_call(h2, logits, w_gate, w_up, w_down, x, mod):
    b, t, d = x.shape
    tm = MOE_TM
    nt = t // tm
    n_tiles = b * nt
    ne, _, hid = w_gate.shape
    meta, metat, nblk, boff = pl.pallas_call(
        _moe_route_kernel,
        grid=(n_tiles,),
        in_specs=[pl.BlockSpec((tm, LANE), lambda i: (i, 0))],
        out_specs=[pl.BlockSpec((tm, LANE), lambda i: (i, 0)),
                   pl.BlockSpec((8, tm), lambda i: (i, 0)),
                   pl.BlockSpec((1, 1, LANE), lambda i: (i, 0, 0)),
                   pl.BlockSpec((1, 1, LANE), lambda i: (i, 0, 0))],
        out_shape=[jax.ShapeDtypeStruct((n_tiles * tm, LANE), F32),
                   jax.ShapeDtypeStruct((n_tiles * 8, tm), F32),
                   jax.ShapeDtypeStruct((n_tiles, 1, LANE), jnp.int32),
                   jax.ShapeDtypeStruct((n_tiles, 1, LANE), jnp.int32)],
        compiler_params=_cparams(("parallel",)),
        name="moe_route",
    )(logits)
    xmap = lambda i, e, nb, bo: (i // nt, i % nt, 0)
    row = lambda i, e, nb, bo: (i, 0)
    wmap = lambda i, e, nb, bo: (e, 0, 0)
    return pl.pallas_call(
        _moe_sorted_kernel,
        grid_spec=pltpu.PrefetchScalarGridSpec(
            num_scalar_prefetch=2,
            grid=(n_tiles, ne // MOE_EPS),
            in_specs=[pl.BlockSpec((tm, d), row), pl.BlockSpec((tm, LANE), row), pl.BlockSpec((8, tm), row),
                      pl.BlockSpec((MOE_EPS, d, hid), wmap), pl.BlockSpec((MOE_EPS, d, hid), wmap),
                      pl.BlockSpec((MOE_EPS, hid, d), wmap),
                      pl.BlockSpec((1, tm, d), xmap),
                      pl.BlockSpec((1, mod.shape[1], d), lambda i, e, nb, bo: (i // nt, 0, 0))],
            out_specs=pl.BlockSpec((1, tm, d), xmap),
            scratch_shapes=[pltpu.VMEM((MOE_SLOTS, d), BF16), pltpu.VMEM((MOE_SLOTS, d), BF16)]),
        out_shape=jax.ShapeDtypeStruct((b, t, d), F32),
        compiler_params=_cparams(("parallel", "arbitrary")),
        name="moe_sorted",
    )(nblk.reshape(-1), boff.reshape(-1), h2, meta, metat, w_gate, w_up, w_down, x, mod)


def _layer(x, mod, pos0, lw, state, cache, out_dtype):
    b, t, d = x.shape
    ssm0, ssm_conv0, hg0, lru0, lru_conv0 = state
    h = _prenorm_call(x, lw['norm1_g'], mod, sh_row=0, sc_row=1)
    p = _matmul_call(h, lw['w_in'])
    y_a, ssm1, ssm_conv1 = _ssd_call(p, b, t, lw['ssd_conv_w'], lw['ssd_conv_b'], lw['ssd_dt_bias'], lw['ssd_a_log'],
                                     lw['ssd_d'], lw['ssd_norm_g'], ssm0, ssm_conv0, out_dtype)
    y_b, hg1 = _hgrn_call(p, b, t, lw['hg_lb'], lw['hg_norm_g'], hg0, out_dtype)
    y_c, ckv, krope = _mla_call(p, b, t, pos0, lw['mla_q_norm_g'], lw['mla_w_uq'], lw['mla_kv_norm_g'],
                                lw['mla_w_ukv'], *cache, out_dtype)
    y_d, lru1, lru_conv1 = _lru_call(p, b, t, pos0, lw['lru_conv_w'], lw['lru_conv_b'], lw['lru_w_r'], lw['lru_b_r'],
                                     lw['lru_w_i'], lw['lru_b_i'], lw['lru_a'], lru0, lru_conv0, out_dtype)
    x, h2, logits = _merge_call(p, (y_a, y_b, y_c, y_d), lw['w_branch'], lw['w_out'], x, mod, lw['norm2_g'],
                                lw['w_router'], lw['b_router'])
    moe = _moe_sorted_call if t % MOE_TM == 0 else _moe_call
    x = moe(h2, logits, lw['moe_w_gate'], lw['moe_w_up'], lw['moe_w_down'], x, mod)
    new =(ckv.reshape(b, t, -1), krope.reshape(b, t, -1), ssm1, ssm_conv1, hg1, lru1, lru_conv1)
    return x, new


def kernel(x_prompt, x_sample, c_prompt, c_sample, cache_kv_latent, cache_k_rope, state_ssm, state_ssm_conv,
           state_hgrn, state_lru, state_lru_conv, page_table, norm1_g, norm2_g, w_mod, b_mod, w_in, ssd_conv_w,
           ssd_conv_b, ssd_dt_bias, ssd_a_log, ssd_d, ssd_norm_g, hg_lb_raw, hg_norm_g, mla_q_norm_g, mla_w_uq,
           mla_kv_norm_g, mla_w_ukv, lru_conv_w, lru_conv_b, lru_w_r, lru_b_r, lru_w_i, lru_b_i, lru_a, w_branch,
           w_out, moe_w_grp, moe_b_grp, moe_w_rt, moe_b_rt, moe_w_gate, moe_w_up, moe_w_down, final_norm_g):
    bp, tp, d = x_prompt.shape
    bs, ts, _ = x_sample.shape
    depth = w_in.shape[0]
    n_past = page_table.shape[1] * cache_kv_latent.shape[2]
    lb_all = jnp.cumsum(jax.nn.softmax(hg_lb_raw.astype(F32), axis=0), axis=0)
    lb_all = lb_all - lb_all[:1]
    c_all = jnp.concatenate([c_prompt, c_sample], axis=0)
    zeros = lambda *s: jnp.zeros(s, F32)
    yp, ys = x_prompt, x_sample
    p_new, s_new = [], []
    for l in range(depth):
        mod = _mod_call(c_all, w_mod[l].astype(BF16), b_mod[l]).reshape(bp + bs, 6, d)
        pad_r = LANE - MOE_E - MOE_G
        lw = {
            'norm1_g': norm1_g[l], 'norm2_g': norm2_g[l], 'w_in': _pack_w_in(w_in[l]),
            'ssd_conv_w': ssd_conv_w[l], 'ssd_conv_b': ssd_conv_b[l], 'ssd_dt_bias': ssd_dt_bias[l],
            'ssd_a_log': ssd_a_log[l], 'ssd_d': ssd_d[l], 'ssd_norm_g': ssd_norm_g[l],
            'hg_lb': lb_all[l], 'hg_norm_g': hg_norm_g[l],
            'mla_q_norm_g': mla_q_norm_g[l], 'mla_w_uq': mla_w_uq[l],
            'mla_kv_norm_g': mla_kv_norm_g[l], 'mla_w_ukv': mla_w_ukv[l],
            'lru_conv_w': lru_conv_w[l], 'lru_conv_b': lru_conv_b[l], 'lru_w_r': lru_w_r[l], 'lru_b_r': lru_b_r[l],
            'lru_w_i': lru_w_i[l], 'lru_b_i': lru_b_i[l], 'lru_a': lru_a[l],
            'w_branch': w_branch[l].astype(BF16), 'w_out': w_out[l].astype(BF16),
            'w_router': jnp.pad(jnp.concatenate([moe_w_rt[l], moe_w_grp[l]], axis=1), ((0, 0), (0, pad_r))),
            'b_router': jnp.pad(jnp.concatenate([moe_b_rt[l], moe_b_grp[l]]), (0, pad_r)).reshape(1, LANE),
            'moe_w_gate': moe_w_gate[l].astype(BF16), 'moe_w_up': moe_w_up[l].astype(BF16),
            'moe_w_down': moe_w_down[l].astype(BF16),
        }
        p_state = (zeros(bp, SSD_H, SSD_P, SSD_N), zeros(bp, CONV_W - 1, SSD_CONV), zeros(bp, HG_H, HG_K, HG_V),
                   zeros(bp, LRU_W), zeros(bp, CONV_W - 1, LRU_W))
        s_state = (state_ssm[l], state_ssm_conv[l], state_hgrn[l], state_lru[l], state_lru_conv[l])
        yp, pn = _layer(yp, mod[:bp], 0, lw, p_state, (None, None, None, l), BF16)
        ys, sn = _layer(ys, mod[bp:], n_past, lw, s_state, (cache_kv_latent, cache_k_rope, page_table, l), F32)
        p_new.append(pn)
        s_new.append(sn)
    no_mod = zeros(1, 2, d)
    yp = _prenorm_call(yp, final_norm_g, jnp.broadcast_to(no_mod, (bp, 2, d)), 0, 1, F32).reshape(bp, tp, d)
    ys = _prenorm_call(ys, final_norm_g, jnp.broadcast_to(no_mod, (bs, 2, d)), 0, 1, F32).reshape(bs, ts, d)
    stk = lambda news, j: jnp.stack([n[j] for n in news])
    return (yp, ys) + tuple(stk(p_new, j) for j in range(7)) + tuple(stk(s_new, j) for j in range(7))
```

```python
import functools
import math

import jax
import jax.numpy as jnp
from jax import lax
from jax.experimental import pallas as pl
from jax.experimental.pallas import tpu as pltpu

F32 = jnp.float32
BF16 = jnp.bfloat16
HI = lax.Precision.HIGHEST
NEG = -1e30

RMS_EPS = 1e-6
D_MODEL = 1024
CONV_W = 4
SSD_H, SSD_P, SSD_G, SSD_N = 8, 64, 2, 64
SSD_INNER = SSD_H * SSD_P
SSD_CONV = SSD_INNER + 2 * SSD_G * SSD_N
SSD_CHUNK = 64
HG_H, HG_K, HG_V = 4, 128, 128
HG_INNER = HG_H * HG_V
HG_CHUNK = 16
MLA_H, MLA_QL, MLA_KVL, MLA_NOPE, MLA_ROPE, MLA_V = 8, 256, 256, 64, 32, 64
ROPE_THETA = 10000.0
LRU_W, LRU_NB = 512, 8
LRU_C = 8.0
N_BRANCH = 4
MOE_G, MOE_EPG, MOE_E, MOE_HID = 4, 8, 32, 256

V7X_VMEM_LIMIT = 56 * 1024 * 1024
LANE = 128

P_GL, P_Z, P_HQ, P_HF, P_HI, P_HG, P_LX, P_LY = 0, 4096, 4608, 5120, 5632, 6144, 6656, 7168
P_XBC, P_QD, P_KVD, P_DT, P_KR = 7680, 8448, 8704, 8960, 9088
P_TOTAL = 9216


def _cparams(sem):
    return pltpu.CompilerParams(dimension_semantics=sem, vmem_limit_bytes=V7X_VMEM_LIMIT)


def _silu(x):
    return x * (1.0 / (1.0 + jnp.exp(-x)))


def _sigmoid(x):
    return 1.0 / (1.0 + jnp.exp(-x))


def _softplus(x):
    return jnp.maximum(x, 0.0) + jnp.log(1.0 + jnp.exp(-jnp.abs(x)))


def _iota(shape, dim):
    return lax.broadcasted_iota(jnp.int32, shape, dim)


def _dot(a, b):
    return jnp.dot(a.astype(BF16), b.astype(BF16), preferred_element_type=F32)


def _dot_nt(a, b):
    return lax.dot_general(a.astype(BF16), b.astype(BF16), (((1,), (1,)), ((), ())), preferred_element_type=F32)


def _dot_tn(a, b):
    return lax.dot_general(a.astype(BF16), b.astype(BF16), (((0,), (0,)), ((), ())), preferred_element_type=F32)


def _dot_hi(a, b):
    return jnp.dot(a, b, precision=HI, preferred_element_type=F32)


def _dot_nt_hi(a, b):
    return lax.dot_general(a, b, (((1,), (1,)), ((), ())), precision=HI, preferred_element_type=F32)


def _tril(n):
    return (_iota((n, n), 0) >= _iota((n, n), 1)).astype(F32)


def _eye(n):
    return (_iota((n, n), 0) == _iota((n, n), 1)).astype(F32)


def _row_blocks(b, t, target):
    if t >= target:
        return 1, target
    return min(b, target // t), t


def _mod_kernel(c_ref, w_ref, b_ref, o_ref):
    o_ref[...] = _dot(_silu(c_ref[...]), w_ref[...]) + b_ref[...]


def _mod_call(c, w, b):
    m, d = c.shape
    n = w.shape[1]
    tn = 1536
    return pl.pallas_call(
        _mod_kernel,
        grid=(n // tn,),
        in_specs=[pl.BlockSpec((m, d), lambda j: (0, 0)),
                  pl.BlockSpec((d, tn), lambda j: (0, j)),
                  pl.BlockSpec((1, tn), lambda j: (0, j))],
        out_specs=pl.BlockSpec((m, tn), lambda j: (0, j)),
        out_shape=jax.ShapeDtypeStruct((m, n), F32),
        compiler_params=_cparams(("parallel",)),
        name="adaln_mod",
    )(c, w, b.reshape(1, n))


def _prenorm_kernel(x_ref, g_ref, mod_ref, o_ref, *, sh_row, sc_row):
    x = x_ref[...]
    bb, tt, d = x.shape
    y = x * lax.rsqrt(jnp.mean(x * x, axis=-1, keepdims=True) + RMS_EPS) * g_ref[...]
    mod = mod_ref[...]
    y = y * (1.0 + mod[:, sc_row:sc_row + 1, :]) + mod[:, sh_row:sh_row + 1, :]
    o_ref[...] = y.reshape(bb * tt, d).astype(o_ref.dtype)


def _prenorm_call(x, g, mod, sh_row, sc_row, out_dtype=BF16):
    b, t, d = x.shape
    bb, tt = _row_blocks(b, t, 512)
    nt = t // tt
    return pl.pallas_call(
        functools.partial(_prenorm_kernel, sh_row=sh_row, sc_row=sc_row),
        grid=(b // bb, nt),
        in_specs=[pl.BlockSpec((bb, tt, d), lambda i, j: (i, j, 0)),
                  pl.BlockSpec((1, 1, d), lambda i, j: (0, 0, 0)),
                  pl.BlockSpec((bb, mod.shape[1], d), lambda i, j: (i, 0, 0))],
        out_specs=pl.BlockSpec((bb * tt, d), lambda i, j: (i * nt + j, 0)),
        out_shape=jax.ShapeDtypeStruct((b * t, d), out_dtype),
        compiler_params=_cparams(("parallel", "parallel")),
        name="prenorm",
    )(x, g.reshape(1, 1, d), mod)


def _matmul_kernel(x_ref, w_ref, o_ref):
    o_ref[...] = jnp.dot(x_ref[...], w_ref[...], preferred_element_type=F32)


def _matmul_call(x, w, tm=1024, tn=1024):
    m, k = x.shape
    n = w.shape[1]
    tm = min(tm, m)
    return pl.pallas_call(
        _matmul_kernel,
        grid=(n // tn, m // tm),
        in_specs=[pl.BlockSpec((tm, k), lambda j, i: (i, 0)),
                  pl.BlockSpec((k, tn), lambda j, i: (0, j))],
        out_specs=pl.BlockSpec((tm, tn), lambda j, i: (i, j)),
        out_shape=jax.ShapeDtypeStruct((m, n), F32),
        compiler_params=_cparams(("parallel", "parallel")),
        name="in_proj",
    )(x, w)


def _pack_w_in(w_in):
    d = w_in.shape[0]
    sizes = (SSD_INNER, SSD_CONV, SSD_H, HG_H * HG_K, HG_H * HG_K, HG_INNER, HG_INNER,
             MLA_QL, MLA_KVL, MLA_ROPE, LRU_W, LRU_W, N_BRANCH * D_MODEL)
    offs = [0]
    for s in sizes:
        offs.append(offs[-1] + s)
    z, xbc, dt, hq, hf, hi, hg, qd, kvd, krr, lx, ly, gl = [w_in[:, offs[i]:offs[i + 1]] for i in range(13)]
    pad = lambda a: jnp.pad(a, ((0, 0), (0, LANE - a.shape[1])))
    return jnp.concatenate([gl, z, hq, hf, hi, hg, lx, ly, xbc, qd, kvd, pad(dt), pad(krr)], axis=1).astype(BF16)


def _causal_conv(xx_ref, x, w_ref, b_ref, tt):
    xx_ref[8:8 + tt, :] = x
    w = w_ref[...]
    y = b_ref[...] + w[3:4, :] * x
    for k in range(1, CONV_W):
        y = y + w[3 - k:4 - k, :] * xx_ref[8 - k:8 - k + tt, :]
    return y


def _causal_conv_short(x, hist, w_ref, b_ref, tt):
    rows_n = x.shape[0]
    tpos = _iota(x.shape, 0) % tt
    w = w_ref[...]
    y = b_ref[...] + w[3:4, :] * x
    for k in range(1, CONV_W):
        src = jnp.where(tpos >= k, pltpu.roll(x, k, 0), pltpu.roll(hist, rows_n - tt + k, 0))
        y = y + w[3 - k:4 - k, :] * src
    return y


def _ssd_kernel(z_ref, xbc_ref, dt_ref, cw_ref, cb_ref, dtb_ref, alog_ref, dfull_ref, ng_ref, s0_ref, c0_ref,
                y_ref, s1_ref, st_ref, xx_ref, yc_ref, *, bb, tt, cl):
    t = pl.program_id(1)
    nt = pl.num_programs(1)
    rows_n = bb * tt
    hp = SSD_H // SSD_G * SSD_P
    heads = [divmod(h, SSD_H // SSD_G) for h in range(SSD_H)]

    @pl.when(t == 0)
    def _():
        for bi in range(bb):
            for h, (g, r) in enumerate(heads):
                st_ref[bi, g, :, r * SSD_P:(r + 1) * SSD_P] = s0_ref[bi, h].T

    if bb == 1:
        @pl.when(t == 0)
        def _():
            xx_ref[0:8, :] = jnp.zeros((8, SSD_CONV), F32)
            xx_ref[5:8, :] = c0_ref[0]

        conv = _causal_conv(xx_ref, xbc_ref[...], cw_ref, cb_ref, tt)
        xx_ref[0:8, :] = xx_ref[tt:tt + 8, :]
    else:
        conv = _causal_conv_short(xbc_ref[...], c0_ref[...], cw_ref, cb_ref, tt)
    act = _silu(conv)
    xs = act[:, :SSD_INNER]
    bm = act[:, SSD_INNER:SSD_INNER + SSD_G * SSD_N]
    cm = act[:, SSD_INNER + SSD_G * SSD_N:]
    dt = _softplus(dt_ref[...] + dtb_ref[...])
    da = dt * (-jnp.exp(alog_ref[...]))
    expand = (_iota((LANE, SSD_INNER), 1) // SSD_P == _iota((LANE, SSD_INNER), 0)).astype(F32)
    dtf = _dot_hi(dt, expand)
    ti, si = _iota((rows_n, rows_n), 0), _iota((rows_n, rows_n), 1)
    same_chunk = ti // cl == si // cl
    acs_all = _dot_hi((same_chunk & (ti >= si)).astype(F32), da)
    acs_t = lax.dot_general(da, (same_chunk & (ti <= si)).astype(F32), (((0,), (0,)), ((), ())),
                            precision=HI, preferred_element_type=F32)
    last_all = _dot_hi((si == ti // cl * cl + (cl - 1)).astype(F32), acs_all)
    eacs_all = jnp.exp(_dot_hi(acs_all, expand))
    dend_all = jnp.exp(_dot_hi(last_all - acs_all, expand))
    xdt_all = xs * dtf
    xsc_all = xdt_all * dend_all
    tri = _iota((cl, cl), 0) >= _iota((cl, cl), 1)

    for c in range(rows_n // cl):
        rows = slice(c * cl, (c + 1) * cl)
        bi = c * cl // tt
        eacs = eacs_all[rows]
        xdt = xdt_all[rows]
        xsc = xsc_all[rows]
        for g in range(SSD_G):
            bg = bm[rows, g * SSD_N:(g + 1) * SSD_N]
            cg = cm[rows, g * SSD_N:(g + 1) * SSD_N]
            cb = _dot_nt(cg, bg)
            sg = st_ref[bi, g]
            y_inter = _dot(cg, sg) * eacs[:, g * hp:(g + 1) * hp]
            for r in range(SSD_H // SSD_G):
                h = g * (SSD_H // SSD_G) + r
                seg = acs_all[rows, h:h + 1] - acs_t[h:h + 1, rows]
                decay = jnp.exp(jnp.where(tri, seg, NEG))
                y_h = _dot(cb * decay, xdt[:, h * SSD_P:(h + 1) * SSD_P])
                yc_ref[rows, h * SSD_P:(h + 1) * SSD_P] = y_h + y_inter[:, r * SSD_P:(r + 1) * SSD_P]
            cdec = eacs[cl - 1:cl, g * hp:(g + 1) * hp]
            st_ref[bi, g] = cdec * sg + _dot_tn(bg, xsc[:, g * hp:(g + 1) * hp])

    y = yc_ref[...] + dfull_ref[...] * xs
    yz = y * _silu(z_ref[...])
    out = yz * lax.rsqrt(jnp.mean(yz * yz, axis=-1, keepdims=True) + RMS_EPS) * ng_ref[...]
    y_ref[...] = out.astype(y_ref.dtype)

    @pl.when(t == nt - 1)
    def _():
        for bi in range(bb):
            for h, (g, r) in enumerate(heads):
                s1_ref[bi, h] = st_ref[bi, g, :, r * SSD_P:(r + 1) * SSD_P].T


def _conv_tail(p, b, t, off, width):
    return p[:, off:off + width].reshape(b, t, width)[:, t - (CONV_W - 1):, :]


def _ssd_call(p, b, t, conv_w, conv_b, dt_bias, a_log, d_skip, norm_g, s0, c0, out_dtype):
    bb, tt = (1, min(t, 256)) if t >= 256 else (min(b, 16), t)
    cl = SSD_CHUNK if tt % SSD_CHUNK == 0 else tt
    nt = t // tt
    rows_n = bb * tt
    padl = lambda v: jnp.pad(v, (0, LANE - v.shape[0])).reshape(1, LANE)
    row = lambda i, j: i * nt + j
    const2 = lambda i, j: (0, 0)
    if bb == 1:
        hist, hist_spec = c0, pl.BlockSpec((1, CONV_W - 1, SSD_CONV), lambda i, j: (i, 0, 0))
    else:
        hist = jnp.pad(c0, ((0, 0), (tt - (CONV_W - 1), 0), (0, 0))).reshape(b * tt, SSD_CONV)
        hist_spec = pl.BlockSpec((rows_n, SSD_CONV), lambda i, j: (i, 0))
    y, s1 = pl.pallas_call(
        functools.partial(_ssd_kernel, bb=bb, tt=tt, cl=cl),
        grid=(b // bb, nt),
        in_specs=[pl.BlockSpec((rows_n, SSD_INNER), lambda i, j: (row(i, j), P_Z // SSD_INNER)),
                  pl.BlockSpec((rows_n, SSD_CONV), lambda i, j: (row(i, j), P_XBC // SSD_CONV)),
                  pl.BlockSpec((rows_n, LANE), lambda i, j: (row(i, j), P_DT // LANE)),
                  pl.BlockSpec((CONV_W, SSD_CONV), const2),
                  pl.BlockSpec((1, SSD_CONV), const2),
                  pl.BlockSpec((1, LANE), const2),
                  pl.BlockSpec((1, LANE), const2),
                  pl.BlockSpec((1, SSD_INNER), const2),
                  pl.BlockSpec((1, SSD_INNER), const2),
                  pl.BlockSpec((bb, SSD_H, SSD_P, SSD_N), lambda i, j: (i, 0, 0, 0)),
                  hist_spec],
        out_specs=[pl.BlockSpec((rows_n, SSD_INNER), lambda i, j: (row(i, j), 0)),
                   pl.BlockSpec((bb, SSD_H, SSD_P, SSD_N), lambda i, j: (i, 0, 0, 0))],
        out_shape=[jax.ShapeDtypeStruct((b * t, SSD_INNER), out_dtype),
                   jax.ShapeDtypeStruct((b, SSD_H, SSD_P, SSD_N), F32)],
        scratch_shapes=[pltpu.VMEM((bb, SSD_G, SSD_N, SSD_H // SSD_G * SSD_P), F32),
                        pltpu.VMEM((tt + 8, SSD_CONV), F32),
                        pltpu.VMEM((rows_n, SSD_INNER), F32)],
        compiler_params=_cparams(("parallel", "arbitrary")),
        name="ssd",
    )(p, p, p, conv_w, conv_b.reshape(1, -1), padl(dt_bias), padl(a_log),
      jnp.repeat(d_skip, SSD_P).reshape(1, -1), norm_g.reshape(1, -1), s0, hist)
    return y, s1, _conv_tail(p, b, t, P_XBC, SSD_CONV)


def _hgrn_kernel(q_ref, f_ref, i_ref, g_ref, loglb_ref, log1m_ref, onem_ref, ng_ref, s0_ref,
                 y_ref, s1_ref, st_ref, o_ref, *, bb, tt, cl):
    t = pl.program_id(1)
    nt = pl.num_programs(1)
    rows_n = bb * tt
    hs = [slice(h * HG_K, (h + 1) * HG_K) for h in range(HG_H)]

    @pl.when(t == 0)
    def _():
        for bi in range(bb):
            for h in range(HG_H):
                st_ref[bi, h] = s0_ref[bi, h].T

    q, hf, v = q_ref[...], f_ref[...], i_ref[...]
    b_ = log1m_ref[...] - _softplus(-hf)
    loglb = loglb_ref[...]
    logf = jnp.maximum(loglb, b_) + jnp.log(1.0 + jnp.exp(-jnp.abs(loglb - b_)))
    kin = onem_ref[...] * _sigmoid(-hf)
    ti, si = _iota((rows_n, rows_n), 0), _iota((rows_n, rows_n), 1)
    bc = _dot_hi(((ti // cl == si // cl) & (ti >= si)).astype(F32), logf)
    bl = _dot_hi((si == ti // cl * cl + (cl - 1)).astype(F32), bc)
    in_chunk = _iota((rows_n, HG_INNER), 0) % cl
    p = q * kin
    o = [jnp.sum(p[:, s], axis=-1, keepdims=True) * v[:, s] for s in hs]
    for d in range(1, cl):
        kd, bd, vd = pltpu.roll(kin, d, 0), pltpu.roll(bc, d, 0), pltpu.roll(v, d, 0)
        p = q * kd * jnp.exp(jnp.where(in_chunk >= d, bc - bd, NEG))
        o = [o[h] + jnp.sum(p[:, s], axis=-1, keepdims=True) * vd[:, s] for h, s in enumerate(hs)]
    for h in range(HG_H):
        o_ref[:, hs[h]] = o[h]
    qe = q * jnp.exp(bc)
    ke = kin * jnp.exp(bl - bc)
    dec = jnp.exp(bl)
    for c in range(rows_n // cl):
        rows = slice(c * cl, (c + 1) * cl)
        bi = c * cl // tt
        for h, s in enumerate(hs):
            st = st_ref[bi, h]
            o_ref[rows, s] += _dot_nt(qe[rows, s], st)
            st_ref[bi, h] = dec[c * cl:c * cl + 1, s] * st + _dot_tn(v[rows, s], ke[rows, s])
    ng = ng_ref[...]
    outs = []
    for s in hs:
        oh = o_ref[:, s]
        outs.append(oh * lax.rsqrt(jnp.mean(oh * oh, axis=-1, keepdims=True) + RMS_EPS) * ng)
    y_ref[...] = (jnp.concatenate(outs, axis=-1) * _silu(g_ref[...])).astype(y_ref.dtype)

    @pl.when(t == nt - 1)
    def _():
        for bi in range(bb):
            for h in range(HG_H):
                s1_ref[bi, h] = st_ref[bi, h].T


def _hgrn_call(p, b, t, lb, norm_g, s0, out_dtype):
    bb, tt = _row_blocks(b, t, 256) if t >= 256 else (min(b, 16), t)
    cl = HG_CHUNK if tt % HG_CHUNK == 0 else tt
    nt = t // tt
    row = lambda i, j: i * nt + j
    const2 = lambda i, j: (0, 0)
    col = lambda off: (lambda i, j: (row(i, j), off // HG_INNER))
    vec = pl.BlockSpec((1, HG_INNER), const2)
    blk = pl.BlockSpec((bb * tt, HG_INNER), lambda i, j: (row(i, j), 0))
    return pl.pallas_call(
        functools.partial(_hgrn_kernel, bb=bb, tt=tt, cl=cl),
        grid=(b // bb, nt),
        in_specs=[pl.BlockSpec((bb * tt, HG_INNER), col(P_HQ)), pl.BlockSpec((bb * tt, HG_INNER), col(P_HF)),
                  pl.BlockSpec((bb * tt, HG_INNER), col(P_HI)), pl.BlockSpec((bb * tt, HG_INNER), col(P_HG)),
                  vec, vec, vec, pl.BlockSpec((1, HG_V), const2),
                  pl.BlockSpec((bb, HG_H, HG_K, HG_V), lambda i, j: (i, 0, 0, 0))],
        out_specs=[blk, pl.BlockSpec((bb, HG_H, HG_K, HG_V), lambda i, j: (i, 0, 0, 0))],
        out_shape=[jax.ShapeDtypeStruct((b * t, HG_INNER), out_dtype),
                   jax.ShapeDtypeStruct((b, HG_H, HG_K, HG_V), F32)],
        scratch_shapes=[pltpu.VMEM((bb, HG_H, HG_V, HG_K), F32), pltpu.VMEM((bb * tt, HG_INNER), F32)],
        compiler_params=_cparams(("parallel", "arbitrary")),
        name="hgrn2",
    )(p, p, p, p, jnp.log(lb).reshape(1, -1), jnp.log1p(-lb).reshape(1, -1), (1.0 - lb).reshape(1, -1),
      norm_g.reshape(1, -1), s0)


def _gelu_tanh(x):
    return 0.5 * x * (1.0 + jnp.tanh(math.sqrt(2.0 / math.pi) * (x + 0.044715 * (x * x * x))))


def _lru_kernel(lx_ref, ly_ref, cw_ref, cb_ref, wr_ref, br_ref, wi_ref, bi_ref, ap_ref, h0_ref, c0_ref,
                y_ref, h1_ref, carry_ref, xx_ref, hs_ref, *, bb, tt, pos0):
    t = pl.program_id(1)
    nt = pl.num_programs(1)
    rows_n = bb * tt
    groups = tt // 8

    if bb == 1:
        @pl.when(t == 0)
        def _():
            xx_ref[0:8, :] = jnp.zeros((8, LRU_W), F32)
            xx_ref[5:8, :] = c0_ref[0]
            carry_ref[...] = h0_ref[0]

        xl = _causal_conv(xx_ref, lx_ref[...], cw_ref, cb_ref, tt)
        xx_ref[0:8, :] = xx_ref[tt:tt + 8, :]
    else:
        xl = _causal_conv_short(lx_ref[...], c0_ref[...], cw_ref, cb_ref, tt)
    r = _sigmoid(_dot(xl, wr_ref[...]) + br_ref[...])
    ig = _sigmoid(_dot(xl, wi_ref[...]) + bi_ref[...])
    log_a = -LRU_C * r * _softplus(-ap_ref[...])
    a = jnp.exp(log_a)
    mult = jnp.sqrt(jnp.tanh(-log_a) * (a * a + 1.0))
    pos = pos0 + t * tt + _iota((rows_n, LRU_W), 0) % tt
    u = jnp.where(pos == 0, 1.0, mult) * (ig * xl)
    row8 = _iota((8, LRU_W), 0)
    carry = carry_ref[...]
    for gi in range(rows_n // 8):
        bi, gj = divmod(gi, groups)
        if bb > 1 and gj == 0:
            carry = h0_ref[bi]
        a8, u8 = a[gi * 8:(gi + 1) * 8], u[gi * 8:(gi + 1) * 8]
        for k in (1, 2, 4):
            m = row8 >= k
            u8, a8 = (jnp.where(m, a8 * pltpu.roll(u8, k, 0) + u8, u8),
                      jnp.where(m, a8 * pltpu.roll(a8, k, 0), a8))
        h8 = u8 + a8 * carry
        carry = h8[7:8, :]
        hs_ref[gi * 8:(gi + 1) * 8, :] = h8
        if bb > 1 and gj == groups - 1:
            h1_ref[bi] = carry
    y_ref[...] = (hs_ref[...] * _gelu_tanh(ly_ref[...])).astype(y_ref.dtype)

    if bb == 1:
        carry_ref[...] = carry

        @pl.when(t == nt - 1)
        def _():
            h1_ref[0] = carry


def _block_diag(w):
    nb, di, do = w.shape
    eye = jnp.eye(nb, dtype=w.dtype)
    return (eye[:, None, :, None] * w[:, :, None, :]).reshape(nb * di, nb * do)


def _lru_call(p, b, t, pos0, conv_w, conv_b, w_r, b_r, w_i, b_i, a_param, h0, c0, out_dtype):
    bb, tt = (1, min(t, 256)) if t >= 256 else (min(b, 16), t)
    nt = t // tt
    rows_n = bb * tt
    row = lambda i, j: i * nt + j
    const2 = lambda i, j: (0, 0)
    vec = pl.BlockSpec((1, LRU_W), const2)
    mat = pl.BlockSpec((LRU_W, LRU_W), const2)
    if bb == 1:
        hist, hist_spec = c0, pl.BlockSpec((1, CONV_W - 1, LRU_W), lambda i, j: (i, 0, 0))
    else:
        hist = jnp.pad(c0, ((0, 0), (tt - (CONV_W - 1), 0), (0, 0))).reshape(b * tt, LRU_W)
        hist_spec = pl.BlockSpec((rows_n, LRU_W), lambda i, j: (i, 0))
    y, h1 = pl.pallas_call(
        functools.partial(_lru_kernel, bb=bb, tt=tt, pos0=pos0),
        grid=(b // bb, nt),
        in_specs=[pl.BlockSpec((rows_n, LRU_W), lambda i, j: (row(i, j), P_LX // LRU_W)),
                  pl.BlockSpec((rows_n, LRU_W), lambda i, j: (row(i, j), P_LY // LRU_W)),
                  pl.BlockSpec((CONV_W, LRU_W), const2), vec, mat, vec, mat, vec, vec,
                  pl.BlockSpec((bb, 1, LRU_W), lambda i, j: (i, 0, 0)),
                  hist_spec],
        out_specs=[pl.BlockSpec((rows_n, LRU_W), lambda i, j: (row(i, j), 0)),
                   pl.BlockSpec((bb, 1, LRU_W), lambda i, j: (i, 0, 0))],
        out_shape=[jax.ShapeDtypeStruct((b * t, LRU_W), out_dtype),
                   jax.ShapeDtypeStruct((b, 1, LRU_W), F32)],
        scratch_shapes=[pltpu.VMEM((1, LRU_W), F32),
                        pltpu.VMEM((tt + 8, LRU_W), F32),
                        pltpu.VMEM((rows_n, LRU_W), F32)],
        compiler_params=_cparams(("parallel", "arbitrary")),
        name="rglru",
    )(p, p, conv_w, conv_b.reshape(1, -1), _block_diag(w_r).astype(BF16), b_r.reshape(1, -1),
      _block_diag(w_i).astype(BF16), b_i.reshape(1, -1), a_param.reshape(1, -1),
      h0.reshape(b, 1, LRU_W), hist)
    return y, h1.reshape(b, LRU_W), _conv_tail(p, b, t, P_LX, LRU_W)


MLA_SCALE = (MLA_NOPE + MLA_ROPE) ** -0.5
ROPE_HALF = MLA_ROPE // 2


def _rope_rotate(x, cos, sin):
    lane = _iota(x.shape, 1)
    rot = jnp.where(lane % MLA_ROPE < ROPE_HALF, -pltpu.roll(x, LANE - ROPE_HALF, 1), pltpu.roll(x, ROPE_HALF, 1))
    return x * cos + rot * sin


MLA_QK = MLA_KVL + LANE

def _mla_prep_kernel(qd_ref, kvd_ref, krr_ref, cos_ref, sin_ref, qg_ref, wuq_ref, kvg_ref, wukt_ref,
                     qcat_ref, ckv_ref, krope_ref, kcat_ref):
    qd = qd_ref[...]
    qn = qd * lax.rsqrt(jnp.mean(qd * qd, axis=-1, keepdims=True) + RMS_EPS) * qg_ref[...]
    q = _dot(qn, wuq_ref[...])
    cos, sin = cos_ref[...], sin_ref[...]
    nn = MLA_H * MLA_NOPE
    lane = _iota((q.shape[0], LANE), 1)
    halves = [_rope_rotate(q[:, nn + i * LANE:nn + (i + 1) * LANE], cos, sin) for i in range(2)]
    per_half = LANE // MLA_ROPE
    for h in range(MLA_H):
        qa = _dot(q[:, h * MLA_NOPE:(h + 1) * MLA_NOPE], wukt_ref[h]) * MLA_SCALE
        qcat_ref[h, :, :MLA_KVL] = qa.astype(qcat_ref.dtype)
        half, sh = halves[h // per_half], (h % per_half) * MLA_ROPE
        piece = pltpu.roll(half, LANE - sh, 1) if sh else half
        qcat_ref[h, :, MLA_KVL:] = jnp.where(lane < MLA_ROPE, piece * MLA_SCALE, 0.0).astype(qcat_ref.dtype)
    kvd = kvd_ref[...]
    ckv = kvd * lax.rsqrt(jnp.mean(kvd * kvd, axis=-1, keepdims=True) + RMS_EPS) * kvg_ref[...]
    ckv_ref[...] = ckv
    kr = jnp.where(lane < MLA_ROPE, _rope_rotate(krr_ref[...], cos, sin), 0.0)
    krope_ref[...] = kr[:, :MLA_ROPE]
    kcat_ref[:, :MLA_KVL] = ckv.astype(BF16)
    kcat_ref[:, MLA_KVL:] = kr.astype(BF16)


def _lane_wide(v, n):
    if n == LANE:
        return v
    return jnp.concatenate([v] * (n // LANE), axis=1) if n % LANE == 0 else v[:, :n]


def _softmax_step(s, vals, m_ref, l_ref, acc_ref):
    m_old = m_ref[...]
    m_new = jnp.maximum(m_old, jnp.max(s, axis=-1, keepdims=True))
    pr = jnp.exp(s - _lane_wide(m_new, s.shape[1]))
    alpha = jnp.exp(m_old - m_new)
    l_ref[...] = alpha * l_ref[...] + jnp.sum(pr, axis=-1, keepdims=True)
    acc_ref[...] = _lane_wide(alpha, acc_ref.shape[1]) * acc_ref[...] + _dot(pr, vals)
    m_ref[...] = m_new


def _softmax_init(m_ref, l_ref, acc_ref):
    m_ref[...] = jnp.full(m_ref.shape, NEG, F32)
    l_ref[...] = jnp.zeros(l_ref.shape, F32)
    acc_ref[...] = jnp.zeros(acc_ref.shape, F32)


def _mla_attn_prompt_kernel(q_ref, k_ref, wuv_ref, y_ref, m_ref, l_ref, acc_ref, *, tq, tk):
    i, j = pl.program_id(1), pl.program_id(2)
    rows = MLA_H * tq
    last = (i * tq + tq - 1) // tk

    @pl.when(j == 0)
    def _():
        _softmax_init(m_ref, l_ref, acc_ref)

    def step(diagonal):
        k = k_ref[...]
        s = _dot_nt(q_ref[...].reshape(rows, MLA_QK), k)
        if diagonal:
            s = jnp.where(j * tk + _iota((rows, tk), 1) <= i * tq + _iota((rows, tk), 0) % tq, s, NEG)
        _softmax_step(s, k[:, :MLA_KVL], m_ref, l_ref, acc_ref)

    @pl.when(j < last)
    def _():
        step(False)

    @pl.when(j == last)
    def _():
        step(True)
        o = acc_ref[...] / _lane_wide(l_ref[...], MLA_KVL)
        for h in range(MLA_H):
            y_ref[:, h * MLA_V:(h + 1) * MLA_V] = _dot(o[h * tq:(h + 1) * tq], wuv_ref[h]).astype(y_ref.dtype)


def _mla_attn_sample_kernel(pt_ref, cache_c_ref, cache_rt_ref, q_ref, ckv_ref, krope_ref, wuv_ref, y_ref,
                            kc_buf, krt_buf, sem, *, layer, n_pages, page, tq):
    b = pl.program_id(0)
    nb = pl.num_programs(0)
    rows = MLA_H * tq

    def page_copies(seq, slot, pg):
        pid = pt_ref[seq, pg]
        return (pltpu.make_async_copy(cache_c_ref.at[layer, pid], kc_buf.at[slot, pl.ds(pg * page, page), :],
                                      sem.at[0, slot]),
                pltpu.make_async_copy(cache_rt_ref.at[layer, pid], krt_buf.at[slot, :, pl.ds(pg * page, page)],
                                      sem.at[1, slot]))

    def start_gather(seq, slot):
        def body(pg, carry):
            for cp in page_copies(seq, slot, pg):
                cp.start()
            return carry
        lax.fori_loop(0, n_pages, body, 0)

    def wait_gather(seq, slot):
        def body(pg, carry):
            for cp in page_copies(seq, slot, pg):
                cp.wait()
            return carry
        lax.fori_loop(0, n_pages, body, 0)

    slot = b % 2

    @pl.when(b == 0)
    def _():
        start_gather(0, 0)

    @pl.when(b + 1 < nb)
    def _():
        start_gather(b + 1, 1 - slot)

    wait_gather(b, slot)

    qcat = q_ref[...].reshape(rows, MLA_QK)
    q, qr = qcat[:, :MLA_KVL], qcat[:, MLA_KVL:MLA_KVL + MLA_ROPE]
    kc = kc_buf[slot].astype(BF16)
    s_old = _dot_nt(q, kc) + _dot(qr, krt_buf[slot])
    kc_new, kr_new = ckv_ref[...], krope_ref[...]
    s_new = _dot_nt(q, kc_new) + _dot_nt(qr, kr_new)
    s_new = jnp.where(_iota((rows, tq), 1) <= _iota((rows, tq), 0) % tq, s_new, NEG)
    m = jnp.maximum(jnp.max(s_old, axis=-1, keepdims=True), jnp.max(s_new, axis=-1, keepdims=True))
    p_old, p_new = jnp.exp(s_old - m), jnp.exp(s_new - m)
    denom = jnp.sum(p_old, axis=-1, keepdims=True) + jnp.sum(p_new, axis=-1, keepdims=True)
    o = (_dot(p_old, kc) + _dot(p_new, kc_new)) / denom
    for h in range(MLA_H):
        y_ref[:, h * MLA_V:(h + 1) * MLA_V] = _dot(o[h * tq:(h + 1) * tq], wuv_ref[h]).astype(y_ref.dtype)


def _rope_tables(pos):
    inv = ROPE_THETA ** (-jnp.arange(ROPE_HALF, dtype=F32) * 2.0 / MLA_ROPE)
    ang = pos.astype(F32)[:, None] * inv[None, :]
    reps = LANE // ROPE_HALF
    return jnp.tile(jnp.cos(ang), (1, reps)), jnp.tile(jnp.sin(ang), (1, reps))


def _mla_call(p, b, t, pos0, q_norm_g, w_uq, kv_norm_g, w_ukv, cache_c, cache_r, page_table, layer, out_dtype):
    m = b * t
    tm = min(m, 256)
    wq = w_uq.reshape(MLA_QL, MLA_H, MLA_NOPE + MLA_ROPE)
    wq = jnp.concatenate([wq[..., :MLA_NOPE].reshape(MLA_QL, -1), wq[..., MLA_NOPE:].reshape(MLA_QL, -1)], axis=1)
    wkv = w_ukv.reshape(MLA_KVL, MLA_H, MLA_NOPE + MLA_V)
    wukt = jnp.transpose(wkv[..., :MLA_NOPE], (1, 2, 0)).astype(BF16)
    wuv = jnp.transpose(wkv[..., MLA_NOPE:], (1, 0, 2)).astype(BF16)
    cos, sin = _rope_tables(pos0 + jnp.arange(t))
    if t < tm:
        cos, sin = jnp.tile(cos, (tm // t, 1)), jnp.tile(sin, (tm // t, 1))
    ntab = cos.shape[0] // tm
    qdt = BF16 if t >= tm else F32
    const2 = lambda i: (0, 0)
    qcat, ckv, krope, kcat = pl.pallas_call(
        _mla_prep_kernel,
        grid=(m // tm,),
        in_specs=[pl.BlockSpec((tm, MLA_QL), lambda i: (i, P_QD // MLA_QL)),
                  pl.BlockSpec((tm, MLA_KVL), lambda i: (i, P_KVD // MLA_KVL)),
                  pl.BlockSpec((tm, LANE), lambda i: (i, P_KR // LANE)),
                  pl.BlockSpec((tm, LANE), lambda i: (i % ntab, 0)),
                  pl.BlockSpec((tm, LANE), lambda i: (i % ntab, 0)),
                  pl.BlockSpec((1, MLA_QL), const2),
                  pl.BlockSpec((MLA_QL, MLA_H * (MLA_NOPE + MLA_ROPE)), const2),
                  pl.BlockSpec((1, MLA_KVL), const2),
                  pl.BlockSpec((MLA_H, MLA_NOPE, MLA_KVL), lambda i: (0, 0, 0))],
        out_specs=[pl.BlockSpec((MLA_H, tm, MLA_QK), lambda i: (0, i, 0)),
                   pl.BlockSpec((tm, MLA_KVL), lambda i: (i, 0)),
                   pl.BlockSpec((tm, MLA_ROPE), lambda i: (i, 0)),
                   pl.BlockSpec((tm, MLA_QK), lambda i: (i, 0))],
        out_shape=[jax.ShapeDtypeStruct((MLA_H, m, MLA_QK), qdt),
                   jax.ShapeDtypeStruct((m, MLA_KVL), F32),
                   jax.ShapeDtypeStruct((m, MLA_ROPE), F32),
                   jax.ShapeDtypeStruct((m, MLA_QK), BF16)],
        compiler_params=_cparams(("parallel",)),
        name="mla_prep",
    )(p, p, p, cos, sin, q_norm_g.reshape(1, -1), wq.astype(BF16), kv_norm_g.reshape(1, -1), wukt)
    rows = MLA_H * min(t, 256)
    softmax_scratch = [pltpu.VMEM((rows, LANE), F32), pltpu.VMEM((rows, LANE), F32), pltpu.VMEM((rows, MLA_KVL), F32)]

    if cache_c is None:
        tq = min(t, 256)
        tk = min(t, 512)
        nq, nk = t // tq, t // tk
        last = lambda i: (i * tq + tq - 1) // tk
        y = pl.pallas_call(
            functools.partial(_mla_attn_prompt_kernel, tq=tq, tk=tk),
            grid=(b, nq, nk),
            in_specs=[pl.BlockSpec((MLA_H, tq, MLA_QK), lambda bi, i, j: (0, bi * nq + i, 0)),
                      pl.BlockSpec((tk, MLA_QK), lambda bi, i, j: (bi * nk + jnp.minimum(j, last(i)), 0)),
                      pl.BlockSpec((MLA_H, MLA_KVL, MLA_V), lambda bi, i, j: (0, 0, 0))],
            out_specs=pl.BlockSpec((tq, MLA_H * MLA_V), lambda bi, i, j: (bi * nq + i, 0)),
            out_shape=jax.ShapeDtypeStruct((m, MLA_H * MLA_V), out_dtype),
            scratch_shapes=softmax_scratch,
            compiler_params=_cparams(("parallel", "parallel", "arbitrary")),
            name="mla_attn_prompt",
        )(qcat, kcat, wuv)
        return y, ckv, krope

    n_pages = page_table.shape[1]
    page = cache_c.shape[2]
    cache_rt = jnp.swapaxes(cache_r, 2, 3)
    y = pl.pallas_call(
        functools.partial(_mla_attn_sample_kernel, layer=layer, n_pages=n_pages, page=page, tq=t),
        grid_spec=pltpu.PrefetchScalarGridSpec(
            num_scalar_prefetch=1,
            grid=(b,),
            in_specs=[pl.BlockSpec(memory_space=pl.ANY), pl.BlockSpec(memory_space=pl.ANY),
                      pl.BlockSpec((MLA_H, t, MLA_QK), lambda bi, pt: (0, bi, 0)),
                      pl.BlockSpec((t, MLA_KVL), lambda bi, pt: (bi, 0)),
                      pl.BlockSpec((t, MLA_ROPE), lambda bi, pt: (bi, 0)),
                      pl.BlockSpec((MLA_H, MLA_KVL, MLA_V), lambda bi, pt: (0, 0, 0))],
            out_specs=pl.BlockSpec((t, MLA_H * MLA_V), lambda bi, pt: (bi, 0)),
            scratch_shapes=[pltpu.VMEM((2, n_pages * page, MLA_KVL), F32),
                            pltpu.VMEM((2, MLA_ROPE, n_pages * page), F32),
                            pltpu.SemaphoreType.DMA((2, 2))]),
        out_shape=jax.ShapeDtypeStruct((m, MLA_H * MLA_V), out_dtype),
        compiler_params=_cparams(("arbitrary",)),
        name="mla_attn_sample",
    )(page_table, cache_c, cache_rt, qcat, ckv, krope, wuv)
    return y, ckv, krope


ROUTER_GRP_LANE = MOE_E


def _merge_kernel(gl_ref, ya_ref, yb_ref, yc_ref, yd_ref, wb_ref, wo_ref, x_ref, mod_ref, ng_ref, wr_ref, br_ref,
                  xn_ref, h2_ref, lg_ref):
    bb, tt, d = x_ref.shape
    merged = None
    for n, y_ref in enumerate((ya_ref, yb_ref, yc_ref, yd_ref)):
        term = _sigmoid(gl_ref[:, n * d:(n + 1) * d]) * _dot(y_ref[...], wb_ref[n])
        merged = term if merged is None else merged + term
    out = _dot(merged, wo_ref[...])
    mod = mod_ref[...]
    x = x_ref[...] + mod[:, 2:3, :] * out.reshape(bb, tt, d)
    xn_ref[...] = x
    h = x * lax.rsqrt(jnp.mean(x * x, axis=-1, keepdims=True) + RMS_EPS) * ng_ref[...]
    h = (h * (1.0 + mod[:, 4:5, :]) + mod[:, 3:4, :]).reshape(bb * tt, d)
    h2_ref[...] = h.astype(h2_ref.dtype)
    lg_ref[...] = _dot_hi(h, wr_ref[...]) + br_ref[...]


def _merge_call(p, ys, w_branch, w_out, x, mod, norm2_g, w_router, b_router):
    b, t, d = x.shape
    bb, tt = _row_blocks(b, t, 512)
    nt = t // tt
    tm = bb * tt
    row = lambda i, j: (i * nt + j, 0)
    yspec = pl.BlockSpec((tm, ys[0].shape[1]), row)
    return pl.pallas_call(
        _merge_kernel,
        grid=(b // bb, nt),
        in_specs=[pl.BlockSpec((tm, N_BRANCH * d), row), yspec, yspec, yspec, yspec,
                  pl.BlockSpec(w_branch.shape, lambda i, j: (0, 0, 0)),
                  pl.BlockSpec(w_out.shape, lambda i, j: (0, 0)),
                  pl.BlockSpec((bb, tt, d), lambda i, j: (i, j, 0)),
                  pl.BlockSpec((bb, mod.shape[1], d), lambda i, j: (i, 0, 0)),
                  pl.BlockSpec((1, 1, d), lambda i, j: (0, 0, 0)),
                  pl.BlockSpec((d, LANE), lambda i, j: (0, 0)),
                  pl.BlockSpec((1, LANE), lambda i, j: (0, 0))],
        out_specs=[pl.BlockSpec((bb, tt, d), lambda i, j: (i, j, 0)),
                   pl.BlockSpec((tm, d), row),
                   pl.BlockSpec((tm, LANE), row)],
        out_shape=[jax.ShapeDtypeStruct((b, t, d), F32),
                   jax.ShapeDtypeStruct((b * t, d), BF16),
                   jax.ShapeDtypeStruct((b * t, LANE), F32)],
        compiler_params=_cparams(("parallel", "parallel")),
        name="merge",
    )(p, *ys, w_branch, w_out, x, mod, norm2_g.reshape(1, 1, d), w_router, b_router)


def _route(logits):
    lane = _iota(logits.shape, 1)
    big = jnp.int32(1 << 20)
    grp = jnp.where(jnp.right_shift(lane, 2) == ROUTER_GRP_LANE // MOE_G, logits, NEG)
    gmax = jnp.max(grp, axis=-1, keepdims=True)
    g_top = 1.0 / jnp.sum(jnp.exp(grp - gmax), axis=-1, keepdims=True)
    gidx = jnp.min(jnp.where(grp == gmax, lane, big), axis=-1, keepdims=True) - ROUTER_GRP_LANE
    el = jnp.where(jnp.right_shift(lane, 3) == gidx, logits, NEG)
    m1 = jnp.max(el, axis=-1, keepdims=True)
    i1 = jnp.min(jnp.where(el == m1, lane, big), axis=-1, keepdims=True)
    el2 = jnp.where(lane == i1, NEG, el)
    m2 = jnp.max(el2, axis=-1, keepdims=True)
    i2 = jnp.min(jnp.where(el2 == m2, lane, big), axis=-1, keepdims=True)
    e2 = jnp.exp(m2 - m1)
    w1 = g_top / (1.0 + e2)
    return jnp.where(lane == i1, w1, 0.0) + jnp.where(lane == i2, w1 * e2, 0.0)


def _moe_dense_kernel(h_ref, lg_ref, wg_ref, wu_ref, wd_ref, x_ref, mod_ref, o_ref, comb_ref, acc_ref):
    e = pl.program_id(1)
    ne = pl.num_programs(1)
    bb, tt, d = x_ref.shape

    @pl.when(e == 0)
    def _():
        comb_ref[...] = _route(lg_ref[...])
        acc_ref[...] = jnp.zeros(acc_ref.shape, F32)

    comb = comb_ref[...]
    ce = jnp.sum(jnp.where(_iota(comb.shape, 1) == e, comb, 0.0), axis=-1, keepdims=True)
    h = h_ref[...]
    hid = _silu(_dot(h, wg_ref[...])) * _dot(h, wu_ref[...]) * ce
    acc_ref[...] += _dot(hid, wd_ref[...])

    @pl.when(e == ne - 1)
    def _():
        o_ref[...] = x_ref[...] + mod_ref[...][:, 5:6, :] * acc_ref[...].reshape(bb, tt, d)


def _moe_call(h2, logits, w_gate, w_up, w_down, x, mod):
    b, t, d = x.shape
    bb, tt = _row_blocks(b, t, 512)
    nt = t // tt
    tm = bb * tt
    ne, _, hid = w_gate.shape
    row = lambda i, e: (i, 0)
    xmap = lambda i, e: (i // nt, i % nt, 0)
    return pl.pallas_call(
        _moe_dense_kernel,
        grid=(b * t // tm, ne),
        in_specs=[pl.BlockSpec((tm, d), row), pl.BlockSpec((tm, LANE), row),
                  pl.BlockSpec((None, d, hid), lambda i, e: (e, 0, 0)),
                  pl.BlockSpec((None, d, hid), lambda i, e: (e, 0, 0)),
                  pl.BlockSpec((None, hid, d), lambda i, e: (e, 0, 0)),
                  pl.BlockSpec((bb, tt, d), xmap),
                  pl.BlockSpec((bb, mod.shape[1], d), lambda i, e: (i // nt, 0, 0))],
        out_specs=pl.BlockSpec((bb, tt, d), xmap),
        out_shape=jax.ShapeDtypeStruct((b, t, d), F32),
        scratch_shapes=[pltpu.VMEM((tm, LANE), F32), pltpu.VMEM((tm, d), F32)],
        compiler_params=_cparams(("parallel", "arbitrary")),
        name="moe",
    )(h2, logits, w_gate, w_up, w_down, x, mod)


MOE_TM = 1024
MOE_RB = 64
MOE_SLOTS = 4096
MOE_SC = 512
MOE_EPS = 4
META_S1, META_S2, META_W1, META_W2 = 0, 1, 2, 3


def _moe_route_kernel(lg_ref, meta_ref, metat_ref, nblk_ref, boff_ref):
    comb = _route(lg_ref[...])
    tm = comb.shape[0]
    lane = _iota(comb.shape, 1)
    big = jnp.int32(1 << 20)
    chosen = comb > 0.0
    a = jnp.where(chosen, 1.0, 0.0)
    rank = _dot((_iota((tm, tm), 0) > _iota((tm, tm), 1)).astype(F32), a)
    cnt = jnp.sum(a, axis=0, keepdims=True)
    nblk = jnp.floor((cnt + (MOE_RB - 1)) * (1.0 / MOE_RB))
    upper = (_iota((LANE, LANE), 0) < _iota((LANE, LANE), 1)).astype(F32)
    boff = _dot_hi(jnp.broadcast_to(nblk, (8, LANE)), upper)[0:1]
    slot = boff * MOE_RB + rank
    i1 = jnp.min(jnp.where(chosen, lane, big), axis=-1, keepdims=True)
    i2 = jnp.max(jnp.where(chosen, lane, -1), axis=-1, keepdims=True)
    pick = lambda v, i: jnp.sum(jnp.where(lane == i, v, 0.0), axis=-1, keepdims=True)
    two = i2 != i1
    s1, w1 = pick(slot, i1), pick(comb, i1)
    s2, w2 = jnp.where(two, pick(slot, i2), -1.0), jnp.where(two, pick(comb, i2), 0.0)
    meta = (jnp.where(lane == META_S1, s1, 0.0) + jnp.where(lane == META_S2, s2, 0.0)
            + jnp.where(lane == META_W1, w1, 0.0) + jnp.where(lane == META_W2, w2, 0.0))
    meta_ref[...] = meta
    metat_ref[...] = meta.T[0:8, :]
    nblk_ref[0] = nblk.astype(jnp.int32)
    boff_ref[0] = boff.astype(jnp.int32)


def _moe_sorted_kernel(nblk_ref, boff_ref, h_ref, meta_ref, metat_ref, wg_ref, wu_ref, wd_ref, x_ref, mod_ref,
                       o_ref, xs_ref, ys_ref):
    i, eg = pl.program_id(0), pl.program_id(1)
    neg = pl.num_programs(1)
    eps = wg_ref.shape[0]
    bb, tt, d = x_ref.shape
    tm = bb * tt
    last = i * LANE + neg * eps - 1
    used = (boff_ref[last] + nblk_ref[last]) * MOE_RB

    @pl.when(eg == 0)
    def _():
        s1 = metat_ref[META_S1:META_S1 + 1, :]
        s2 = metat_ref[META_S2:META_S2 + 1, :]
        h = h_ref[...]
        for c in range(MOE_SLOTS // MOE_SC):
            @pl.when(c * MOE_SC < used)
            def _():
                sl = (c * MOE_SC + _iota((MOE_SC, tm), 0)).astype(F32)
                perm = jnp.where((sl == s1) | (sl == s2), 1.0, 0.0)
                xs_ref[c * MOE_SC:(c + 1) * MOE_SC, :] = _dot(perm, h).astype(BF16)
        ys_ref[...] = jnp.zeros(ys_ref.shape, BF16)

    def expert_rows(j, first_block, n_rows):
        rows = pl.ds(pl.multiple_of(first_block * MOE_RB, MOE_RB), n_rows)
        xb = xs_ref[rows, :]
        hid = _silu(_dot(xb, wg_ref[j])) * _dot(xb, wu_ref[j])
        ys_ref[rows, :] = _dot(hid, wd_ref[j]).astype(BF16)

    for j in range(eps):
        e = i * LANE + eg * eps + j
        b0, nb = boff_ref[e], nblk_ref[e]

        def pair(k, carry, j=j, b0=b0):
            expert_rows(j, b0 + 2 * k, 2 * MOE_RB)
            return carry

        lax.fori_loop(0, lax.shift_right_logical(nb, 1), pair, 0)

        @pl.when(lax.bitwise_and(nb, 1) == 1)
        def _(j=j, b0=b0, nb=nb):
            expert_rows(j, b0 + nb - 1, MOE_RB)

    @pl.when(eg == neg - 1)
    def _():
        meta = meta_ref[...]
        s1, s2 = meta[:, META_S1:META_S1 + 1], meta[:, META_S2:META_S2 + 1]
        w1, w2 = meta[:, META_W1:META_W1 + 1], meta[:, META_W2:META_W2 + 1]
        o_ref[...] = jnp.zeros(o_ref.shape, F32)
        for c in range(MOE_SLOTS // MOE_SC):
            @pl.when(c * MOE_SC < used)
            def _():
                sl = (c * MOE_SC + _iota((tm, MOE_SC), 1)).astype(F32)
                back = jnp.where(sl == s1, w1, 0.0) + jnp.where(sl == s2, w2, 0.0)
                o_ref[...] += _dot(back, ys_ref[c * MOE_SC:(c + 1) * MOE_SC, :]).reshape(bb, tt, d)
        o_ref[...] = x_ref[...] + mod_ref[...][:, 5:6, :] * o_ref[...]


def _moe_sorted_call(h2, logits, w_gate, w_up, w_down, x, mod):
    b, t, d = x.shape
    tm = MOE_TM
    nt = t // tm
    n_tiles = b * nt
    ne, _, hid = w_gate.shape
    meta, metat, nblk, boff = pl.pallas_call(
        _moe_route_kernel,
        grid=(n_tiles,),
        in_specs=[pl.BlockSpec((tm, LANE), lambda i: (i, 0))],
        out_specs=[pl.BlockSpec((tm, LANE), lambda i: (i, 0)),
                   pl.BlockSpec((8, tm), lambda i: (i, 0)),
                   pl.BlockSpec((1, 1, LANE), lambda i: (i, 0, 0)),
                   pl.BlockSpec((1, 1, LANE), lambda i: (i, 0, 0))],
        out_shape=[jax.ShapeDtypeStruct((n_tiles * tm, LANE), F32),
                   jax.ShapeDtypeStruct((n_tiles * 8, tm), F32),
                   jax.ShapeDtypeStruct((n_tiles, 1, LANE), jnp.int32),
                   jax.ShapeDtypeStruct((n_tiles, 1, LANE), jnp.int32)],
        compiler_params=_cparams(("parallel",)),
        name="moe_route",
    )(logits)
    xmap = lambda i, e, nb, bo: (i // nt, i % nt, 0)
    row = lambda i, e, nb, bo: (i, 0)
    wmap = lambda i, e, nb, bo: (e, 0, 0)
    return pl.pallas_call(
        _moe_sorted_kernel,
        grid_spec=pltpu.PrefetchScalarGridSpec(
            num_scalar_prefetch=2,
            grid=(n_tiles, ne // MOE_EPS),
            in_specs=[pl.BlockSpec((tm, d), row), pl.BlockSpec((tm, LANE), row), pl.BlockSpec((8, tm), row),
                      pl.BlockSpec((MOE_EPS, d, hid), wmap), pl.BlockSpec((MOE_EPS, d, hid), wmap),
                      pl.BlockSpec((MOE_EPS, hid, d), wmap),
                      pl.BlockSpec((1, tm, d), xmap),
                      pl.BlockSpec((1, mod.shape[1], d), lambda i, e, nb, bo: (i // nt, 0, 0))],
            out_specs=pl.BlockSpec((1, tm, d), xmap),
            scratch_shapes=[pltpu.VMEM((MOE_SLOTS, d), BF16), pltpu.VMEM((MOE_SLOTS, d), BF16)]),
        out_shape=jax.ShapeDtypeStruct((b, t, d), F32),
        compiler_params=_cparams(("parallel", "arbitrary")),
        name="moe_sorted",
    )(nblk.reshape(-1), boff.reshape(-1), h2, meta, metat, w_gate, w_up, w_down, x, mod)


def _layer(x, mod, pos0, lw, state, cache, out_dtype):
    b, t, d = x.shape
    ssm0, ssm_conv0, hg0, lru0, lru_conv0 = state
    h = _prenorm_call(x, lw['norm1_g'], mod, sh_row=0, sc_row=1)
    p = _matmul_call(h, lw['w_in'])
    y_a, ssm1, ssm_conv1 = _ssd_call(p, b, t, lw['ssd_conv_w'], lw['ssd_conv_b'], lw['ssd_dt_bias'], lw['ssd_a_log'],
                                     lw['ssd_d'], lw['ssd_norm_g'], ssm0, ssm_conv0, out_dtype)
    y_b, hg1 = _hgrn_call(p, b, t, lw['hg_lb'], lw['hg_norm_g'], hg0, out_dtype)
    y_c, ckv, krope = _mla_call(p, b, t, pos0, lw['mla_q_norm_g'], lw['mla_w_uq'], lw['mla_kv_norm_g'],
                                lw['mla_w_ukv'], *cache, out_dtype)
    y_d, lru1, lru_conv1 = _lru_call(p, b, t, pos0, lw['lru_conv_w'], lw['lru_conv_b'], lw['lru_w_r'], lw['lru_b_r'],
                                     lw['lru_w_i'], lw['lru_b_i'], lw['lru_a'], lru0, lru_conv0, out_dtype)
    x, h2, logits = _merge_call(p, (y_a, y_b, y_c, y_d), lw['w_branch'], lw['w_out'], x, mod, lw['norm2_g'],
                                lw['w_router'], lw['b_router'])
    moe = _moe_sorted_call if t % MOE_TM == 0 else _moe_call
    x = moe(h2, logits, lw['moe_w_gate'], lw['moe_w_up'], lw['moe_w_down'], x, mod)
    new =(ckv.reshape(b, t, -1), krope.reshape(b, t, -1), ssm1, ssm_conv1, hg1, lru1, lru_conv1)
    return x, new


def kernel(x_prompt, x_sample, c_prompt, c_sample, cache_kv_latent, cache_k_rope, state_ssm, state_ssm_conv,
           state_hgrn, state_lru, state_lru_conv, page_table, norm1_g, norm2_g, w_mod, b_mod, w_in, ssd_conv_w,
           ssd_conv_b, ssd_dt_bias, ssd_a_log, ssd_d, ssd_norm_g, hg_lb_raw, hg_norm_g, mla_q_norm_g, mla_w_uq,
           mla_kv_norm_g, mla_w_ukv, lru_conv_w, lru_conv_b, lru_w_r, lru_b_r, lru_w_i, lru_b_i, lru_a, w_branch,
           w_out, moe_w_grp, moe_b_grp, moe_w_rt, moe_b_rt, moe_w_gate, moe_w_up, moe_w_down, final_norm_g):
    bp, tp, d = x_prompt.shape
    bs, ts, _ = x_sample.shape
    depth = w_in.shape[0]
    n_past = page_table.shape[1] * cache_kv_latent.shape[2]
    lb_all = jnp.cumsum(jax.nn.softmax(hg_lb_raw.astype(F32), axis=0), axis=0)
    lb_all = lb_all - lb_all[:1]
    c_all = jnp.concatenate([c_prompt, c_sample], axis=0)
    zeros = lambda *s: jnp.zeros(s, F32)
    yp, ys = x_prompt, x_sample
    p_new, s_new = [], []
    for l in range(depth):
        mod = _mod_call(c_all, w_mod[l].astype(BF16), b_mod[l]).reshape(bp + bs, 6, d)
        pad_r = LANE - MOE_E - MOE_G
        lw = {
            'norm1_g': norm1_g[l], 'norm2_g': norm2_g[l], 'w_in': _pack_w_in(w_in[l]),
            'ssd_conv_w': ssd_conv_w[l], 'ssd_conv_b': ssd_conv_b[l], 'ssd_dt_bias': ssd_dt_bias[l],
            'ssd_a_log': ssd_a_log[l], 'ssd_d': ssd_d[l], 'ssd_norm_g': ssd_norm_g[l],
            'hg_lb': lb_all[l], 'hg_norm_g': hg_norm_g[l],
            'mla_q_norm_g': mla_q_norm_g[l], 'mla_w_uq': mla_w_uq[l],
            'mla_kv_norm_g': mla_kv_norm_g[l], 'mla_w_ukv': mla_w_ukv[l],
            'lru_conv_w': lru_conv_w[l], 'lru_conv_b': lru_conv_b[l], 'lru_w_r': lru_w_r[l], 'lru_b_r': lru_b_r[l],
            'lru_w_i': lru_w_i[l], 'lru_b_i': lru_b_i[l], 'lru_a': lru_a[l],
            'w_branch': w_branch[l].astype(BF16), 'w_out': w_out[l].astype(BF16),
            'w_router': jnp.pad(jnp.concatenate([moe_w_rt[l], moe_w_grp[l]], axis=1), ((0, 0), (0, pad_r))),
            'b_router': jnp.pad(jnp.concatenate([moe_b_rt[l], moe_b_grp[l]]), (0, pad_r)).reshape(1, LANE),
            'moe_w_gate': moe_w_gate[l].astype(BF16), 'moe_w_up': moe_w_up[l].astype(BF16),
            'moe_w_down': moe_w_down[l].astype(BF16),
        }
        p_state = (zeros(bp, SSD_H, SSD_P, SSD_N), zeros(bp, CONV_W - 1, SSD_CONV), zeros(bp, HG_H, HG_K, HG_V),
                   zeros(bp, LRU_W), zeros(bp, CONV_W - 1, LRU_W))
        s_state = (state_ssm[l], state_ssm_conv[l], state_hgrn[l], state_lru[l], state_lru_conv[l])
        yp, pn = _layer(yp, mod[:bp], 0, lw, p_state, (None, None, None, l), BF16)
        ys, sn = _layer(ys, mod[bp:], n_past, lw, s_state, (cache_kv_latent, cache_k_rope, page_table, l), F32)
        p_new.append(pn)
        s_new.append(sn)
    no_mod = zeros(1, 2, d)
    yp = _prenorm_call(yp, final_norm_g, jnp.broadcast_to(no_mod, (bp, 2, d)), 0, 1, F32).reshape(bp, tp, d)
    ys = _prenorm_call(ys, final_norm_g, jnp.broadcast_to(no_mod, (bs, 2, d)), 0, 1, F32).reshape(bs, ts, d)
    stk = lambda news, j: jnp.stack([n[j] for n in news])
    return (yp, ys) + tuple(stk(p_new, j) for j in range(7)) + tuple(stk(s_new, j) for j in range(7))
```

```python
import functools
import math

import jax
import jax.numpy as jnp
from jax import lax
from jax.experimental import pallas as pl
from jax.experimental.pallas import tpu as pltpu

F32 = jnp.float32
BF16 = jnp.bfloat16
HI = lax.Precision.HIGHEST
NEG = -1e30

RMS_EPS = 1e-6
D_MODEL = 1024
CONV_W = 4
SSD_H, SSD_P, SSD_G, SSD_N = 8, 64, 2, 64
SSD_INNER = SSD_H * SSD_P
SSD_CONV = SSD_INNER + 2 * SSD_G * SSD_N
SSD_CHUNK = 64
HG_H, HG_K, HG_V = 4, 128, 128
HG_INNER = HG_H * HG_V
HG_CHUNK = 16
MLA_H, MLA_QL, MLA_KVL, MLA_NOPE, MLA_ROPE, MLA_V = 8, 256, 256, 64, 32, 64
ROPE_THETA = 10000.0
LRU_W, LRU_NB = 512, 8
LRU_C = 8.0
N_BRANCH = 4
MOE_G, MOE_EPG, MOE_E, MOE_HID = 4, 8, 32, 256

V7X_VMEM_LIMIT = 56 * 1024 * 1024
LANE = 128

P_GL, P_Z, P_HQ, P_HF, P_HI, P_HG, P_LX, P_LY = 0, 4096, 4608, 5120, 5632, 6144, 6656, 7168
P_XBC, P_QD, P_KVD, P_DT, P_KR = 7680, 8448, 8704, 8960, 9088
P_TOTAL = 9216


def _cparams(sem):
    return pltpu.CompilerParams(dimension_semantics=sem, vmem_limit_bytes=V7X_VMEM_LIMIT)


def _silu(x):
    return x * (1.0 / (1.0 + jnp.exp(-x)))


def _sigmoid(x):
    return 1.0 / (1.0 + jnp.exp(-x))


def _softplus(x):
    return jnp.maximum(x, 0.0) + jnp.log(1.0 + jnp.exp(-jnp.abs(x)))


def _iota(shape, dim):
    return lax.broadcasted_iota(jnp.int32, shape, dim)


def _dot(a, b):
    return jnp.dot(a.astype(BF16), b.astype(BF16), preferred_element_type=F32)


def _dot_nt(a, b):
    return lax.dot_general(a.astype(BF16), b.astype(BF16), (((1,), (1,)), ((), ())), preferred_element_type=F32)


def _dot_tn(a, b):
    return lax.dot_general(a.astype(BF16), b.astype(BF16), (((0,), (0,)), ((), ())), preferred_element_type=F32)


def _dot_hi(a, b):
    return jnp.dot(a, b, precision=HI, preferred_element_type=F32)


def _dot_nt_hi(a, b):
    return lax.dot_general(a, b, (((1,), (1,)), ((), ())), precision=HI, preferred_element_type=F32)


def _tril(n):
    return (_iota((n, n), 0) >= _iota((n, n), 1)).astype(F32)


def _eye(n):
    return (_iota((n, n), 0) == _iota((n, n), 1)).astype(F32)


def _row_blocks(b, t, target):
    if t >= target:
        return 1, target
    return min(b, target // t), t


def _mod_kernel(c_ref, w_ref, b_ref, o_ref):
    o_ref[...] = _dot(_silu(c_ref[...]), w_ref[...]) + b_ref[...]


def _mod_call(c, w, b):
    m, d = c.shape
    n = w.shape[1]
    tn = 1536
    return pl.pallas_call(
        _mod_kernel,
        grid=(n // tn,),
        in_specs=[pl.BlockSpec((m, d), lambda j: (0, 0)),
                  pl.BlockSpec((d, tn), lambda j: (0, j)),
                  pl.BlockSpec((1, tn), lambda j: (0, j))],
        out_specs=pl.BlockSpec((m, tn), lambda j: (0, j)),
        out_shape=jax.ShapeDtypeStruct((m, n), F32),
        compiler_params=_cparams(("parallel",)),
        name="adaln_mod",
    )(c, w, b.reshape(1, n))


def _prenorm_kernel(x_ref, g_ref, mod_ref, o_ref, *, sh_row, sc_row):
    x = x_ref[...]
    bb, tt, d = x.shape
    y = x * lax.rsqrt(jnp.mean(x * x, axis=-1, keepdims=True) + RMS_EPS) * g_ref[...]
    mod = mod_ref[...]
    y = y * (1.0 + mod[:, sc_row:sc_row + 1, :]) + mod[:, sh_row:sh_row + 1, :]
    o_ref[...] = y.reshape(bb * tt, d).astype(o_ref.dtype)


def _prenorm_call(x, g, mod, sh_row, sc_row, out_dtype=BF16):
    b, t, d = x.shape
    bb, tt = _row_blocks(b, t, 512)
    nt = t // tt
    return pl.pallas_call(
        functools.partial(_prenorm_kernel, sh_row=sh_row, sc_row=sc_row),
        grid=(b // bb, nt),
        in_specs=[pl.BlockSpec((bb, tt, d), lambda i, j: (i, j, 0)),
                  pl.BlockSpec((1, 1, d), lambda i, j: (0, 0, 0)),
                  pl.BlockSpec((bb, mod.shape[1], d), lambda i, j: (i, 0, 0))],
        out_specs=pl.BlockSpec((bb * tt, d), lambda i, j: (i * nt + j, 0)),
        out_shape=jax.ShapeDtypeStruct((b * t, d), out_dtype),
        compiler_params=_cparams(("parallel", "parallel")),
        name="prenorm",
    )(x, g.reshape(1, 1, d), mod)


def _matmul_kernel(x_ref, w_ref, o_ref):
    o_ref[...] = jnp.dot(x_ref[...], w_ref[...], preferred_element_type=F32)


def _matmul_call(x, w, tm=1024, tn=1024):
    m, k = x.shape
    n = w.shape[1]
    tm = min(tm, m)
    return pl.pallas_call(
        _matmul_kernel,
        grid=(n // tn, m // tm),
        in_specs=[pl.BlockSpec((tm, k), lambda j, i: (i, 0)),
                  pl.BlockSpec((k, tn), lambda j, i: (0, j))],
        out_specs=pl.BlockSpec((tm, tn), lambda j, i: (i, j)),
        out_shape=jax.ShapeDtypeStruct((m, n), F32),
        compiler_params=_cparams(("parallel", "parallel")),
        name="in_proj",
    )(x, w)


def _pack_w_in(w_in):
    d = w_in.shape[0]
    sizes = (SSD_INNER, SSD_CONV, SSD_H, HG_H * HG_K, HG_H * HG_K, HG_INNER, HG_INNER,
             MLA_QL, MLA_KVL, MLA_ROPE, LRU_W, LRU_W, N_BRANCH * D_MODEL)
    offs = [0]
    for s in sizes:
        offs.append(offs[-1] + s)
    z, xbc, dt, hq, hf, hi, hg, qd, kvd, krr, lx, ly, gl = [w_in[:, offs[i]:offs[i + 1]] for i in range(13)]
    pad = lambda a: jnp.pad(a, ((0, 0), (0, LANE - a.shape[1])))
    return jnp.concatenate([gl, z, hq, hf, hi, hg, lx, ly, xbc, qd, kvd, pad(dt), pad(krr)], axis=1).astype(BF16)


def _causal_conv(xx_ref, x, w_ref, b_ref, tt):
    xx_ref[8:8 + tt, :] = x
    w = w_ref[...]
    y = b_ref[...] + w[3:4, :] * x
    for k in range(1, CONV_W):
        y = y + w[3 - k:4 - k, :] * xx_ref[8 - k:8 - k + tt, :]
    return y


def _causal_conv_short(x, hist, w_ref, b_ref, tt):
    rows_n = x.shape[0]
    tpos = _iota(x.shape, 0) % tt
    w = w_ref[...]
    y = b_ref[...] + w[3:4, :] * x
    for k in range(1, CONV_W):
        src = jnp.where(tpos >= k, pltpu.roll(x, k, 0), pltpu.roll(hist, rows_n - tt + k, 0))
        y = y + w[3 - k:4 - k, :] * src
    return y


def _ssd_kernel(z_ref, xbc_ref, dt_ref, cw_ref, cb_ref, dtb_ref, alog_ref, dfull_ref, ng_ref, s0_ref, c0_ref,
                y_ref, s1_ref, st_ref, xx_ref, yc_ref, *, bb, tt, cl):
    t = pl.program_id(1)
    nt = pl.num_programs(1)
    rows_n = bb * tt
    hp = SSD_H // SSD_G * SSD_P
    heads = [divmod(h, SSD_H // SSD_G) for h in range(SSD_H)]

    @pl.when(t == 0)
    def _():
        for bi in range(bb):
            for h, (g, r) in enumerate(heads):
                st_ref[bi, g, :, r * SSD_P:(r + 1) * SSD_P] = s0_ref[bi, h].T

    if bb == 1:
        @pl.when(t == 0)
        def _():
            xx_ref[0:8, :] = jnp.zeros((8, SSD_CONV), F32)
            xx_ref[5:8, :] = c0_ref[0]

        conv = _causal_conv(xx_ref, xbc_ref[...], cw_ref, cb_ref, tt)
        xx_ref[0:8, :] = xx_ref[tt:tt + 8, :]
    else:
        conv = _causal_conv_short(xbc_ref[...], c0_ref[...], cw_ref, cb_ref, tt)
    act = _silu(conv)
    xs = act[:, :SSD_INNER]
    bm = act[:, SSD_INNER:SSD_INNER + SSD_G * SSD_N]
    cm = act[:, SSD_INNER + SSD_G * SSD_N:]
    dt = _softplus(dt_ref[...] + dtb_ref[...])
    da = dt * (-jnp.exp(alog_ref[...]))
    expand = (_iota((LANE, SSD_INNER), 1) // SSD_P == _iota((LANE, SSD_INNER), 0)).astype(F32)
    dtf = _dot_hi(dt, expand)
    ti, si = _iota((rows_n, rows_n), 0), _iota((rows_n, rows_n), 1)
    same_chunk = ti // cl == si // cl
    acs_all = _dot_hi((same_chunk & (ti >= si)).astype(F32), da)
    acs_t = lax.dot_general(da, (same_chunk & (ti <= si)).astype(F32), (((0,), (0,)), ((), ())),
                            precision=HI, preferred_element_type=F32)
    last_all = _dot_hi((si == ti // cl * cl + (cl - 1)).astype(F32), acs_all)
    eacs_all = jnp.exp(_dot_hi(acs_all, expand))
    dend_all = jnp.exp(_dot_hi(last_all - acs_all, expand))
    xdt_all = xs * dtf
    xsc_all = xdt_all * dend_all
    tri = _iota((cl, cl), 0) >= _iota((cl, cl), 1)

    for c in range(rows_n // cl):
        rows = slice(c * cl, (c + 1) * cl)
        bi = c * cl // tt
        eacs = eacs_all[rows]
        xdt = xdt_all[rows]
        xsc = xsc_all[rows]
        for g in range(SSD_G):
            bg = bm[rows, g * SSD_N:(g + 1) * SSD_N]
            cg = cm[rows, g * SSD_N:(g + 1) * SSD_N]
            cb = _dot_nt(cg, bg)
            sg = st_ref[bi, g]
            y_inter = _dot(cg, sg) * eacs[:, g * hp:(g + 1) * hp]
            for r in range(SSD_H // SSD_G):
                h = g * (SSD_H // SSD_G) + r
                seg = acs_all[rows, h:h + 1] - acs_t[h:h + 1, rows]
                decay = jnp.exp(jnp.where(tri, seg, NEG))
                y_h = _dot(cb * decay, xdt[:, h * SSD_P:(h + 1) * SSD_P])
                yc_ref[rows, h * SSD_P:(h + 1) * SSD_P] = y_h + y_inter[:, r * SSD_P:(r + 1) * SSD_P]
            cdec = eacs[cl - 1:cl, g * hp:(g + 1) * hp]
            st_ref[bi, g] = cdec * sg + _dot_tn(bg, xsc[:, g * hp:(g + 1) * hp])

    y = yc_ref[...] + dfull_ref[...] * xs
    yz = y * _silu(z_ref[...])
    out = yz * lax.rsqrt(jnp.mean(yz * yz, axis=-1, keepdims=True) + RMS_EPS) * ng_ref[...]
    y_ref[...] = out.astype(y_ref.dtype)

    @pl.when(t == nt - 1)
    def _():
        for bi in range(bb):
            for h, (g, r) in enumerate(heads):
                s1_ref[bi, h] = st_ref[bi, g, :, r * SSD_P:(r + 1) * SSD_P].T


def _conv_tail(p, b, t, off, width):
    return p[:, off:off + width].reshape(b, t, width)[:, t - (CONV_W - 1):, :]


def _ssd_call(p, b, t, conv_w, conv_b, dt_bias, a_log, d_skip, norm_g, s0, c0, out_dtype):
    bb, tt = (1, min(t, 256)) if t >= 256 else (min(b, 16), t)
    cl = SSD_CHUNK if tt % SSD_CHUNK == 0 else tt
    nt = t // tt
    rows_n = bb * tt
    padl = lambda v: jnp.pad(v, (0, LANE - v.shape[0])).reshape(1, LANE)
    row = lambda i, j: i * nt + j
    const2 = lambda i, j: (0, 0)
    if bb == 1:
        hist, hist_spec = c0, pl.BlockSpec((1, CONV_W - 1, SSD_CONV), lambda i, j: (i, 0, 0))
    else:
        hist = jnp.pad(c0, ((0, 0), (tt - (CONV_W - 1), 0), (0, 0))).reshape(b * tt, SSD_CONV)
        hist_spec = pl.BlockSpec((rows_n, SSD_CONV), lambda i, j: (i, 0))
    y, s1 = pl.pallas_call(
        functools.partial(_ssd_kernel, bb=bb, tt=tt, cl=cl),
        grid=(b // bb, nt),
        in_specs=[pl.BlockSpec((rows_n, SSD_INNER), lambda i, j: (row(i, j), P_Z // SSD_INNER)),
                  pl.BlockSpec((rows_n, SSD_CONV), lambda i, j: (row(i, j), P_XBC // SSD_CONV)),
                  pl.BlockSpec((rows_n, LANE), lambda i, j: (row(i, j), P_DT // LANE)),
                  pl.BlockSpec((CONV_W, SSD_CONV), const2),
                  pl.BlockSpec((1, SSD_CONV), const2),
                  pl.BlockSpec((1, LANE), const2),
                  pl.BlockSpec((1, LANE), const2),
                  pl.BlockSpec((1, SSD_INNER), const2),
                  pl.BlockSpec((1, SSD_INNER), const2),
                  pl.BlockSpec((bb, SSD_H, SSD_P, SSD_N), lambda i, j: (i, 0, 0, 0)),
                  hist_spec],
        out_specs=[pl.BlockSpec((rows_n, SSD_INNER), lambda i, j: (row(i, j), 0)),
                   pl.BlockSpec((bb, SSD_H, SSD_P, SSD_N), lambda i, j: (i, 0, 0, 0))],
        out_shape=[jax.ShapeDtypeStruct((b * t, SSD_INNER), out_dtype),
                   jax.ShapeDtypeStruct((b, SSD_H, SSD_P, SSD_N), F32)],
        scratch_shapes=[pltpu.VMEM((bb, SSD_G, SSD_N, SSD_H // SSD_G * SSD_P), F32),
                        pltpu.VMEM((tt + 8, SSD_CONV), F32),
                        pltpu.VMEM((rows_n, SSD_INNER), F32)],
        compiler_params=_cparams(("parallel", "arbitrary")),
        name="ssd",
    )(p, p, p, conv_w, conv_b.reshape(1, -1), padl(dt_bias), padl(a_log),
      jnp.repeat(d_skip, SSD_P).reshape(1, -1), norm_g.reshape(1, -1), s0, hist)
    return y, s1, _conv_tail(p, b, t, P_XBC, SSD_CONV)


def _hgrn_kernel(q_ref, f_ref, i_ref, g_ref, loglb_ref, log1m_ref, onem_ref, ng_ref, s0_ref,
                 y_ref, s1_ref, st_ref, o_ref, *, bb, tt, cl):
    t = pl.program_id(1)
    nt = pl.num_programs(1)
    rows_n = bb * tt
    hs = [slice(h * HG_K, (h + 1) * HG_K) for h in range(HG_H)]

    @pl.when(t == 0)
    def _():
        for bi in range(bb):
            for h in range(HG_H):
                st_ref[bi, h] = s0_ref[bi, h].T

    q, hf, v = q_ref[...], f_ref[...], i_ref[...]
    b_ = log1m_ref[...] - _softplus(-hf)
    loglb = loglb_ref[...]
    logf = jnp.maximum(loglb, b_) + jnp.log(1.0 + jnp.exp(-jnp.abs(loglb - b_)))
    kin = onem_ref[...] * _sigmoid(-hf)
    ti, si = _iota((rows_n, rows_n), 0), _iota((rows_n, rows_n), 1)
    bc = _dot_hi(((ti // cl == si // cl) & (ti >= si)).astype(F32), logf)
    bl = _dot_hi((si == ti // cl * cl + (cl - 1)).astype(F32), bc)
    in_chunk = _iota((rows_n, HG_INNER), 0) % cl
    p = q * kin
    o = [jnp.sum(p[:, s], axis=-1, keepdims=True) * v[:, s] for s in hs]
    for d in range(1, cl):
        kd, bd, vd = pltpu.roll(kin, d, 0), pltpu.roll(bc, d, 0), pltpu.roll(v, d, 0)
        p = q * kd * jnp.exp(jnp.where(in_chunk >= d, bc - bd, NEG))
        o = [o[h] + jnp.sum(p[:, s], axis=-1, keepdims=True) * vd[:, s] for h, s in enumerate(hs)]
    for h in range(HG_H):
        o_ref[:, hs[h]] = o[h]
    qe = q * jnp.exp(bc)
    ke = kin * jnp.exp(bl - bc)
    dec = jnp.exp(bl)
    for c in range(rows_n // cl):
        rows = slice(c * cl, (c + 1) * cl)
        bi = c * cl // tt
        for h, s in enumerate(hs):
            st = st_ref[bi, h]
            o_ref[rows, s] += _dot_nt(qe[rows, s], st)
            st_ref[bi, h] = dec[c * cl:c * cl + 1, s] * st + _dot_tn(v[rows, s], ke[rows, s])
    ng = ng_ref[...]
    outs = []
    for s in hs:
        oh = o_ref[:, s]
        outs.append(oh * lax.rsqrt(jnp.mean(oh * oh, axis=-1, keepdims=True) + RMS_EPS) * ng)
    y_ref[...] = (jnp.concatenate(outs, axis=-1) * _silu(g_ref[...])).astype(y_ref.dtype)

    @pl.when(t == nt - 1)
    def _():
        for bi in range(bb):
            for h in range(HG_H):
                s1_ref[bi, h] = st_ref[bi, h].T


def _hgrn_call(p, b, t, lb, norm_g, s0, out_dtype):
    bb, tt = _row_blocks(b, t, 256) if t >= 256 else (min(b, 16), t)
    cl = HG_CHUNK if tt % HG_CHUNK == 0 else tt
    nt = t // tt
    row = lambda i, j: i * nt + j
    const2 = lambda i, j: (0, 0)
    col = lambda off: (lambda i, j: (row(i, j), off // HG_INNER))
    vec = pl.BlockSpec((1, HG_INNER), const2)
    blk = pl.BlockSpec((bb * tt, HG_INNER), lambda i, j: (row(i, j), 0))
    return pl.pallas_call(
        functools.partial(_hgrn_kernel, bb=bb, tt=tt, cl=cl),
        grid=(b // bb, nt),
        in_specs=[pl.BlockSpec((bb * tt, HG_INNER), col(P_HQ)), pl.BlockSpec((bb * tt, HG_INNER), col(P_HF)),
                  pl.BlockSpec((bb * tt, HG_INNER), col(P_HI)), pl.BlockSpec((bb * tt, HG_INNER), col(P_HG)),
                  vec, vec, vec, pl.BlockSpec((1, HG_V), const2),
                  pl.BlockSpec((bb, HG_H, HG_K, HG_V), lambda i, j: (i, 0, 0, 0))],
        out_specs=[blk, pl.BlockSpec((bb, HG_H, HG_K, HG_V), lambda i, j: (i, 0, 0, 0))],
        out_shape=[jax.ShapeDtypeStruct((b * t, HG_INNER), out_dtype),
                   jax.ShapeDtypeStruct((b, HG_H, HG_K, HG_V), F32)],
        scratch_shapes=[pltpu.VMEM((bb, HG_H, HG_V, HG_K), F32), pltpu.VMEM((bb * tt, HG_INNER), F32)],
        compiler_params=_cparams(("parallel", "arbitrary")),
        name="hgrn2",
    )(p, p, p, p, jnp.log(lb).reshape(1, -1), jnp.log1p(-lb).reshape(1, -1), (1.0 - lb).reshape(1, -1),
      norm_g.reshape(1, -1), s0)


def _gelu_tanh(x):
    return 0.5 * x * (1.0 + jnp.tanh(math.sqrt(2.0 / math.pi) * (x + 0.044715 * (x * x * x))))


def _lru_kernel(lx_ref, ly_ref, cw_ref, cb_ref, wr_ref, br_ref, wi_ref, bi_ref, ap_ref, h0_ref, c0_ref,
                y_ref, h1_ref, carry_ref, xx_ref, hs_ref, *, bb, tt, pos0):
    t = pl.program_id(1)
    nt = pl.num_programs(1)
    rows_n = bb * tt
    groups = tt // 8

    if bb == 1:
        @pl.when(t == 0)
        def _():
            xx_ref[0:8, :] = jnp.zeros((8, LRU_W), F32)
            xx_ref[5:8, :] = c0_ref[0]
            carry_ref[...] = h0_ref[0]

        xl = _causal_conv(xx_ref, lx_ref[...], cw_ref, cb_ref, tt)
        xx_ref[0:8, :] = xx_ref[tt:tt + 8, :]
    else:
        xl = _causal_conv_short(lx_ref[...], c0_ref[...], cw_ref, cb_ref, tt)
    r = _sigmoid(_dot(xl, wr_ref[...]) + br_ref[...])
    ig = _sigmoid(_dot(xl, wi_ref[...]) + bi_ref[...])
    log_a = -LRU_C * r * _softplus(-ap_ref[...])
    a = jnp.exp(log_a)
    mult = jnp.sqrt(jnp.tanh(-log_a) * (a * a + 1.0))
    pos = pos0 + t * tt + _iota((rows_n, LRU_W), 0) % tt
    u = jnp.where(pos == 0, 1.0, mult) * (ig * xl)
    row8 = _iota((8, LRU_W), 0)
    carry = carry_ref[...]
    for gi in range(rows_n // 8):
        bi, gj = divmod(gi, groups)
        if bb > 1 and gj == 0:
            carry = h0_ref[bi]
        a8, u8 = a[gi * 8:(gi + 1) * 8], u[gi * 8:(gi + 1) * 8]
        for k in (1, 2, 4):
            m = row8 >= k
            u8, a8 = (jnp.where(m, a8 * pltpu.roll(u8, k, 0) + u8, u8),
                      jnp.where(m, a8 * pltpu.roll(a8, k, 0), a8))
        h8 = u8 + a8 * carry
        carry = h8[7:8, :]
        hs_ref[gi * 8:(gi + 1) * 8, :] = h8
        if bb > 1 and gj == groups - 1:
            h1_ref[bi] = carry
    y_ref[...] = (hs_ref[...] * _gelu_tanh(ly_ref[...])).astype(y_ref.dtype)

    if bb == 1:
        carry_ref[...] = carry

        @pl.when(t == nt - 1)
        def _():
            h1_ref[0] = carry


def _block_diag(w):
    nb, di, do = w.shape
    eye = jnp.eye(nb, dtype=w.dtype)
    return (eye[:, None, :, None] * w[:, :, None, :]).reshape(nb * di, nb * do)


def _lru_call(p, b, t, pos0, conv_w, conv_b, w_r, b_r, w_i, b_i, a_param, h0, c0, out_dtype):
    bb, tt = (1, min(t, 256)) if t >= 256 else (min(b, 16), t)
    nt = t // tt
    rows_n = bb * tt
    row = lambda i, j: i * nt + j
    const2 = lambda i, j: (0, 0)
    vec = pl.BlockSpec((1, LRU_W), const2)
    mat = pl.BlockSpec((LRU_W, LRU_W), const2)
    if bb == 1:
        hist, hist_spec = c0, pl.BlockSpec((1, CONV_W - 1, LRU_W), lambda i, j: (i, 0, 0))
    else:
        hist = jnp.pad(c0, ((0, 0), (tt - (CONV_W - 1), 0), (0, 0))).reshape(b * tt, LRU_W)
        hist_spec = pl.BlockSpec((rows_n, LRU_W), lambda i, j: (i, 0))
    y, h1 = pl.pallas_call(
        functools.partial(_lru_kernel, bb=bb, tt=tt, pos0=pos0),
        grid=(b // bb, nt),
        in_specs=[pl.BlockSpec((rows_n, LRU_W), lambda i, j: (row(i, j), P_LX // LRU_W)),
                  pl.BlockSpec((rows_n, LRU_W), lambda i, j: (row(i, j), P_LY // LRU_W)),
                  pl.BlockSpec((CONV_W, LRU_W), const2), vec, mat, vec, mat, vec, vec,
                  pl.BlockSpec((bb, 1, LRU_W), lambda i, j: (i, 0, 0)),
                  hist_spec],
        out_specs=[pl.BlockSpec((rows_n, LRU_W), lambda i, j: (row(i, j), 0)),
                   pl.BlockSpec((bb, 1, LRU_W), lambda i, j: (i, 0, 0))],
        out_shape=[jax.ShapeDtypeStruct((b * t, LRU_W), out_dtype),
                   jax.ShapeDtypeStruct((b, 1, LRU_W), F32)],
        scratch_shapes=[pltpu.VMEM((1, LRU_W), F32),
                        pltpu.VMEM((tt + 8, LRU_W), F32),
                        pltpu.VMEM((rows_n, LRU_W), F32)],
        compiler_params=_cparams(("parallel", "arbitrary")),
        name="rglru",
    )(p, p, conv_w, conv_b.reshape(1, -1), _block_diag(w_r).astype(BF16), b_r.reshape(1, -1),
      _block_diag(w_i).astype(BF16), b_i.reshape(1, -1), a_param.reshape(1, -1),
      h0.reshape(b, 1, LRU_W), hist)
    return y, h1.reshape(b, LRU_W), _conv_tail(p, b, t, P_LX, LRU_W)


MLA_SCALE = (MLA_NOPE + MLA_ROPE) ** -0.5
ROPE_HALF = MLA_ROPE // 2


def _rope_rotate(x, cos, sin):
    lane = _iota(x.shape, 1)
    rot = jnp.where(lane % MLA_ROPE < ROPE_HALF, -pltpu.roll(x, LANE - ROPE_HALF, 1), pltpu.roll(x, ROPE_HALF, 1))
    return x * cos + rot * sin


MLA_QK = MLA_KVL + LANE

def _mla_prep_kernel(qd_ref, kvd_ref, krr_ref, cos_ref, sin_ref, qg_ref, wuq_ref, kvg_ref, wukt_ref,
                     qcat_ref, ckv_ref, krope_ref, kcat_ref):
    qd = qd_ref[...]
    qn = qd * lax.rsqrt(jnp.mean(qd * qd, axis=-1, keepdims=True) + RMS_EPS) * qg_ref[...]
    q = _dot(qn, wuq_ref[...])
    cos, sin = cos_ref[...], sin_ref[...]
    nn = MLA_H * MLA_NOPE
    lane = _iota((q.shape[0], LANE), 1)
    halves = [_rope_rotate(q[:, nn + i * LANE:nn + (i + 1) * LANE], cos, sin) for i in range(2)]
    per_half = LANE // MLA_ROPE
    for h in range(MLA_H):
        qa = _dot(q[:, h * MLA_NOPE:(h + 1) * MLA_NOPE], wukt_ref[h]) * MLA_SCALE
        qcat_ref[h, :, :MLA_KVL] = qa.astype(qcat_ref.dtype)
        half, sh = halves[h // per_half], (h % per_half) * MLA_ROPE
        piece = pltpu.roll(half, LANE - sh, 1) if sh else half
        qcat_ref[h, :, MLA_KVL:] = jnp.where(lane < MLA_ROPE, piece * MLA_SCALE, 0.0).astype(qcat_ref.dtype)
    kvd = kvd_ref[...]
    ckv = kvd * lax.rsqrt(jnp.mean(kvd * kvd, axis=-1, keepdims=True) + RMS_EPS) * kvg_ref[...]
    ckv_ref[...] = ckv
    kr = jnp.where(lane < MLA_ROPE, _rope_rotate(krr_ref[...], cos, sin), 0.0)
    krope_ref[...] = kr[:, :MLA_ROPE]
    kcat_ref[:, :MLA_KVL] = ckv.astype(BF16)
    kcat_ref[:, MLA_KVL:] = kr.astype(BF16)


def _lane_wide(v, n):
    if n == LANE:
        return v
    return jnp.concatenate([v] * (n // LANE), axis=1) if n % LANE == 0 else v[:, :n]


def _softmax_step(s, vals, m_ref, l_ref, acc_ref):
    m_old = m_ref[...]
    m_new = jnp.maximum(m_old, jnp.max(s, axis=-1, keepdims=True))
    pr = jnp.exp(s - _lane_wide(m_new, s.shape[1]))
    alpha = jnp.exp(m_old - m_new)
    l_ref[...] = alpha * l_ref[...] + jnp.sum(pr, axis=-1, keepdims=True)
    acc_ref[...] = _lane_wide(alpha, acc_ref.shape[1]) * acc_ref[...] + _dot(pr, vals)
    m_ref[...] = m_new


def _softmax_init(m_ref, l_ref, acc_ref):
    m_ref[...] = jnp.full(m_ref.shape, NEG, F32)
    l_ref[...] = jnp.zeros(l_ref.shape, F32)
    acc_ref[...] = jnp.zeros(acc_ref.shape, F32)


def _mla_attn_prompt_kernel(q_ref, k_ref, wuv_ref, y_ref, m_ref, l_ref, acc_ref, *, tq, tk):
    i, j = pl.program_id(1), pl.program_id(2)
    rows = MLA_H * tq
    last = (i * tq + tq - 1) // tk

    @pl.when(j == 0)
    def _():
        _softmax_init(m_ref, l_ref, acc_ref)

    def step(diagonal):
        k = k_ref[...]
        s = _dot_nt(q_ref[...].reshape(rows, MLA_QK), k)
        if diagonal:
            s = jnp.where(j * tk + _iota((rows, tk), 1) <= i * tq + _iota((rows, tk), 0) % tq, s, NEG)
        _softmax_step(s, k[:, :MLA_KVL], m_ref, l_ref, acc_ref)

    @pl.when(j < last)
    def _():
        step(False)

    @pl.when(j == last)
    def _():
        step(True)
        o = acc_ref[...] / _lane_wide(l_ref[...], MLA_KVL)
        for h in range(MLA_H):
            y_ref[:, h * MLA_V:(h + 1) * MLA_V] = _dot(o[h * tq:(h + 1) * tq], wuv_ref[h]).astype(y_ref.dtype)


def _mla_attn_sample_kernel(pt_ref, cache_c_ref, cache_rt_ref, q_ref, ckv_ref, krope_ref, wuv_ref, y_ref,
                            kc_buf, krt_buf, sem, *, layer, n_pages, page, tq):
    b = pl.program_id(0)
    nb = pl.num_programs(0)
    rows = MLA_H * tq

    def page_copies(seq, slot, pg):
        pid = pt_ref[seq, pg]
        return (pltpu.make_async_copy(cache_c_ref.at[layer, pid], kc_buf.at[slot, pl.ds(pg * page, page), :],
                                      sem.at[0, slot]),
                pltpu.make_async_copy(cache_rt_ref.at[layer, pid], krt_buf.at[slot, :, pl.ds(pg * page, page)],
                                      sem.at[1, slot]))

    def start_gather(seq, slot):
        def body(pg, carry):
            for cp in page_copies(seq, slot, pg):
                cp.start()
            return carry
        lax.fori_loop(0, n_pages, body, 0)

    def wait_gather(seq, slot):
        def body(pg, carry):
            for cp in page_copies(seq, slot, pg):
                cp.wait()
            return carry
        lax.fori_loop(0, n_pages, body, 0)

    slot = b % 2

    @pl.when(b == 0)
    def _():
        start_gather(0, 0)

    @pl.when(b + 1 < nb)
    def _():
        start_gather(b + 1, 1 - slot)

    wait_gather(b, slot)

    qcat = q_ref[...].reshape(rows, MLA_QK)
    q, qr = qcat[:, :MLA_KVL], qcat[:, MLA_KVL:MLA_KVL + MLA_ROPE]
    kc = kc_buf[slot].astype(BF16)
    s_old = _dot_nt(q, kc) + _dot(qr, krt_buf[slot])
    kc_new, kr_new = ckv_ref[...], krope_ref[...]
    s_new = _dot_nt(q, kc_new) + _dot_nt(qr, kr_new)
    s_new = jnp.where(_iota((rows, tq), 1) <= _iota((rows, tq), 0) % tq, s_new, NEG)
    m = jnp.maximum(jnp.max(s_old, axis=-1, keepdims=True), jnp.max(s_new, axis=-1, keepdims=True))
    p_old, p_new = jnp.exp(s_old - m), jnp.exp(s_new - m)
    denom = jnp.sum(p_old, axis=-1, keepdims=True) + jnp.sum(p_new, axis=-1, keepdims=True)
    o = (_dot(p_old, kc) + _dot(p_new, kc_new)) / denom
    for h in range(MLA_H):
        y_ref[:, h * MLA_V:(h + 1) * MLA_V] = _dot(o[h * tq:(h + 1) * tq], wuv_ref[h]).astype(y_ref.dtype)


def _rope_tables(pos):
    inv = ROPE_THETA ** (-jnp.arange(ROPE_HALF, dtype=F32) * 2.0 / MLA_ROPE)
    ang = pos.astype(F32)[:, None] * inv[None, :]
    reps = LANE // ROPE_HALF
    return jnp.tile(jnp.cos(ang), (1, reps)), jnp.tile(jnp.sin(ang), (1, reps))


def _mla_call(p, b, t, pos0, q_norm_g, w_uq, kv_norm_g, w_ukv, cache_c, cache_r, page_table, layer, out_dtype):
    m = b * t
    tm = min(m, 256)
    wq = w_uq.reshape(MLA_QL, MLA_H, MLA_NOPE + MLA_ROPE)
    wq = jnp.concatenate([wq[..., :MLA_NOPE].reshape(MLA_QL, -1), wq[..., MLA_NOPE:].reshape(MLA_QL, -1)], axis=1)
    wkv = w_ukv.reshape(MLA_KVL, MLA_H, MLA_NOPE + MLA_V)
    wukt = jnp.transpose(wkv[..., :MLA_NOPE], (1, 2, 0)).astype(BF16)
    wuv = jnp.transpose(wkv[..., MLA_NOPE:], (1, 0, 2)).astype(BF16)
    cos, sin = _rope_tables(pos0 + jnp.arange(t))
    if t < tm:
        cos, sin = jnp.tile(cos, (tm // t, 1)), jnp.tile(sin, (tm // t, 1))
    ntab = cos.shape[0] // tm
    qdt = BF16 if t >= tm else F32
    const2 = lambda i: (0, 0)
    qcat, ckv, krope, kcat = pl.pallas_call(
        _mla_prep_kernel,
        grid=(m // tm,),
        in_specs=[pl.BlockSpec((tm, MLA_QL), lambda i: (i, P_QD // MLA_QL)),
                  pl.BlockSpec((tm, MLA_KVL), lambda i: (i, P_KVD // MLA_KVL)),
                  pl.BlockSpec((tm, LANE), lambda i: (i, P_KR // LANE)),
                  pl.BlockSpec((tm, LANE), lambda i: (i % ntab, 0)),
                  pl.BlockSpec((tm, LANE), lambda i: (i % ntab, 0)),
                  pl.BlockSpec((1, MLA_QL), const2),
                  pl.BlockSpec((MLA_QL, MLA_H * (MLA_NOPE + MLA_ROPE)), const2),
                  pl.BlockSpec((1, MLA_KVL), const2),
                  pl.BlockSpec((MLA_H, MLA_NOPE, MLA_KVL), lambda i: (0, 0, 0))],
        out_specs=[pl.BlockSpec((MLA_H, tm, MLA_QK), lambda i: (0, i, 0)),
                   pl.BlockSpec((tm, MLA_KVL), lambda i: (i, 0)),
                   pl.BlockSpec((tm, MLA_ROPE), lambda i: (i, 0)),
                   pl.BlockSpec((tm, MLA_QK), lambda i: (i, 0))],
        out_shape=[jax.ShapeDtypeStruct((MLA_H, m, MLA_QK), qdt),
                   jax.ShapeDtypeStruct((m, MLA_KVL), F32),
                   jax.ShapeDtypeStruct((m, MLA_ROPE), F32),
                   jax.ShapeDtypeStruct((m, MLA_QK), BF16)],
        compiler_params=_cparams(("parallel",)),
        name="mla_prep",
    )(p, p, p, cos, sin, q_norm_g.reshape(1, -1), wq.astype(BF16), kv_norm_g.reshape(1, -1), wukt)
    rows = MLA_H * min(t, 256)
    softmax_scratch = [pltpu.VMEM((rows, LANE), F32), pltpu.VMEM((rows, LANE), F32), pltpu.VMEM((rows, MLA_KVL), F32)]

    if cache_c is None:
        tq = min(t, 256)
        tk = min(t, 512)
        nq, nk = t // tq, t // tk
        last = lambda i: (i * tq + tq - 1) // tk
        y = pl.pallas_call(
            functools.partial(_mla_attn_prompt_kernel, tq=tq, tk=tk),
            grid=(b, nq, nk),
            in_specs=[pl.BlockSpec((MLA_H, tq, MLA_QK), lambda bi, i, j: (0, bi * nq + i, 0)),
                      pl.BlockSpec((tk, MLA_QK), lambda bi, i, j: (bi * nk + jnp.minimum(j, last(i)), 0)),
                      pl.BlockSpec((MLA_H, MLA_KVL, MLA_V), lambda bi, i, j: (0, 0, 0))],
            out_specs=pl.BlockSpec((tq, MLA_H * MLA_V), lambda bi, i, j: (bi * nq + i, 0)),
            out_shape=jax.ShapeDtypeStruct((m, MLA_H * MLA_V), out_dtype),
            scratch_shapes=softmax_scratch,
            compiler_params=_cparams(("parallel", "parallel", "arbitrary")),
            name="mla_attn_prompt",
        )(qcat, kcat, wuv)
        return y, ckv, krope

    n_pages = page_table.shape[1]
    page = cache_c.shape[2]
    cache_rt = jnp.swapaxes(cache_r, 2, 3)
    y = pl.pallas_call(
        functools.partial(_mla_attn_sample_kernel, layer=layer, n_pages=n_pages, page=page, tq=t),
        grid_spec=pltpu.PrefetchScalarGridSpec(
            num_scalar_prefetch=1,
            grid=(b,),
            in_specs=[pl.BlockSpec(memory_space=pl.ANY), pl.BlockSpec(memory_space=pl.ANY),
                      pl.BlockSpec((MLA_H, t, MLA_QK), lambda bi, pt: (0, bi, 0)),
                      pl.BlockSpec((t, MLA_KVL), lambda bi, pt: (bi, 0)),
                      pl.BlockSpec((t, MLA_ROPE), lambda bi, pt: (bi, 0)),
                      pl.BlockSpec((MLA_H, MLA_KVL, MLA_V), lambda bi, pt: (0, 0, 0))],
            out_specs=pl.BlockSpec((t, MLA_H * MLA_V), lambda bi, pt: (bi, 0)),
            scratch_shapes=[pltpu.VMEM((2, n_pages * page, MLA_KVL), F32),
                            pltpu.VMEM((2, MLA_ROPE, n_pages * page), F32),
                            pltpu.SemaphoreType.DMA((2, 2))]),
        out_shape=jax.ShapeDtypeStruct((m, MLA_H * MLA_V), out_dtype),
        compiler_params=_cparams(("arbitrary",)),
        name="mla_attn_sample",
    )(page_table, cache_c, cache_rt, qcat, ckv, krope, wuv)
    return y, ckv, krope


ROUTER_GRP_LANE = MOE_E


def _merge_kernel(gl_ref, ya_ref, yb_ref, yc_ref, yd_ref, wb_ref, wo_ref, x_ref, mod_ref, ng_ref, wr_ref, br_ref,
                  xn_ref, h2_ref, lg_ref):
    bb, tt, d = x_ref.shape
    merged = None
    for n, y_ref in enumerate((ya_ref, yb_ref, yc_ref, yd_ref)):
        term = _sigmoid(gl_ref[:, n * d:(n + 1) * d]) * _dot(y_ref[...], wb_ref[n])
        merged = term if merged is None else merged + term
    out = _dot(merged, wo_ref[...])
    mod = mod_ref[...]
    x = x_ref[...] + mod[:, 2:3, :] * out.reshape(bb, tt, d)
    xn_ref[...] = x
    h = x * lax.rsqrt(jnp.mean(x * x, axis=-1, keepdims=True) + RMS_EPS) * ng_ref[...]
    h = (h * (1.0 + mod[:, 4:5, :]) + mod[:, 3:4, :]).reshape(bb * tt, d)
    h2_ref[...] = h.astype(h2_ref.dtype)
    lg_ref[...] = _dot_hi(h, wr_ref[...]) + br_ref[...]


def _merge_call(p, ys, w_branch, w_out, x, mod, norm2_g, w_router, b_router):
    b, t, d = x.shape
    bb, tt = _row_blocks(b, t, 512)
    nt = t // tt
    tm = bb * tt
    row = lambda i, j: (i * nt + j, 0)
    yspec = pl.BlockSpec((tm, ys[0].shape[1]), row)
    return pl.pallas_call(
        _merge_kernel,
        grid=(b // bb, nt),
        in_specs=[pl.BlockSpec((tm, N_BRANCH * d), row), yspec, yspec, yspec, yspec,
                  pl.BlockSpec(w_branch.shape, lambda i, j: (0, 0, 0)),
                  pl.BlockSpec(w_out.shape, lambda i, j: (0, 0)),
                  pl.BlockSpec((bb, tt, d), lambda i, j: (i, j, 0)),
                  pl.BlockSpec((bb, mod.shape[1], d), lambda i, j: (i, 0, 0)),
                  pl.BlockSpec((1, 1, d), lambda i, j: (0, 0, 0)),
                  pl.BlockSpec((d, LANE), lambda i, j: (0, 0)),
                  pl.BlockSpec((1, LANE), lambda i, j: (0, 0))],
        out_specs=[pl.BlockSpec((bb, tt, d), lambda i, j: (i, j, 0)),
                   pl.BlockSpec((tm, d), row),
                   pl.BlockSpec((tm, LANE), row)],
        out_shape=[jax.ShapeDtypeStruct((b, t, d), F32),
                   jax.ShapeDtypeStruct((b * t, d), BF16),
                   jax.ShapeDtypeStruct((b * t, LANE), F32)],
        compiler_params=_cparams(("parallel", "parallel")),
        name="merge",
    )(p, *ys, w_branch, w_out, x, mod, norm2_g.reshape(1, 1, d), w_router, b_router)


def _route(logits):
    lane = _iota(logits.shape, 1)
    big = jnp.int32(1 << 20)
    grp = jnp.where(jnp.right_shift(lane, 2) == ROUTER_GRP_LANE // MOE_G, logits, NEG)
    gmax = jnp.max(grp, axis=-1, keepdims=True)
    g_top = 1.0 / jnp.sum(jnp.exp(grp - gmax), axis=-1, keepdims=True)
    gidx = jnp.min(jnp.where(grp == gmax, lane, big), axis=-1, keepdims=True) - ROUTER_GRP_LANE
    el = jnp.where(jnp.right_shift(lane, 3) == gidx, logits, NEG)
    m1 = jnp.max(el, axis=-1, keepdims=True)
    i1 = jnp.min(jnp.where(el == m1, lane, big), axis=-1, keepdims=True)
    el2 = jnp.where(lane == i1, NEG, el)
    m2 = jnp.max(el2, axis=-1, keepdims=True)
    i2 = jnp.min(jnp.where(el2 == m2, lane, big), axis=-1, keepdims=True)
    e2 = jnp.exp(m2 - m1)
    w1 = g_top / (1.0 + e2)
    return jnp.where(lane == i1, w1, 0.0) + jnp.where(lane == i2, w1 * e2, 0.0)


def _moe_dense_kernel(h_ref, lg_ref, wg_ref, wu_ref, wd_ref, x_ref, mod_ref, o_ref, comb_ref, acc_ref):
    e = pl.program_id(1)
    ne = pl.num_programs(1)
    bb, tt, d = x_ref.shape

    @pl.when(e == 0)
    def _():
        comb_ref[...] = _route(lg_ref[...])
        acc_ref[...] = jnp.zeros(acc_ref.shape, F32)

    comb = comb_ref[...]
    ce = jnp.sum(jnp.where(_iota(comb.shape, 1) == e, comb, 0.0), axis=-1, keepdims=True)
    h = h_ref[...]
    hid = _silu(_dot(h, wg_ref[...])) * _dot(h, wu_ref[...]) * ce
    acc_ref[...] += _dot(hid, wd_ref[...])

    @pl.when(e == ne - 1)
    def _():
        o_ref[...] = x_ref[...] + mod_ref[...][:, 5:6, :] * acc_ref[...].reshape(bb, tt, d)


def _moe_call(h2, logits, w_gate, w_up, w_down, x, mod):
    b, t, d = x.shape
    bb, tt = _row_blocks(b, t, 1024)
    nt = t // tt
    tm = bb * tt
    ne, _, hid = w_gate.shape
    row = lambda i, e: (i, 0)
    xmap = lambda i, e: (i // nt, i % nt, 0)
    return pl.pallas_call(
        _moe_dense_kernel,
        grid=(b * t // tm, ne),
        in_specs=[pl.BlockSpec((tm, d), row), pl.BlockSpec((tm, LANE), row),
                  pl.BlockSpec((None, d, hid), lambda i, e: (e, 0, 0)),
                  pl.BlockSpec((None, d, hid), lambda i, e: (e, 0, 0)),
                  pl.BlockSpec((None, hid, d), lambda i, e: (e, 0, 0)),
                  pl.BlockSpec((bb, tt, d), xmap),
                  pl.BlockSpec((bb, mod.shape[1], d), lambda i, e: (i // nt, 0, 0))],
        out_specs=pl.BlockSpec((bb, tt, d), xmap),
        out_shape=jax.ShapeDtypeStruct((b, t, d), F32),
        scratch_shapes=[pltpu.VMEM((tm, LANE), F32), pltpu.VMEM((tm, d), F32)],
        compiler_params=_cparams(("parallel", "arbitrary")),
        name="moe",
    )(h2, logits, w_gate, w_up, w_down, x, mod)


MOE_TM = 1024
MOE_RB = 64
MOE_SLOTS = 4096
MOE_SC = 512
MOE_EPS = 4
META_S1, META_S2, META_W1, META_W2 = 0, 1, 2, 3


def _moe_route_kernel(lg_ref, meta_ref, metat_ref, nblk_ref, boff_ref):
    comb = _route(lg_ref[...])
    tm = comb.shape[0]
    lane = _iota(comb.shape, 1)
    big = jnp.int32(1 << 20)
    chosen = comb > 0.0
    a = jnp.where(chosen, 1.0, 0.0)
    rank = _dot((_iota((tm, tm), 0) > _iota((tm, tm), 1)).astype(F32), a)
    cnt = jnp.sum(a, axis=0, keepdims=True)
    nblk = jnp.floor((cnt + (MOE_RB - 1)) * (1.0 / MOE_RB))
    upper = (_iota((LANE, LANE), 0) < _iota((LANE, LANE), 1)).astype(F32)
    boff = _dot_hi(jnp.broadcast_to(nblk, (8, LANE)), upper)[0:1]
    slot = boff * MOE_RB + rank
    i1 = jnp.min(jnp.where(chosen, lane, big), axis=-1, keepdims=True)
    i2 = jnp.max(jnp.where(chosen, lane, -1), axis=-1, keepdims=True)
    pick = lambda v, i: jnp.sum(jnp.where(lane == i, v, 0.0), axis=-1, keepdims=True)
    two = i2 != i1
    s1, w1 = pick(slot, i1), pick(comb, i1)
    s2, w2 = jnp.where(two, pick(slot, i2), -1.0), jnp.where(two, pick(comb, i2), 0.0)
    meta = (jnp.where(lane == META_S1, s1, 0.0) + jnp.where(lane == META_S2, s2, 0.0)
            + jnp.where(lane == META_W1, w1, 0.0) + jnp.where(lane == META_W2, w2, 0.0))
    meta_ref[...] = meta
    metat_ref[...] = meta.T[0:8, :]
    nblk_ref[0] = nblk.astype(jnp.int32)
    boff_ref[0] = boff.astype(jnp.int32)


def _moe_sorted_kernel(nblk_ref, boff_ref, h_ref, meta_ref, metat_ref, wg_ref, wu_ref, wd_ref, x_ref, mod_ref,
                       o_ref, xs_ref, ys_ref):
    i, eg = pl.program_id(0), pl.program_id(1)
    neg = pl.num_programs(1)
    eps = wg_ref.shape[0]
    bb, tt, d = x_ref.shape
    tm = bb * tt
    last = i * LANE + neg * eps - 1
    used = (boff_ref[last] + nblk_ref[last]) * MOE_RB

    @pl.when(eg == 0)
    def _():
        s1 = metat_ref[META_S1:META_S1 + 1, :]
        s2 = metat_ref[META_S2:META_S2 + 1, :]
        h = h_ref[...]
        for c in range(MOE_SLOTS // MOE_SC):
            @pl.when(c * MOE_SC < used)
            def _():
                sl = (c * MOE_SC + _iota((MOE_SC, tm), 0)).astype(F32)
                perm = jnp.where((sl == s1) | (sl == s2), 1.0, 0.0)
                xs_ref[c * MOE_SC:(c + 1) * MOE_SC, :] = _dot(perm, h).astype(BF16)
        ys_ref[...] = jnp.zeros(ys_ref.shape, BF16)

    def expert_rows(j, first_block, n_rows):
        rows = pl.ds(pl.multiple_of(first_block * MOE_RB, MOE_RB), n_rows)
        xb = xs_ref[rows, :]
        hid = _silu(_dot(xb, wg_ref[j])) * _dot(xb, wu_ref[j])
        ys_ref[rows, :] = _dot(hid, wd_ref[j]).astype(BF16)

    for j in range(eps):
        e = i * LANE + eg * eps + j
        b0, nb = boff_ref[e], nblk_ref[e]

        def pair(k, carry, j=j, b0=b0):
            expert_rows(j, b0 + 2 * k, 2 * MOE_RB)
            return carry

        lax.fori_loop(0, lax.shift_right_logical(nb, 1), pair, 0)

        @pl.when(lax.bitwise_and(nb, 1) == 1)
        def _(j=j, b0=b0, nb=nb):
            expert_rows(j, b0 + nb - 1, MOE_RB)

    @pl.when(eg == neg - 1)
    def _():
        meta = meta_ref[...]
        s1, s2 = meta[:, META_S1:META_S1 + 1], meta[:, META_S2:META_S2 + 1]
        w1, w2 = meta[:, META_W1:META_W1 + 1], meta[:, META_W2:META_W2 + 1]
        o_ref[...] = jnp.zeros(o_ref.shape, F32)
        for c in range(MOE_SLOTS // MOE_SC):
            @pl.when(c * MOE_SC < used)
            def _():
                sl = (c * MOE_SC + _iota((tm, MOE_SC), 1)).astype(F32)
                back = jnp.where(sl == s1, w1, 0.0) + jnp.where(sl == s2, w2, 0.0)
                o_ref[...] += _dot(back, ys_ref[c * MOE_SC:(c + 1) * MOE_SC, :]).reshape(bb, tt, d)
        o_ref[...] = x_ref[...] + mod_ref[...][:, 5:6, :] * o_ref[...]


def _moe_sorted_call(h2, logits, w_gate, w_up, w_down, x, mod):
    b, t, d = x.shape
    tm = MOE_TM
    nt = t // tm
    n_tiles = b * nt
    ne, _, hid = w_gate.shape
    meta, metat, nblk, boff = pl.pallas_call(
        _moe_route_kernel,
        grid=(n_tiles,),
        in_specs=[pl.BlockSpec((tm, LANE), lambda i: (i, 0))],
        out_specs=[pl.BlockSpec((tm, LANE), lambda i: (i, 0)),
                   pl.BlockSpec((8, tm), lambda i: (i, 0)),
                   pl.BlockSpec((1, 1, LANE), lambda i: (i, 0, 0)),
                   pl.BlockSpec((1, 1, LANE), lambda i: (i, 0, 0))],
        out_shape=[jax.ShapeDtypeStruct((n_tiles * tm, LANE), F32),
                   jax.ShapeDtypeStruct((n_tiles * 8, tm), F32),
                   jax.ShapeDtypeStruct((n_tiles, 1, LANE), jnp.int32),
                   jax.ShapeDtypeStruct((n_tiles, 1, LANE), jnp.int32)],
        compiler_params=_cparams(("parallel",)),
        name="moe_route",
    )(logits)
    xmap = lambda i, e, nb, bo: (i // nt, i % nt, 0)
    row = lambda i, e, nb, bo: (i, 0)
    wmap = lambda i, e, nb, bo: (e, 0, 0)
    return pl.pallas_call(
        _moe_sorted_kernel,
        grid_spec=pltpu.PrefetchScalarGridSpec(
            num_scalar_prefetch=2,
            grid=(n_tiles, ne // MOE_EPS),
            in_specs=[pl.BlockSpec((tm, d), row), pl.BlockSpec((tm, LANE), row), pl.BlockSpec((8, tm), row),
                      pl.BlockSpec((MOE_EPS, d, hid), wmap), pl.BlockSpec((MOE_EPS, d, hid), wmap),
                      pl.BlockSpec((MOE_EPS, hid, d), wmap),
                      pl.BlockSpec((1, tm, d), xmap),
                      pl.BlockSpec((1, mod.shape[1], d), lambda i, e, nb, bo: (i // nt, 0, 0))],
            out_specs=pl.BlockSpec((1, tm, d), xmap),
            scratch_shapes=[pltpu.VMEM((MOE_SLOTS, d), BF16), pltpu.VMEM((MOE_SLOTS, d), BF16)]),
        out_shape=jax.ShapeDtypeStruct((b, t, d), F32),
        compiler_params=_cparams(("parallel", "arbitrary")),
        name="moe_sorted",
    )(nblk.reshape(-1), boff.reshape(-1), h2, meta, metat, w_gate, w_up, w_down, x, mod)


def _layer(x, mod, pos0, lw, state, cache, out_dtype):
    b, t, d = x.shape
    ssm0, ssm_conv0, hg0, lru0, lru_conv0 = state
    h = _prenorm_call(x, lw['norm1_g'], mod, sh_row=0, sc_row=1)
    p = _matmul_call(h, lw['w_in'])
    y_a, ssm1, ssm_conv1 = _ssd_call(p, b, t, lw['ssd_conv_w'], lw['ssd_conv_b'], lw['ssd_dt_bias'], lw['ssd_a_log'],
                                     lw['ssd_d'], lw['ssd_norm_g'], ssm0, ssm_conv0, out_dtype)
    y_b, hg1 = _hgrn_call(p, b, t, lw['hg_lb'], lw['hg_norm_g'], hg0, out_dtype)
    y_c, ckv, krope = _mla_call(p, b, t, pos0, lw['mla_q_norm_g'], lw['mla_w_uq'], lw['mla_kv_norm_g'],
                                lw['mla_w_ukv'], *cache, out_dtype)
    y_d, lru1, lru_conv1 = _lru_call(p, b, t, pos0, lw['lru_conv_w'], lw['lru_conv_b'], lw['lru_w_r'], lw['lru_b_r'],
                                     lw['lru_w_i'], lw['lru_b_i'], lw['lru_a'], lru0, lru_conv0, out_dtype)
    x, h2, logits = _merge_call(p, (y_a, y_b, y_c, y_d), lw['w_branch'], lw['w_out'], x, mod, lw['norm2_g'],
                                lw['w_router'], lw['b_router'])
    moe = _moe_sorted_call if t % MOE_TM == 0 else _moe_call
    x = moe(h2, logits, lw['moe_w_gate'], lw['moe_w_up'], lw['moe_w_down'], x, mod)
    new =(ckv.reshape(b, t, -1), krope.reshape(b, t, -1), ssm1, ssm_conv1, hg1, lru1, lru_conv1)
    return x, new


def kernel(x_prompt, x_sample, c_prompt, c_sample, cache_kv_latent, cache_k_rope, state_ssm, state_ssm_conv,
           state_hgrn, state_lru, state_lru_conv, page_table, norm1_g, norm2_g, w_mod, b_mod, w_in, ssd_conv_w,
           ssd_conv_b, ssd_dt_bias, ssd_a_log, ssd_d, ssd_norm_g, hg_lb_raw, hg_norm_g, mla_q_norm_g, mla_w_uq,
           mla_kv_norm_g, mla_w_ukv, lru_conv_w, lru_conv_b, lru_w_r, lru_b_r, lru_w_i, lru_b_i, lru_a, w_branch,
           w_out, moe_w_grp, moe_b_grp, moe_w_rt, moe_b_rt, moe_w_gate, moe_w_up, moe_w_down, final_norm_g):
    bp, tp, d = x_prompt.shape
    bs, ts, _ = x_sample.shape
    depth = w_in.shape[0]
    n_past = page_table.shape[1] * cache_kv_latent.shape[2]
    lb_all = jnp.cumsum(jax.nn.softmax(hg_lb_raw.astype(F32), axis=0), axis=0)
    lb_all = lb_all - lb_all[:1]
    c_all = jnp.concatenate([c_prompt, c_sample], axis=0)
    zeros = lambda *s: jnp.zeros(s, F32)
    yp, ys = x_prompt, x_sample
    p_new, s_new = [], []
    for l in range(depth):
        mod = _mod_call(c_all, w_mod[l].astype(BF16), b_mod[l]).reshape(bp + bs, 6, d)
        pad_r = LANE - MOE_E - MOE_G
        lw = {
            'norm1_g': norm1_g[l], 'norm2_g': norm2_g[l], 'w_in': _pack_w_in(w_in[l]),
            'ssd_conv_w': ssd_conv_w[l], 'ssd_conv_b': ssd_conv_b[l], 'ssd_dt_bias': ssd_dt_bias[l],
            'ssd_a_log': ssd_a_log[l], 'ssd_d': ssd_d[l], 'ssd_norm_g': ssd_norm_g[l],
            'hg_lb': lb_all[l], 'hg_norm_g': hg_norm_g[l],
            'mla_q_norm_g': mla_q_norm_g[l], 'mla_w_uq': mla_w_uq[l],
            'mla_kv_norm_g': mla_kv_norm_g[l], 'mla_w_ukv': mla_w_ukv[l],
            'lru_conv_w': lru_conv_w[l], 'lru_conv_b': lru_conv_b[l], 'lru_w_r': lru_w_r[l], 'lru_b_r': lru_b_r[l],
            'lru_w_i': lru_w_i[l], 'lru_b_i': lru_b_i[l], 'lru_a': lru_a[l],
            'w_branch': w_branch[l].astype(BF16), 'w_out': w_out[l].astype(BF16),
            'w_router': jnp.pad(jnp.concatenate([moe_w_rt[l], moe_w_grp[l]], axis=1), ((0, 0), (0, pad_r))),
            'b_router': jnp.pad(jnp.concatenate([moe_b_rt[l], moe_b_grp[l]]), (0, pad_r)).reshape(1, LANE),
            'moe_w_gate': moe_w_gate[l].astype(BF16), 'moe_w_up': moe_w_up[l].astype(BF16),
            'moe_w_down': moe_w_down[l].astype(BF16),
        }
        p_state = (zeros(bp, SSD_H, SSD_P, SSD_N), zeros(bp, CONV_W - 1, SSD_CONV), zeros(bp, HG_H, HG_K, HG_V),
                   zeros(bp, LRU_W), zeros(bp, CONV_W - 1, LRU_W))
        s_state = (state_ssm[l], state_ssm_conv[l], state_hgrn[l], state_lru[l], state_lru_conv[l])
        yp, pn = _layer(yp, mod[:bp], 0, lw, p_state, (None, None, None, l), BF16)
        ys, sn = _layer(ys, mod[bp:], n_past, lw, s_state, (cache_kv_latent, cache_k_rope, page_table, l), F32)
        p_new.append(pn)
        s_new.append(sn)
    no_mod = zeros(1, 2, d)
    yp = _prenorm_call(yp, final_norm_g, jnp.broadcast_to(no_mod, (bp, 2, d)), 0, 1, F32).reshape(bp, tp, d)
    ys = _prenorm_call(ys, final_norm_g, jnp.broadcast_to(no_mod, (bs, 2, d)), 0, 1, F32).reshape(bs, ts, d)
    stk = lambda news, j: jnp.stack([n[j] for n in news])
    return (yp, ys) + tuple(stk(p_new, j) for j in range(7)) + tuple(stk(s_new, j) for j in range(7))
```

```python
import functools
import math

import jax
import jax.numpy as jnp
from jax import lax
from jax.experimental import pallas as pl
from jax.experimental.pallas import tpu as pltpu

F32 = jnp.float32
BF16 = jnp.bfloat16
HI = lax.Precision.HIGHEST
NEG = -1e30

RMS_EPS = 1e-6
D_MODEL = 1024
CONV_W = 4
SSD_H, SSD_P, SSD_G, SSD_N = 8, 64, 2, 64
SSD_INNER = SSD_H * SSD_P
SSD_CONV = SSD_INNER + 2 * SSD_G * SSD_N
SSD_CHUNK = 64
HG_H, HG_K, HG_V = 4, 128, 128
HG_INNER = HG_H * HG_V
HG_CHUNK = 16
MLA_H, MLA_QL, MLA_KVL, MLA_NOPE, MLA_ROPE, MLA_V = 8, 256, 256, 64, 32, 64
ROPE_THETA = 10000.0
LRU_W, LRU_NB = 512, 8
LRU_C = 8.0
N_BRANCH = 4
MOE_G, MOE_EPG, MOE_E, MOE_HID = 4, 8, 32, 256

V7X_VMEM_LIMIT = 56 * 1024 * 1024
LANE = 128

P_GL, P_Z, P_HQ, P_HF, P_HI, P_HG, P_LX, P_LY = 0, 4096, 4608, 5120, 5632, 6144, 6656, 7168
P_XBC, P_QD, P_KVD, P_DT, P_KR = 7680, 8448, 8704, 8960, 9088
P_TOTAL = 9216


def _cparams(sem):
    return pltpu.CompilerParams(dimension_semantics=sem, vmem_limit_bytes=V7X_VMEM_LIMIT)


def _silu(x):
    return x * (1.0 / (1.0 + jnp.exp(-x)))


def _sigmoid(x):
    return 1.0 / (1.0 + jnp.exp(-x))


def _softplus(x):
    return jnp.maximum(x, 0.0) + jnp.log(1.0 + jnp.exp(-jnp.abs(x)))


def _iota(shape, dim):
    return lax.broadcasted_iota(jnp.int32, shape, dim)


def _dot(a, b):
    return jnp.dot(a.astype(BF16), b.astype(BF16), preferred_element_type=F32)


def _dot_nt(a, b):
    return lax.dot_general(a.astype(BF16), b.astype(BF16), (((1,), (1,)), ((), ())), preferred_element_type=F32)


def _dot_tn(a, b):
    return lax.dot_general(a.astype(BF16), b.astype(BF16), (((0,), (0,)), ((), ())), preferred_element_type=F32)


def _dot_hi(a, b):
    return jnp.dot(a, b, precision=HI, preferred_element_type=F32)


def _dot_nt_hi(a, b):
    return lax.dot_general(a, b, (((1,), (1,)), ((), ())), precision=HI, preferred_element_type=F32)


def _tril(n):
    return (_iota((n, n), 0) >= _iota((n, n), 1)).astype(F32)


def _eye(n):
    return (_iota((n, n), 0) == _iota((n, n), 1)).astype(F32)


def _row_blocks(b, t, target):
    if t >= target:
        return 1, target
    return min(b, target // t), t


def _mod_kernel(c_ref, w_ref, b_ref, o_ref):
    o_ref[...] = _dot(_silu(c_ref[...]), w_ref[...]) + b_ref[...]


def _mod_call(c, w, b):
    m, d = c.shape
    n = w.shape[1]
    tn = 1536
    return pl.pallas_call(
        _mod_kernel,
        grid=(n // tn,),
        in_specs=[pl.BlockSpec((m, d), lambda j: (0, 0)),
                  pl.BlockSpec((d, tn), lambda j: (0, j)),
                  pl.BlockSpec((1, tn), lambda j: (0, j))],
        out_specs=pl.BlockSpec((m, tn), lambda j: (0, j)),
        out_shape=jax.ShapeDtypeStruct((m, n), F32),
        compiler_params=_cparams(("parallel",)),
        name="adaln_mod",
    )(c, w, b.reshape(1, n))


def _prenorm_kernel(x_ref, g_ref, mod_ref, o_ref, *, sh_row, sc_row):
    x = x_ref[...]
    bb, tt, d = x.shape
    y = x * lax.rsqrt(jnp.mean(x * x, axis=-1, keepdims=True) + RMS_EPS) * g_ref[...]
    mod = mod_ref[...]
    y = y * (1.0 + mod[:, sc_row:sc_row + 1, :]) + mod[:, sh_row:sh_row + 1, :]
    o_ref[...] = y.reshape(bb * tt, d).astype(o_ref.dtype)


def _prenorm_call(x, g, mod, sh_row, sc_row, out_dtype=BF16):
    b, t, d = x.shape
    bb, tt = _row_blocks(b, t, 512)
    nt = t // tt
    return pl.pallas_call(
        functools.partial(_prenorm_kernel, sh_row=sh_row, sc_row=sc_row),
        grid=(b // bb, nt),
        in_specs=[pl.BlockSpec((bb, tt, d), lambda i, j: (i, j, 0)),
                  pl.BlockSpec((1, 1, d), lambda i, j: (0, 0, 0)),
                  pl.BlockSpec((bb, mod.shape[1], d), lambda i, j: (i, 0, 0))],
        out_specs=pl.BlockSpec((bb * tt, d), lambda i, j: (i * nt + j, 0)),
        out_shape=jax.ShapeDtypeStruct((b * t, d), out_dtype),
        compiler_params=_cparams(("parallel", "parallel")),
        name="prenorm",
    )(x, g.reshape(1, 1, d), mod)


def _matmul_kernel(x_ref, w_ref, o_ref):
    o_ref[...] = jnp.dot(x_ref[...], w_ref[...], preferred_element_type=F32)


def _matmul_call(x, w, tm=1024, tn=1024):
    m, k = x.shape
    n = w.shape[1]
    tm = min(tm, m)
    return pl.pallas_call(
        _matmul_kernel,
        grid=(n // tn, m // tm),
        in_specs=[pl.BlockSpec((tm, k), lambda j, i: (i, 0)),
                  pl.BlockSpec((k, tn), lambda j, i: (0, j))],
        out_specs=pl.BlockSpec((tm, tn), lambda j, i: (i, j)),
        out_shape=jax.ShapeDtypeStruct((m, n), F32),
        compiler_params=_cparams(("parallel", "parallel")),
        name="in_proj",
    )(x, w)


def _pack_w_in(w_in):
    d = w_in.shape[0]
    sizes = (SSD_INNER, SSD_CONV, SSD_H, HG_H * HG_K, HG_H * HG_K, HG_INNER, HG_INNER,
             MLA_QL, MLA_KVL, MLA_ROPE, LRU_W, LRU_W, N_BRANCH * D_MODEL)
    offs = [0]
    for s in sizes:
        offs.append(offs[-1] + s)
    z, xbc, dt, hq, hf, hi, hg, qd, kvd, krr, lx, ly, gl = [w_in[:, offs[i]:offs[i + 1]] for i in range(13)]
    pad = lambda a: jnp.pad(a, ((0, 0), (0, LANE - a.shape[1])))
    return jnp.concatenate([gl, z, hq, hf, hi, hg, lx, ly, xbc, qd, kvd, pad(dt), pad(krr)], axis=1).astype(BF16)


def _causal_conv(xx_ref, x, w_ref, b_ref, tt):
    xx_ref[8:8 + tt, :] = x
    w = w_ref[...]
    y = b_ref[...] + w[3:4, :] * x
    for k in range(1, CONV_W):
        y = y + w[3 - k:4 - k, :] * xx_ref[8 - k:8 - k + tt, :]
    return y


def _causal_conv_short(x, hist, w_ref, b_ref, tt):
    rows_n = x.shape[0]
    tpos = _iota(x.shape, 0) % tt
    w = w_ref[...]
    y = b_ref[...] + w[3:4, :] * x
    for k in range(1, CONV_W):
        src = jnp.where(tpos >= k, pltpu.roll(x, k, 0), pltpu.roll(hist, rows_n - tt + k, 0))
        y = y + w[3 - k:4 - k, :] * src
    return y


def _ssd_kernel(z_ref, xbc_ref, dt_ref, cw_ref, cb_ref, dtb_ref, alog_ref, dfull_ref, ng_ref, s0_ref, c0_ref,
                y_ref, s1_ref, st_ref, xx_ref, yc_ref, *, bb, tt, cl):
    t = pl.program_id(1)
    nt = pl.num_programs(1)
    rows_n = bb * tt
    hp = SSD_H // SSD_G * SSD_P
    heads = [divmod(h, SSD_H // SSD_G) for h in range(SSD_H)]

    @pl.when(t == 0)
    def _():
        for bi in range(bb):
            for h, (g, r) in enumerate(heads):
                st_ref[bi, g, :, r * SSD_P:(r + 1) * SSD_P] = s0_ref[bi, h].T

    if bb == 1:
        @pl.when(t == 0)
        def _():
            xx_ref[0:8, :] = jnp.zeros((8, SSD_CONV), F32)
            xx_ref[5:8, :] = c0_ref[0]

        conv = _causal_conv(xx_ref, xbc_ref[...], cw_ref, cb_ref, tt)
        xx_ref[0:8, :] = xx_ref[tt:tt + 8, :]
    else:
        conv = _causal_conv_short(xbc_ref[...], c0_ref[...], cw_ref, cb_ref, tt)
    act = _silu(conv)
    xs = act[:, :SSD_INNER]
    bm = act[:, SSD_INNER:SSD_INNER + SSD_G * SSD_N]
    cm = act[:, SSD_INNER + SSD_G * SSD_N:]
    dt = _softplus(dt_ref[...] + dtb_ref[...])
    da = dt * (-jnp.exp(alog_ref[...]))
    expand = (_iota((LANE, SSD_INNER), 1) // SSD_P == _iota((LANE, SSD_INNER), 0)).astype(F32)
    dtf = _dot_hi(dt, expand)
    ti, si = _iota((rows_n, rows_n), 0), _iota((rows_n, rows_n), 1)
    same_chunk = ti // cl == si // cl
    acs_all = _dot_hi((same_chunk & (ti >= si)).astype(F32), da)
    acs_t = lax.dot_general(da, (same_chunk & (ti <= si)).astype(F32), (((0,), (0,)), ((), ())),
                            precision=HI, preferred_element_type=F32)
    last_all = _dot_hi((si == ti // cl * cl + (cl - 1)).astype(F32), acs_all)
    eacs_all = jnp.exp(_dot_hi(acs_all, expand))
    dend_all = jnp.exp(_dot_hi(last_all - acs_all, expand))
    xdt_all = xs * dtf
    xsc_all = xdt_all * dend_all
    tri = _iota((cl, cl), 0) >= _iota((cl, cl), 1)

    for c in range(rows_n // cl):
        rows = slice(c * cl, (c + 1) * cl)
        bi = c * cl // tt
        eacs = eacs_all[rows]
        xdt = xdt_all[rows]
        xsc = xsc_all[rows]
        for g in range(SSD_G):
            bg = bm[rows, g * SSD_N:(g + 1) * SSD_N]
            cg = cm[rows, g * SSD_N:(g + 1) * SSD_N]
            cb = _dot_nt(cg, bg)
            sg = st_ref[bi, g]
            y_inter = _dot(cg, sg) * eacs[:, g * hp:(g + 1) * hp]
            for r in range(SSD_H // SSD_G):
                h = g * (SSD_H // SSD_G) + r
                seg = acs_all[rows, h:h + 1] - acs_t[h:h + 1, rows]
                decay = jnp.exp(jnp.where(tri, seg, NEG))
                y_h = _dot(cb * decay, xdt[:, h * SSD_P:(h + 1) * SSD_P])
                yc_ref[rows, h * SSD_P:(h + 1) * SSD_P] = y_h + y_inter[:, r * SSD_P:(r + 1) * SSD_P]
            cdec = eacs[cl - 1:cl, g * hp:(g + 1) * hp]
            st_ref[bi, g] = cdec * sg + _dot_tn(bg, xsc[:, g * hp:(g + 1) * hp])

    y = yc_ref[...] + dfull_ref[...] * xs
    yz = y * _silu(z_ref[...])
    out = yz * lax.rsqrt(jnp.mean(yz * yz, axis=-1, keepdims=True) + RMS_EPS) * ng_ref[...]
    y_ref[...] = out.astype(y_ref.dtype)

    @pl.when(t == nt - 1)
    def _():
        for bi in range(bb):
            for h, (g, r) in enumerate(heads):
                s1_ref[bi, h] = st_ref[bi, g, :, r * SSD_P:(r + 1) * SSD_P].T


def _conv_tail(p, b, t, off, width):
    return p[:, off:off + width].reshape(b, t, width)[:, t - (CONV_W - 1):, :]


def _ssd_call(p, b, t, conv_w, conv_b, dt_bias, a_log, d_skip, norm_g, s0, c0, out_dtype):
    bb, tt = (1, min(t, 256)) if t >= 256 else (min(b, 16), t)
    cl = SSD_CHUNK if tt % SSD_CHUNK == 0 else tt
    nt = t // tt
    rows_n = bb * tt
    padl = lambda v: jnp.pad(v, (0, LANE - v.shape[0])).reshape(1, LANE)
    row = lambda i, j: i * nt + j
    const2 = lambda i, j: (0, 0)
    if bb == 1:
        hist, hist_spec = c0, pl.BlockSpec((1, CONV_W - 1, SSD_CONV), lambda i, j: (i, 0, 0))
    else:
        hist = jnp.pad(c0, ((0, 0), (tt - (CONV_W - 1), 0), (0, 0))).reshape(b * tt, SSD_CONV)
        hist_spec = pl.BlockSpec((rows_n, SSD_CONV), lambda i, j: (i, 0))
    y, s1 = pl.pallas_call(
        functools.partial(_ssd_kernel, bb=bb, tt=tt, cl=cl),
        grid=(b // bb, nt),
        in_specs=[pl.BlockSpec((rows_n, SSD_INNER), lambda i, j: (row(i, j), P_Z // SSD_INNER)),
                  pl.BlockSpec((rows_n, SSD_CONV), lambda i, j: (row(i, j), P_XBC // SSD_CONV)),
                  pl.BlockSpec((rows_n, LANE), lambda i, j: (row(i, j), P_DT // LANE)),
                  pl.BlockSpec((CONV_W, SSD_CONV), const2),
                  pl.BlockSpec((1, SSD_CONV), const2),
                  pl.BlockSpec((1, LANE), const2),
                  pl.BlockSpec((1, LANE), const2),
                  pl.BlockSpec((1, SSD_INNER), const2),
                  pl.BlockSpec((1, SSD_INNER), const2),
                  pl.BlockSpec((bb, SSD_H, SSD_P, SSD_N), lambda i, j: (i, 0, 0, 0)),
                  hist_spec],
        out_specs=[pl.BlockSpec((rows_n, SSD_INNER), lambda i, j: (row(i, j), 0)),
                   pl.BlockSpec((bb, SSD_H, SSD_P, SSD_N), lambda i, j: (i, 0, 0, 0))],
        out_shape=[jax.ShapeDtypeStruct((b * t, SSD_INNER), out_dtype),
                   jax.ShapeDtypeStruct((b, SSD_H, SSD_P, SSD_N), F32)],
        scratch_shapes=[pltpu.VMEM((bb, SSD_G, SSD_N, SSD_H // SSD_G * SSD_P), F32),
                        pltpu.VMEM((tt + 8, SSD_CONV), F32),
                        pltpu.VMEM((rows_n, SSD_INNER), F32)],
        compiler_params=_cparams(("parallel", "arbitrary")),
        name="ssd",
    )(p, p, p, conv_w, conv_b.reshape(1, -1), padl(dt_bias), padl(a_log),
      jnp.repeat(d_skip, SSD_P).reshape(1, -1), norm_g.reshape(1, -1), s0, hist)
    return y, s1, _conv_tail(p, b, t, P_XBC, SSD_CONV)


def _hgrn_kernel(q_ref, f_ref, i_ref, g_ref, loglb_ref, log1m_ref, onem_ref, ng_ref, s0_ref,
                 y_ref, s1_ref, st_ref, o_ref, *, bb, tt, cl):
    t = pl.program_id(1)
    nt = pl.num_programs(1)
    rows_n = bb * tt
    hs = [slice(h * HG_K, (h + 1) * HG_K) for h in range(HG_H)]

    @pl.when(t == 0)
    def _():
        for bi in range(bb):
            for h in range(HG_H):
                st_ref[bi, h] = s0_ref[bi, h].T

    q, hf, v = q_ref[...], f_ref[...], i_ref[...]
    b_ = log1m_ref[...] - _softplus(-hf)
    loglb = loglb_ref[...]
    logf = jnp.maximum(loglb, b_) + jnp.log(1.0 + jnp.exp(-jnp.abs(loglb - b_)))
    kin = onem_ref[...] * _sigmoid(-hf)
    ti, si = _iota((rows_n, rows_n), 0), _iota((rows_n, rows_n), 1)
    bc = _dot_hi(((ti // cl == si // cl) & (ti >= si)).astype(F32), logf)
    bl = _dot_hi((si == ti // cl * cl + (cl - 1)).astype(F32), bc)
    in_chunk = _iota((rows_n, HG_INNER), 0) % cl
    p = q * kin
    o = [jnp.sum(p[:, s], axis=-1, keepdims=True) * v[:, s] for s in hs]
    for d in range(1, cl):
        kd, bd, vd = pltpu.roll(kin, d, 0), pltpu.roll(bc, d, 0), pltpu.roll(v, d, 0)
        p = q * kd * jnp.exp(jnp.where(in_chunk >= d, bc - bd, NEG))
        o = [o[h] + jnp.sum(p[:, s], axis=-1, keepdims=True) * vd[:, s] for h, s in enumerate(hs)]
    for h in range(HG_H):
        o_ref[:, hs[h]] = o[h]
    qe = q * jnp.exp(bc)
    ke = kin * jnp.exp(bl - bc)
    dec = jnp.exp(bl)
    for c in range(rows_n // cl):
        rows = slice(c * cl, (c + 1) * cl)
        bi = c * cl // tt
        for h, s in enumerate(hs):
            st = st_ref[bi, h]
            o_ref[rows, s] += _dot_nt(qe[rows, s], st)
            st_ref[bi, h] = dec[c * cl:c * cl + 1, s] * st + _dot_tn(v[rows, s], ke[rows, s])
    ng = ng_ref[...]
    outs = []
    for s in hs:
        oh = o_ref[:, s]
        outs.append(oh * lax.rsqrt(jnp.mean(oh * oh, axis=-1, keepdims=True) + RMS_EPS) * ng)
    y_ref[...] = (jnp.concatenate(outs, axis=-1) * _silu(g_ref[...])).astype(y_ref.dtype)

    @pl.when(t == nt - 1)
    def _():
        for bi in range(bb):
            for h in range(HG_H):
                s1_ref[bi, h] = st_ref[bi, h].T


def _hgrn_call(p, b, t, lb, norm_g, s0, out_dtype):
    bb, tt = _row_blocks(b, t, 256) if t >= 256 else (min(b, 16), t)
    cl = HG_CHUNK if tt % HG_CHUNK == 0 else tt
    nt = t // tt
    row = lambda i, j: i * nt + j
    const2 = lambda i, j: (0, 0)
    col = lambda off: (lambda i, j: (row(i, j), off // HG_INNER))
    vec = pl.BlockSpec((1, HG_INNER), const2)
    blk = pl.BlockSpec((bb * tt, HG_INNER), lambda i, j: (row(i, j), 0))
    return pl.pallas_call(
        functools.partial(_hgrn_kernel, bb=bb, tt=tt, cl=cl),
        grid=(b // bb, nt),
        in_specs=[pl.BlockSpec((bb * tt, HG_INNER), col(P_HQ)), pl.BlockSpec((bb * tt, HG_INNER), col(P_HF)),
                  pl.BlockSpec((bb * tt, HG_INNER), col(P_HI)), pl.BlockSpec((bb * tt, HG_INNER), col(P_HG)),
                  vec, vec, vec, pl.BlockSpec((1, HG_V), const2),
                  pl.BlockSpec((bb, HG_H, HG_K, HG_V), lambda i, j: (i, 0, 0, 0))],
        out_specs=[blk, pl.BlockSpec((bb, HG_H, HG_K, HG_V), lambda i, j: (i, 0, 0, 0))],
        out_shape=[jax.ShapeDtypeStruct((b * t, HG_INNER), out_dtype),
                   jax.ShapeDtypeStruct((b, HG_H, HG_K, HG_V), F32)],
        scratch_shapes=[pltpu.VMEM((bb, HG_H, HG_V, HG_K), F32), pltpu.VMEM((bb * tt, HG_INNER), F32)],
        compiler_params=_cparams(("parallel", "arbitrary")),
        name="hgrn2",
    )(p, p, p, p, jnp.log(lb).reshape(1, -1), jnp.log1p(-lb).reshape(1, -1), (1.0 - lb).reshape(1, -1),
      norm_g.reshape(1, -1), s0)


def _gelu_tanh(x):
    return 0.5 * x * (1.0 + jnp.tanh(math.sqrt(2.0 / math.pi) * (x + 0.044715 * (x * x * x))))


def _lru_kernel(lx_ref, ly_ref, cw_ref, cb_ref, wr_ref, br_ref, wi_ref, bi_ref, ap_ref, h0_ref, c0_ref,
                y_ref, h1_ref, carry_ref, xx_ref, hs_ref, *, bb, tt, pos0):
    t = pl.program_id(1)
    nt = pl.num_programs(1)
    rows_n = bb * tt
    groups = tt // 8

    if bb == 1:
        @pl.when(t == 0)
        def _():
            xx_ref[0:8, :] = jnp.zeros((8, LRU_W), F32)
            xx_ref[5:8, :] = c0_ref[0]
            carry_ref[...] = h0_ref[0]

        xl = _causal_conv(xx_ref, lx_ref[...], cw_ref, cb_ref, tt)
        xx_ref[0:8, :] = xx_ref[tt:tt + 8, :]
    else:
        xl = _causal_conv_short(lx_ref[...], c0_ref[...], cw_ref, cb_ref, tt)
    r = _sigmoid(_dot(xl, wr_ref[...]) + br_ref[...])
    ig = _sigmoid(_dot(xl, wi_ref[...]) + bi_ref[...])
    log_a = -LRU_C * r * _softplus(-ap_ref[...])
    a = jnp.exp(log_a)
    mult = jnp.sqrt(jnp.tanh(-log_a) * (a * a + 1.0))
    pos = pos0 + t * tt + _iota((rows_n, LRU_W), 0) % tt
    u = jnp.where(pos == 0, 1.0, mult) * (ig * xl)
    row8 = _iota((8, LRU_W), 0)
    carry = carry_ref[...]
    for gi in range(rows_n // 8):
        bi, gj = divmod(gi, groups)
        if bb > 1 and gj == 0:
            carry = h0_ref[bi]
        a8, u8 = a[gi * 8:(gi + 1) * 8], u[gi * 8:(gi + 1) * 8]
        for k in (1, 2, 4):
            m = row8 >= k
            u8, a8 = (jnp.where(m, a8 * pltpu.roll(u8, k, 0) + u8, u8),
                      jnp.where(m, a8 * pltpu.roll(a8, k, 0), a8))
        h8 = u8 + a8 * carry
        carry = h8[7:8, :]
        hs_ref[gi * 8:(gi + 1) * 8, :] = h8
        if bb > 1 and gj == groups - 1:
            h1_ref[bi] = carry
    y_ref[...] = (hs_ref[...] * _gelu_tanh(ly_ref[...])).astype(y_ref.dtype)

    if bb == 1:
        carry_ref[...] = carry

        @pl.when(t == nt - 1)
        def _():
            h1_ref[0] = carry


def _block_diag(w):
    nb, di, do = w.shape
    eye = jnp.eye(nb, dtype=w.dtype)
    return (eye[:, None, :, None] * w[:, :, None, :]).reshape(nb * di, nb * do)


def _lru_call(p, b, t, pos0, conv_w, conv_b, w_r, b_r, w_i, b_i, a_param, h0, c0, out_dtype):
    bb, tt = (1, min(t, 256)) if t >= 256 else (min(b, 16), t)
    nt = t // tt
    rows_n = bb * tt
    row = lambda i, j: i * nt + j
    const2 = lambda i, j: (0, 0)
    vec = pl.BlockSpec((1, LRU_W), const2)
    mat = pl.BlockSpec((LRU_W, LRU_W), const2)
    if bb == 1:
        hist, hist_spec = c0, pl.BlockSpec((1, CONV_W - 1, LRU_W), lambda i, j: (i, 0, 0))
    else:
        hist = jnp.pad(c0, ((0, 0), (tt - (CONV_W - 1), 0), (0, 0))).reshape(b * tt, LRU_W)
        hist_spec = pl.BlockSpec((rows_n, LRU_W), lambda i, j: (i, 0))
    y, h1 = pl.pallas_call(
        functools.partial(_lru_kernel, bb=bb, tt=tt, pos0=pos0),
        grid=(b // bb, nt),
        in_specs=[pl.BlockSpec((rows_n, LRU_W), lambda i, j: (row(i, j), P_LX // LRU_W)),
                  pl.BlockSpec((rows_n, LRU_W), lambda i, j: (row(i, j), P_LY // LRU_W)),
                  pl.BlockSpec((CONV_W, LRU_W), const2), vec, mat, vec, mat, vec, vec,
                  pl.BlockSpec((bb, 1, LRU_W), lambda i, j: (i, 0, 0)),
                  hist_spec],
        out_specs=[pl.BlockSpec((rows_n, LRU_W), lambda i, j: (row(i, j), 0)),
                   pl.BlockSpec((bb, 1, LRU_W), lambda i, j: (i, 0, 0))],
        out_shape=[jax.ShapeDtypeStruct((b * t, LRU_W), out_dtype),
                   jax.ShapeDtypeStruct((b, 1, LRU_W), F32)],
        scratch_shapes=[pltpu.VMEM((1, LRU_W), F32),
                        pltpu.VMEM((tt + 8, LRU_W), F32),
                        pltpu.VMEM((rows_n, LRU_W), F32)],
        compiler_params=_cparams(("parallel", "arbitrary")),
        name="rglru",
    )(p, p, conv_w, conv_b.reshape(1, -1), _block_diag(w_r).astype(BF16), b_r.reshape(1, -1),
      _block_diag(w_i).astype(BF16), b_i.reshape(1, -1), a_param.reshape(1, -1),
      h0.reshape(b, 1, LRU_W), hist)
    return y, h1.reshape(b, LRU_W), _conv_tail(p, b, t, P_LX, LRU_W)


MLA_SCALE = (MLA_NOPE + MLA_ROPE) ** -0.5
ROPE_HALF = MLA_ROPE // 2


def _rope_rotate(x, cos, sin):
    lane = _iota(x.shape, 1)
    rot = jnp.where(lane % MLA_ROPE < ROPE_HALF, -pltpu.roll(x, LANE - ROPE_HALF, 1), pltpu.roll(x, ROPE_HALF, 1))
    return x * cos + rot * sin


MLA_QK = MLA_KVL + LANE

def _mla_prep_kernel(qd_ref, kvd_ref, krr_ref, cos_ref, sin_ref, qg_ref, wuq_ref, kvg_ref, wukt_ref,
                     qcat_ref, ckv_ref, krope_ref, kcat_ref):
    qd = qd_ref[...]
    qn = qd * lax.rsqrt(jnp.mean(qd * qd, axis=-1, keepdims=True) + RMS_EPS) * qg_ref[...]
    q = _dot(qn, wuq_ref[...])
    cos, sin = cos_ref[...], sin_ref[...]
    nn = MLA_H * MLA_NOPE
    lane = _iota((q.shape[0], LANE), 1)
    halves = [_rope_rotate(q[:, nn + i * LANE:nn + (i + 1) * LANE], cos, sin) for i in range(2)]
    per_half = LANE // MLA_ROPE
    for h in range(MLA_H):
        qa = _dot(q[:, h * MLA_NOPE:(h + 1) * MLA_NOPE], wukt_ref[h]) * MLA_SCALE
        qcat_ref[h, :, :MLA_KVL] = qa.astype(qcat_ref.dtype)
        half, sh = halves[h // per_half], (h % per_half) * MLA_ROPE
        piece = pltpu.roll(half, LANE - sh, 1) if sh else half
        qcat_ref[h, :, MLA_KVL:] = jnp.where(lane < MLA_ROPE, piece * MLA_SCALE, 0.0).astype(qcat_ref.dtype)
    kvd = kvd_ref[...]
    ckv = kvd * lax.rsqrt(jnp.mean(kvd * kvd, axis=-1, keepdims=True) + RMS_EPS) * kvg_ref[...]
    ckv_ref[...] = ckv
    kr = jnp.where(lane < MLA_ROPE, _rope_rotate(krr_ref[...], cos, sin), 0.0)
    krope_ref[...] = kr[:, :MLA_ROPE]
    kcat_ref[:, :MLA_KVL] = ckv.astype(BF16)
    kcat_ref[:, MLA_KVL:] = kr.astype(BF16)


def _lane_wide(v, n):
    if n == LANE:
        return v
    return jnp.concatenate([v] * (n // LANE), axis=1) if n % LANE == 0 else v[:, :n]


def _softmax_step(s, vals, m_ref, l_ref, acc_ref):
    m_old = m_ref[...]
    m_new = jnp.maximum(m_old, jnp.max(s, axis=-1, keepdims=True))
    pr = jnp.exp(s - _lane_wide(m_new, s.shape[1]))
    alpha = jnp.exp(m_old - m_new)
    l_ref[...] = alpha * l_ref[...] + jnp.sum(pr, axis=-1, keepdims=True)
    acc_ref[...] = _lane_wide(alpha, acc_ref.shape[1]) * acc_ref[...] + _dot(pr, vals)
    m_ref[...] = m_new


def _softmax_init(m_ref, l_ref, acc_ref):
    m_ref[...] = jnp.full(m_ref.shape, NEG, F32)
    l_ref[...] = jnp.zeros(l_ref.shape, F32)
    acc_ref[...] = jnp.zeros(acc_ref.shape, F32)


def _mla_attn_prompt_kernel(q_ref, k_ref, wuv_ref, y_ref, m_ref, l_ref, acc_ref, *, tq, tk):
    i, j = pl.program_id(1), pl.program_id(2)
    rows = MLA_H * tq
    last = (i * tq + tq - 1) // tk

    @pl.when(j == 0)
    def _():
        _softmax_init(m_ref, l_ref, acc_ref)

    def step(diagonal):
        k = k_ref[...]
        s = _dot_nt(q_ref[...].reshape(rows, MLA_QK), k)
        if diagonal:
            s = jnp.where(j * tk + _iota((rows, tk), 1) <= i * tq + _iota((rows, tk), 0) % tq, s, NEG)
        _softmax_step(s, k[:, :MLA_KVL], m_ref, l_ref, acc_ref)

    @pl.when(j < last)
    def _():
        step(False)

    @pl.when(j == last)
    def _():
        step(True)
        o = acc_ref[...] / _lane_wide(l_ref[...], MLA_KVL)
        for h in range(MLA_H):
            y_ref[:, h * MLA_V:(h + 1) * MLA_V] = _dot(o[h * tq:(h + 1) * tq], wuv_ref[h]).astype(y_ref.dtype)


def _mla_attn_sample_kernel(pt_ref, cache_c_ref, cache_rt_ref, q_ref, ckv_ref, krope_ref, wuv_ref, y_ref,
                            kc_buf, krt_buf, sem, *, layer, n_pages, page, tq):
    b = pl.program_id(0)
    nb = pl.num_programs(0)
    rows = MLA_H * tq

    def page_copies(seq, slot, pg):
        pid = pt_ref[seq, pg]
        return (pltpu.make_async_copy(cache_c_ref.at[layer, pid], kc_buf.at[slot, pl.ds(pg * page, page), :],
                                      sem.at[0, slot]),
                pltpu.make_async_copy(cache_rt_ref.at[layer, pid], krt_buf.at[slot, :, pl.ds(pg * page, page)],
                                      sem.at[1, slot]))

    def start_gather(seq, slot):
        def body(k, carry):
            for prio in range(2):
                for cp in page_copies(seq, slot, 2 * k + prio):
                    cp.start(priority=prio)
            return carry
        lax.fori_loop(0, n_pages // 2, body, 0)

    def wait_gather(seq, slot):
        def body(pg, carry):
            for cp in page_copies(seq, slot, pg):
                cp.wait()
            return carry
        lax.fori_loop(0, n_pages, body, 0)

    slot = b % 2

    @pl.when(b == 0)
    def _():
        start_gather(0, 0)

    @pl.when(b + 1 < nb)
    def _():
        start_gather(b + 1, 1 - slot)

    wait_gather(b, slot)

    qcat = q_ref[...].reshape(rows, MLA_QK)
    q, qr = qcat[:, :MLA_KVL], qcat[:, MLA_KVL:MLA_KVL + MLA_ROPE]
    kc = kc_buf[slot].astype(BF16)
    s_old = _dot_nt(q, kc) + _dot(qr, krt_buf[slot])
    kc_new, kr_new = ckv_ref[...], krope_ref[...]
    s_new = _dot_nt(q, kc_new) + _dot_nt(qr, kr_new)
    s_new = jnp.where(_iota((rows, tq), 1) <= _iota((rows, tq), 0) % tq, s_new, NEG)
    m = jnp.maximum(jnp.max(s_old, axis=-1, keepdims=True), jnp.max(s_new, axis=-1, keepdims=True))
    p_old, p_new = jnp.exp(s_old - m), jnp.exp(s_new - m)
    denom = jnp.sum(p_old, axis=-1, keepdims=True) + jnp.sum(p_new, axis=-1, keepdims=True)
    o = (_dot(p_old, kc) + _dot(p_new, kc_new)) / denom
    for h in range(MLA_H):
        y_ref[:, h * MLA_V:(h + 1) * MLA_V] = _dot(o[h * tq:(h + 1) * tq], wuv_ref[h]).astype(y_ref.dtype)


def _rope_tables(pos):
    inv = ROPE_THETA ** (-jnp.arange(ROPE_HALF, dtype=F32) * 2.0 / MLA_ROPE)
    ang = pos.astype(F32)[:, None] * inv[None, :]
    reps = LANE // ROPE_HALF
    return jnp.tile(jnp.cos(ang), (1, reps)), jnp.tile(jnp.sin(ang), (1, reps))


def _mla_call(p, b, t, pos0, q_norm_g, w_uq, kv_norm_g, w_ukv, cache_c, cache_r, page_table, layer, out_dtype):
    m = b * t
    tm = min(m, 256)
    wq = w_uq.reshape(MLA_QL, MLA_H, MLA_NOPE + MLA_ROPE)
    wq = jnp.concatenate([wq[..., :MLA_NOPE].reshape(MLA_QL, -1), wq[..., MLA_NOPE:].reshape(MLA_QL, -1)], axis=1)
    wkv = w_ukv.reshape(MLA_KVL, MLA_H, MLA_NOPE + MLA_V)
    wukt = jnp.transpose(wkv[..., :MLA_NOPE], (1, 2, 0)).astype(BF16)
    wuv = jnp.transpose(wkv[..., MLA_NOPE:], (1, 0, 2)).astype(BF16)
    cos, sin = _rope_tables(pos0 + jnp.arange(t))
    if t < tm:
        cos, sin = jnp.tile(cos, (tm // t, 1)), jnp.tile(sin, (tm // t, 1))
    ntab = cos.shape[0] // tm
    qdt = BF16 if t >= tm else F32
    const2 = lambda i: (0, 0)
    qcat, ckv, krope, kcat = pl.pallas_call(
        _mla_prep_kernel,
        grid=(m // tm,),
        in_specs=[pl.BlockSpec((tm, MLA_QL), lambda i: (i, P_QD // MLA_QL)),
                  pl.BlockSpec((tm, MLA_KVL), lambda i: (i, P_KVD // MLA_KVL)),
                  pl.BlockSpec((tm, LANE), lambda i: (i, P_KR // LANE)),
                  pl.BlockSpec((tm, LANE), lambda i: (i % ntab, 0)),
                  pl.BlockSpec((tm, LANE), lambda i: (i % ntab, 0)),
                  pl.BlockSpec((1, MLA_QL), const2),
                  pl.BlockSpec((MLA_QL, MLA_H * (MLA_NOPE + MLA_ROPE)), const2),
                  pl.BlockSpec((1, MLA_KVL), const2),
                  pl.BlockSpec((MLA_H, MLA_NOPE, MLA_KVL), lambda i: (0, 0, 0))],
        out_specs=[pl.BlockSpec((MLA_H, tm, MLA_QK), lambda i: (0, i, 0)),
                   pl.BlockSpec((tm, MLA_KVL), lambda i: (i, 0)),
                   pl.BlockSpec((tm, MLA_ROPE), lambda i: (i, 0)),
                   pl.BlockSpec((tm, MLA_QK), lambda i: (i, 0))],
        out_shape=[jax.ShapeDtypeStruct((MLA_H, m, MLA_QK), qdt),
                   jax.ShapeDtypeStruct((m, MLA_KVL), F32),
                   jax.ShapeDtypeStruct((m, MLA_ROPE), F32),
                   jax.ShapeDtypeStruct((m, MLA_QK), BF16)],
        compiler_params=_cparams(("parallel",)),
        name="mla_prep",
    )(p, p, p, cos, sin, q_norm_g.reshape(1, -1), wq.astype(BF16), kv_norm_g.reshape(1, -1), wukt)
    rows = MLA_H * min(t, 256)
    softmax_scratch = [pltpu.VMEM((rows, LANE), F32), pltpu.VMEM((rows, LANE), F32), pltpu.VMEM((rows, MLA_KVL), F32)]

    if cache_c is None:
        tq = min(t, 256)
        tk = min(t, 512)
        nq, nk = t // tq, t // tk
        last = lambda i: (i * tq + tq - 1) // tk
        y = pl.pallas_call(
            functools.partial(_mla_attn_prompt_kernel, tq=tq, tk=tk),
            grid=(b, nq, nk),
            in_specs=[pl.BlockSpec((MLA_H, tq, MLA_QK), lambda bi, i, j: (0, bi * nq + i, 0)),
                      pl.BlockSpec((tk, MLA_QK), lambda bi, i, j: (bi * nk + jnp.minimum(j, last(i)), 0)),
                      pl.BlockSpec((MLA_H, MLA_KVL, MLA_V), lambda bi, i, j: (0, 0, 0))],
            out_specs=pl.BlockSpec((tq, MLA_H * MLA_V), lambda bi, i, j: (bi * nq + i, 0)),
            out_shape=jax.ShapeDtypeStruct((m, MLA_H * MLA_V), out_dtype),
            scratch_shapes=softmax_scratch,
            compiler_params=_cparams(("parallel", "parallel", "arbitrary")),
            name="mla_attn_prompt",
        )(qcat, kcat, wuv)
        return y, ckv, krope

    n_pages = page_table.shape[1]
    assert n_pages % 2 == 0, "the page gather issues pages in pairs"
    page = cache_c.shape[2]
    cache_rt = jnp.swapaxes(cache_r, 2, 3)
    y = pl.pallas_call(
        functools.partial(_mla_attn_sample_kernel, layer=layer, n_pages=n_pages, page=page, tq=t),
        grid_spec=pltpu.PrefetchScalarGridSpec(
            num_scalar_prefetch=1,
            grid=(b,),
            in_specs=[pl.BlockSpec(memory_space=pl.ANY), pl.BlockSpec(memory_space=pl.ANY),
                      pl.BlockSpec((MLA_H, t, MLA_QK), lambda bi, pt: (0, bi, 0)),
                      pl.BlockSpec((t, MLA_KVL), lambda bi, pt: (bi, 0)),
                      pl.BlockSpec((t, MLA_ROPE), lambda bi, pt: (bi, 0)),
                      pl.BlockSpec((MLA_H, MLA_KVL, MLA_V), lambda bi, pt: (0, 0, 0))],
            out_specs=pl.BlockSpec((t, MLA_H * MLA_V), lambda bi, pt: (bi, 0)),
            scratch_shapes=[pltpu.VMEM((2, n_pages * page, MLA_KVL), F32),
                            pltpu.VMEM((2, MLA_ROPE, n_pages * page), F32),
                            pltpu.SemaphoreType.DMA((2, 2))]),
        out_shape=jax.ShapeDtypeStruct((m, MLA_H * MLA_V), out_dtype),
        compiler_params=_cparams(("arbitrary",)),
        name="mla_attn_sample",
    )(page_table, cache_c, cache_rt, qcat, ckv, krope, wuv)
    return y, ckv, krope


ROUTER_GRP_LANE = MOE_E


def _merge_kernel(gl_ref, ya_ref, yb_ref, yc_ref, yd_ref, wb_ref, wo_ref, x_ref, mod_ref, ng_ref, wr_ref, br_ref,
                  xn_ref, h2_ref, lg_ref):
    bb, tt, d = x_ref.shape
    merged = None
    for n, y_ref in enumerate((ya_ref, yb_ref, yc_ref, yd_ref)):
        term = _sigmoid(gl_ref[:, n * d:(n + 1) * d]) * _dot(y_ref[...], wb_ref[n])
        merged = term if merged is None else merged + term
    out = _dot(merged, wo_ref[...])
    mod = mod_ref[...]
    x = x_ref[...] + mod[:, 2:3, :] * out.reshape(bb, tt, d)
    xn_ref[...] = x
    h = x * lax.rsqrt(jnp.mean(x * x, axis=-1, keepdims=True) + RMS_EPS) * ng_ref[...]
    h = (h * (1.0 + mod[:, 4:5, :]) + mod[:, 3:4, :]).reshape(bb * tt, d)
    h2_ref[...] = h.astype(h2_ref.dtype)
    lg_ref[...] = _dot_hi(h, wr_ref[...]) + br_ref[...]


def _merge_call(p, ys, w_branch, w_out, x, mod, norm2_g, w_router, b_router):
    b, t, d = x.shape
    bb, tt = _row_blocks(b, t, 512)
    nt = t // tt
    tm = bb * tt
    row = lambda i, j: (i * nt + j, 0)
    yspec = pl.BlockSpec((tm, ys[0].shape[1]), row)
    return pl.pallas_call(
        _merge_kernel,
        grid=(b // bb, nt),
        in_specs=[pl.BlockSpec((tm, N_BRANCH * d), row), yspec, yspec, yspec, yspec,
                  pl.BlockSpec(w_branch.shape, lambda i, j: (0, 0, 0)),
                  pl.BlockSpec(w_out.shape, lambda i, j: (0, 0)),
                  pl.BlockSpec((bb, tt, d), lambda i, j: (i, j, 0)),
                  pl.BlockSpec((bb, mod.shape[1], d), lambda i, j: (i, 0, 0)),
                  pl.BlockSpec((1, 1, d), lambda i, j: (0, 0, 0)),
                  pl.BlockSpec((d, LANE), lambda i, j: (0, 0)),
                  pl.BlockSpec((1, LANE), lambda i, j: (0, 0))],
        out_specs=[pl.BlockSpec((bb, tt, d), lambda i, j: (i, j, 0)),
                   pl.BlockSpec((tm, d), row),
                   pl.BlockSpec((tm, LANE), row)],
        out_shape=[jax.ShapeDtypeStruct((b, t, d), F32),
                   jax.ShapeDtypeStruct((b * t, d), BF16),
                   jax.ShapeDtypeStruct((b * t, LANE), F32)],
        compiler_params=_cparams(("parallel", "parallel")),
        name="merge",
    )(p, *ys, w_branch, w_out, x, mod, norm2_g.reshape(1, 1, d), w_router, b_router)


def _route(logits):
    lane = _iota(logits.shape, 1)
    big = jnp.int32(1 << 20)
    grp = jnp.where(jnp.right_shift(lane, 2) == ROUTER_GRP_LANE // MOE_G, logits, NEG)
    gmax = jnp.max(grp, axis=-1, keepdims=True)
    g_top = 1.0 / jnp.sum(jnp.exp(grp - gmax), axis=-1, keepdims=True)
    gidx = jnp.min(jnp.where(grp == gmax, lane, big), axis=-1, keepdims=True) - ROUTER_GRP_LANE
    el = jnp.where(jnp.right_shift(lane, 3) == gidx, logits, NEG)
    m1 = jnp.max(el, axis=-1, keepdims=True)
    i1 = jnp.min(jnp.where(el == m1, lane, big), axis=-1, keepdims=True)
    el2 = jnp.where(lane == i1, NEG, el)
    m2 = jnp.max(el2, axis=-1, keepdims=True)
    i2 = jnp.min(jnp.where(el2 == m2, lane, big), axis=-1, keepdims=True)
    e2 = jnp.exp(m2 - m1)
    w1 = g_top / (1.0 + e2)
    return jnp.where(lane == i1, w1, 0.0) + jnp.where(lane == i2, w1 * e2, 0.0)


def _moe_dense_kernel(h_ref, lg_ref, wg_ref, wu_ref, wd_ref, x_ref, mod_ref, o_ref, comb_ref, acc_ref):
    e = pl.program_id(1)
    ne = pl.num_programs(1)
    bb, tt, d = x_ref.shape

    @pl.when(e == 0)
    def _():
        comb_ref[...] = _route(lg_ref[...])
        acc_ref[...] = jnp.zeros(acc_ref.shape, F32)

    comb = comb_ref[...]
    ce = jnp.sum(jnp.where(_iota(comb.shape, 1) == e, comb, 0.0), axis=-1, keepdims=True)
    h = h_ref[...]
    hid = _silu(_dot(h, wg_ref[...])) * _dot(h, wu_ref[...]) * ce
    acc_ref[...] += _dot(hid, wd_ref[...])

    @pl.when(e == ne - 1)
    def _():
        o_ref[...] = x_ref[...] + mod_ref[...][:, 5:6, :] * acc_ref[...].reshape(bb, tt, d)


def _moe_call(h2, logits, w_gate, w_up, w_down, x, mod):
    b, t, d = x.shape
    bb, tt = _row_blocks(b, t, 1024)
    nt = t // tt
    tm = bb * tt
    ne, _, hid = w_gate.shape
    row = lambda i, e: (i, 0)
    xmap = lambda i, e: (i // nt, i % nt, 0)
    return pl.pallas_call(
        _moe_dense_kernel,
        grid=(b * t // tm, ne),
        in_specs=[pl.BlockSpec((tm, d), row), pl.BlockSpec((tm, LANE), row),
                  pl.BlockSpec((None, d, hid), lambda i, e: (e, 0, 0)),
                  pl.BlockSpec((None, d, hid), lambda i, e: (e, 0, 0)),
                  pl.BlockSpec((None, hid, d), lambda i, e: (e, 0, 0)),
                  pl.BlockSpec((bb, tt, d), xmap),
                  pl.BlockSpec((bb, mod.shape[1], d), lambda i, e: (i // nt, 0, 0))],
        out_specs=pl.BlockSpec((bb, tt, d), xmap),
        out_shape=jax.ShapeDtypeStruct((b, t, d), F32),
        scratch_shapes=[pltpu.VMEM((tm, LANE), F32), pltpu.VMEM((tm, d), F32)],
        compiler_params=_cparams(("parallel", "arbitrary")),
        name="moe",
    )(h2, logits, w_gate, w_up, w_down, x, mod)


MOE_TM = 1024
MOE_RB = 64
MOE_SLOTS = 4096
MOE_SC = 512
MOE_EPS = 4
META_S1, META_S2, META_W1, META_W2 = 0, 1, 2, 3


def _moe_route_kernel(lg_ref, meta_ref, metat_ref, nblk_ref, boff_ref):
    comb = _route(lg_ref[...])
    tm = comb.shape[0]
    lane = _iota(comb.shape, 1)
    big = jnp.int32(1 << 20)
    chosen = comb > 0.0
    a = jnp.where(chosen, 1.0, 0.0)
    rank = _dot((_iota((tm, tm), 0) > _iota((tm, tm), 1)).astype(F32), a)
    cnt = jnp.sum(a, axis=0, keepdims=True)
    nblk = jnp.floor((cnt + (MOE_RB - 1)) * (1.0 / MOE_RB))
    upper = (_iota((LANE, LANE), 0) < _iota((LANE, LANE), 1)).astype(F32)
    boff = _dot_hi(jnp.broadcast_to(nblk, (8, LANE)), upper)[0:1]
    slot = boff * MOE_RB + rank
    i1 = jnp.min(jnp.where(chosen, lane, big), axis=-1, keepdims=True)
    i2 = jnp.max(jnp.where(chosen, lane, -1), axis=-1, keepdims=True)
    pick = lambda v, i: jnp.sum(jnp.where(lane == i, v, 0.0), axis=-1, keepdims=True)
    two = i2 != i1
    s1, w1 = pick(slot, i1), pick(comb, i1)
    s2, w2 = jnp.where(two, pick(slot, i2), -1.0), jnp.where(two, pick(comb, i2), 0.0)
    meta = (jnp.where(lane == META_S1, s1, 0.0) + jnp.where(lane == META_S2, s2, 0.0)
            + jnp.where(lane == META_W1, w1, 0.0) + jnp.where(lane == META_W2, w2, 0.0))
    meta_ref[...] = meta
    metat_ref[...] = meta.T[0:8, :]
    nblk_ref[0] = nblk.astype(jnp.int32)
    boff_ref[0] = boff.astype(jnp.int32)


def _moe_sorted_kernel(nblk_ref, boff_ref, h_ref, meta_ref, metat_ref, wg_ref, wu_ref, wd_ref, x_ref, mod_ref,
                       o_ref, xs_ref, ys_ref):
    i, eg = pl.program_id(0), pl.program_id(1)
    neg = pl.num_programs(1)
    eps = wg_ref.shape[0]
    bb, tt, d = x_ref.shape
    tm = bb * tt
    last = i * LANE + neg * eps - 1
    used = (boff_ref[last] + nblk_ref[last]) * MOE_RB

    @pl.when(eg == 0)
    def _():
        s1 = metat_ref[META_S1:META_S1 + 1, :]
        s2 = metat_ref[META_S2:META_S2 + 1, :]
        h = h_ref[...]
        for c in range(MOE_SLOTS // MOE_SC):
            @pl.when(c * MOE_SC < used)
            def _():
                sl = (c * MOE_SC + _iota((MOE_SC, tm), 0)).astype(F32)
                perm = jnp.where((sl == s1) | (sl == s2), 1.0, 0.0)
                xs_ref[c * MOE_SC:(c + 1) * MOE_SC, :] = _dot(perm, h).astype(BF16)
        ys_ref[...] = jnp.zeros(ys_ref.shape, BF16)

    def expert_rows(j, first_block, n_rows):
        rows = pl.ds(pl.multiple_of(first_block * MOE_RB, MOE_RB), n_rows)
        xb = xs_ref[rows, :]
        hid = _silu(_dot(xb, wg_ref[j])) * _dot(xb, wu_ref[j])
        ys_ref[rows, :] = _dot(hid, wd_ref[j]).astype(BF16)

    for j in range(eps):
        e = i * LANE + eg * eps + j
        b0, nb = boff_ref[e], nblk_ref[e]

        def pair(k, carry, j=j, b0=b0):
            expert_rows(j, b0 + 2 * k, 2 * MOE_RB)
            return carry

        lax.fori_loop(0, lax.shift_right_logical(nb, 1), pair, 0)

        @pl.when(lax.bitwise_and(nb, 1) == 1)
        def _(j=j, b0=b0, nb=nb):
            expert_rows(j, b0 + nb - 1, MOE_RB)

    @pl.when(eg == neg - 1)
    def _():
        meta = meta_ref[...]
        s1, s2 = meta[:, META_S1:META_S1 + 1], meta[:, META_S2:META_S2 + 1]
        w1, w2 = meta[:, META_W1:META_W1 + 1], meta[:, META_W2:META_W2 + 1]
        o_ref[...] = jnp.zeros(o_ref.shape, F32)
        for c in range(MOE_SLOTS // MOE_SC):
            @pl.when(c * MOE_SC < used)
            def _():
                sl = (c * MOE_SC + _iota((tm, MOE_SC), 1)).astype(F32)
                back = jnp.where(sl == s1, w1, 0.0) + jnp.where(sl == s2, w2, 0.0)
                o_ref[...] += _dot(back, ys_ref[c * MOE_SC:(c + 1) * MOE_SC, :]).reshape(bb, tt, d)
        o_ref[...] = x_ref[...] + mod_ref[...][:, 5:6, :] * o_ref[...]


def _moe_sorted_call(h2, logits, w_gate, w_up, w_down, x, mod):
    b, t, d = x.shape
    tm = MOE_TM
    nt = t // tm
    n_tiles = b * nt
    ne, _, hid = w_gate.shape
    meta, metat, nblk, boff = pl.pallas_call(
        _moe_route_kernel,
        grid=(n_tiles,),
        in_specs=[pl.BlockSpec((tm, LANE), lambda i: (i, 0))],
        out_specs=[pl.BlockSpec((tm, LANE), lambda i: (i, 0)),
                   pl.BlockSpec((8, tm), lambda i: (i, 0)),
                   pl.BlockSpec((1, 1, LANE), lambda i: (i, 0, 0)),
                   pl.BlockSpec((1, 1, LANE), lambda i: (i, 0, 0))],
        out_shape=[jax.ShapeDtypeStruct((n_tiles * tm, LANE), F32),
                   jax.ShapeDtypeStruct((n_tiles * 8, tm), F32),
                   jax.ShapeDtypeStruct((n_tiles, 1, LANE), jnp.int32),
                   jax.ShapeDtypeStruct((n_tiles, 1, LANE), jnp.int32)],
        compiler_params=_cparams(("parallel",)),
        name="moe_route",
    )(logits)
    xmap = lambda i, e, nb, bo: (i // nt, i % nt, 0)
    row = lambda i, e, nb, bo: (i, 0)
    wmap = lambda i, e, nb, bo: (e, 0, 0)
    return pl.pallas_call(
        _moe_sorted_kernel,
        grid_spec=pltpu.PrefetchScalarGridSpec(
            num_scalar_prefetch=2,
            grid=(n_tiles, ne // MOE_EPS),
            in_specs=[pl.BlockSpec((tm, d), row), pl.BlockSpec((tm, LANE), row), pl.BlockSpec((8, tm), row),
                      pl.BlockSpec((MOE_EPS, d, hid), wmap), pl.BlockSpec((MOE_EPS, d, hid), wmap),
                      pl.BlockSpec((MOE_EPS, hid, d), wmap),
                      pl.BlockSpec((1, tm, d), xmap),
                      pl.BlockSpec((1, mod.shape[1], d), lambda i, e, nb, bo: (i // nt, 0, 0))],
            out_specs=pl.BlockSpec((1, tm, d), xmap),
            scratch_shapes=[pltpu.VMEM((MOE_SLOTS, d), BF16), pltpu.VMEM((MOE_SLOTS, d), BF16)]),
        out_shape=jax.ShapeDtypeStruct((b, t, d), F32),
        compiler_params=_cparams(("parallel", "arbitrary")),
        name="moe_sorted",
    )(nblk.reshape(-1), boff.reshape(-1), h2, meta, metat, w_gate, w_up, w_down, x, mod)


def _layer(x, mod, pos0, lw, state, cache, out_dtype):
    b, t, d = x.shape
    ssm0, ssm_conv0, hg0, lru0, lru_conv0 = state
    h = _prenorm_call(x, lw['norm1_g'], mod, sh_row=0, sc_row=1)
    p = _matmul_call(h, lw['w_in'])
    y_a, ssm1, ssm_conv1 = _ssd_call(p, b, t, lw['ssd_conv_w'], lw['ssd_conv_b'], lw['ssd_dt_bias'], lw['ssd_a_log'],
                                     lw['ssd_d'], lw['ssd_norm_g'], ssm0, ssm_conv0, out_dtype)
    y_b, hg1 = _hgrn_call(p, b, t, lw['hg_lb'], lw['hg_norm_g'], hg0, out_dtype)
    y_c, ckv, krope = _mla_call(p, b, t, pos0, lw['mla_q_norm_g'], lw['mla_w_uq'], lw['mla_kv_norm_g'],
                                lw['mla_w_ukv'], *cache, out_dtype)
    y_d, lru1, lru_conv1 = _lru_call(p, b, t, pos0, lw['lru_conv_w'], lw['lru_conv_b'], lw['lru_w_r'], lw['lru_b_r'],
                                     lw['lru_w_i'], lw['lru_b_i'], lw['lru_a'], lru0, lru_conv0, out_dtype)
    x, h2, logits = _merge_call(p, (y_a, y_b, y_c, y_d), lw['w_branch'], lw['w_out'], x, mod, lw['norm2_g'],
                                lw['w_router'], lw['b_router'])
    moe = _moe_sorted_call if t % MOE_TM == 0 else _moe_call
    x = moe(h2, logits, lw['moe_w_gate'], lw['moe_w_up'], lw['moe_w_down'], x, mod)
    new =(ckv.reshape(b, t, -1), krope.reshape(b, t, -1), ssm1, ssm_conv1, hg1, lru1, lru_conv1)
    return x, new


def kernel(x_prompt, x_sample, c_prompt, c_sample, cache_kv_latent, cache_k_rope, state_ssm, state_ssm_conv,
           state_hgrn, state_lru, state_lru_conv, page_table, norm1_g, norm2_g, w_mod, b_mod, w_in, ssd_conv_w,
           ssd_conv_b, ssd_dt_bias, ssd_a_log, ssd_d, ssd_norm_g, hg_lb_raw, hg_norm_g, mla_q_norm_g, mla_w_uq,
           mla_kv_norm_g, mla_w_ukv, lru_conv_w, lru_conv_b, lru_w_r, lru_b_r, lru_w_i, lru_b_i, lru_a, w_branch,
           w_out, moe_w_grp, moe_b_grp, moe_w_rt, moe_b_rt, moe_w_gate, moe_w_up, moe_w_down, final_norm_g):
    bp, tp, d = x_prompt.shape
    bs, ts, _ = x_sample.shape
    depth = w_in.shape[0]
    n_past = page_table.shape[1] * cache_kv_latent.shape[2]
    lb_all = jnp.cumsum(jax.nn.softmax(hg_lb_raw.astype(F32), axis=0), axis=0)
    lb_all = lb_all - lb_all[:1]
    c_all = jnp.concatenate([c_prompt, c_sample], axis=0)
    zeros = lambda *s: jnp.zeros(s, F32)
    yp, ys = x_prompt, x_sample
    p_new, s_new = [], []
    for l in range(depth):
        mod = _mod_call(c_all, w_mod[l].astype(BF16), b_mod[l]).reshape(bp + bs, 6, d)
        pad_r = LANE - MOE_E - MOE_G
        lw = {
            'norm1_g': norm1_g[l], 'norm2_g': norm2_g[l], 'w_in': _pack_w_in(w_in[l]),
            'ssd_conv_w': ssd_conv_w[l], 'ssd_conv_b': ssd_conv_b[l], 'ssd_dt_bias': ssd_dt_bias[l],
            'ssd_a_log': ssd_a_log[l], 'ssd_d': ssd_d[l], 'ssd_norm_g': ssd_norm_g[l],
            'hg_lb': lb_all[l], 'hg_norm_g': hg_norm_g[l],
            'mla_q_norm_g': mla_q_norm_g[l], 'mla_w_uq': mla_w_uq[l],
            'mla_kv_norm_g': mla_kv_norm_g[l], 'mla_w_ukv': mla_w_ukv[l],
            'lru_conv_w': lru_conv_w[l], 'lru_conv_b': lru_conv_b[l], 'lru_w_r': lru_w_r[l], 'lru_b_r': lru_b_r[l],
            'lru_w_i': lru_w_i[l], 'lru_b_i': lru_b_i[l], 'lru_a': lru_a[l],
            'w_branch': w_branch[l].astype(BF16), 'w_out': w_out[l].astype(BF16),
            'w_router': jnp.pad(jnp.concatenate([moe_w_rt[l], moe_w_grp[l]], axis=1), ((0, 0), (0, pad_r))),
            'b_router': jnp.pad(jnp.concatenate([moe_b_rt[l], moe_b_grp[l]]), (0, pad_r)).reshape(1, LANE),
            'moe_w_gate': moe_w_gate[l].astype(BF16), 'moe_w_up': moe_w_up[l].astype(BF16),
            'moe_w_down': moe_w_down[l].astype(BF16),
        }
        p_state = (zeros(bp, SSD_H, SSD_P, SSD_N), zeros(bp, CONV_W - 1, SSD_CONV), zeros(bp, HG_H, HG_K, HG_V),
                   zeros(bp, LRU_W), zeros(bp, CONV_W - 1, LRU_W))
        s_state = (state_ssm[l], state_ssm_conv[l], state_hgrn[l], state_lru[l], state_lru_conv[l])
        yp, pn = _layer(yp, mod[:bp], 0, lw, p_state, (None, None, None, l), BF16)
        ys, sn = _layer(ys, mod[bp:], n_past, lw, s_state, (cache_kv_latent, cache_k_rope, page_table, l), F32)
        p_new.append(pn)
        s_new.append(sn)
    no_mod = zeros(1, 2, d)
    yp = _prenorm_call(yp, final_norm_g, jnp.broadcast_to(no_mod, (bp, 2, d)), 0, 1, F32).reshape(bp, tp, d)
    ys = _prenorm_call(ys, final_norm_g, jnp.broadcast_to(no_mod, (bs, 2, d)), 0, 1, F32).reshape(bs, ts, d)
    stk = lambda news, j: jnp.stack([n[j] for n in news])
    return (yp, ys) + tuple(stk(p_new, j) for j in range(7)) + tuple(stk(s_new, j) for j in range(7))
```
